```python
import math
import jax
import jax.numpy as jnp
from jax import lax
import numpy as np

D_MODEL = 2048
BATCH = 16
SEQ = 2048
DEPTH = 4

CTX_LEN = 256
GRID_W = 64
N_MIXERS = 3
RMS_EPS = 1e-6
N_MOD = 6

ATT_HEADS = 32
ATT_KV_HEADS = 4
ATT_GQA = ATT_HEADS // ATT_KV_HEADS
ATT_HEAD_DIM = 64
ATT_WINDOW = 128
ATT_BLOCK = 128
ATT_Q_DIM = ATT_HEADS * ATT_HEAD_DIM
ATT_KV_DIM = ATT_KV_HEADS * ATT_HEAD_DIM
ATT_IN = 2 * ATT_KV_DIM + ATT_Q_DIM
ROPE_THETA = 10000.0
ROPE_PAIRS = ATT_HEAD_DIM // 4
NEG_INF = -1e30

SSD_D_INNER = 2 * D_MODEL
SSD_HEAD_DIM = 64
SSD_HEADS = SSD_D_INNER // SSD_HEAD_DIM
SSD_GROUPS = 8
SSD_STATE = 128
SSD_GN = SSD_GROUPS * SSD_STATE
SSD_CONV_W = 3
SSD_CHUNK = 64
SSD_CONV_DIM = SSD_D_INNER + 2 * SSD_GN
SSD_IN = SSD_CONV_DIM + 2 * SSD_HEADS + SSD_D_INNER

HG_EXPAND = 128
HG_HEADS = D_MODEL // HG_EXPAND
HG_KEY = HG_HEADS * HG_EXPAND
HG_HEAD_V = D_MODEL // HG_HEADS
HG_VAL = HG_HEADS * HG_HEAD_V
HG_CHUNK = 64
HG_IN = 2 * HG_KEY + HG_VAL + HG_KEY + HG_VAL

FFN_DIM = -(-8 * D_MODEL // (3 * 256)) * 256

kernel_name = 'hybrid_interleaved_dit_trunk'


def _rms(t):
    t32 = t.astype(jnp.float32)
    return (t32 * lax.rsqrt(jnp.mean(t32 * t32, axis=-1, keepdims=True) + RMS_EPS)).astype(t.dtype)


def rms_norm(t, g):
    return _rms(t) * g


def modulate(t, g, shift, scale):
    return rms_norm(t, g) * (1.0 + scale) + shift


def _flip(t):
    return jnp.flip(t, axis=1)


def swiglu(t, w_in, w_out):
    gu = t @ w_in
    return (jax.nn.silu(gu[..., :FFN_DIM]) * gu[..., FFN_DIM:]) @ w_out


def axial_rope_tables(n_tokens):
    rows = n_tokens // GRID_W
    row = jnp.broadcast_to(jnp.arange(rows)[:, None], (rows, GRID_W)).reshape(-1).astype(jnp.float32)
    col = jnp.broadcast_to(jnp.arange(GRID_W)[None, :], (rows, GRID_W)).reshape(-1).astype(jnp.float32)
    inv_freq = ROPE_THETA ** (-jnp.arange(ROPE_PAIRS, dtype=jnp.float32) / ROPE_PAIRS)
    ang = jnp.stack([row[:, None] * inv_freq, col[:, None] * inv_freq], axis=1)
    return jnp.cos(ang), jnp.sin(ang)


def apply_axial_rope(t, cos, sin):
    shp = t.shape
    tt = t.reshape(shp[:-1] + (2, 2, ROPE_PAIRS))
    bshape = (shp[1],) + (1,) * (len(shp) - 3) + (2, ROPE_PAIRS)
    cs = cos.reshape(bshape).astype(t.dtype)
    sn = sin.reshape(bshape).astype(t.dtype)
    t1, t2 = tt[..., 0, :], tt[..., 1, :]
    return jnp.stack([t1 * cs - t2 * sn, t2 * cs + t1 * sn], axis=-2).reshape(shp)


def sink_softmax(scores, sink):
    sink_col = jnp.broadcast_to(sink[:, :, None, None], scores.shape[:-1] + (1,))
    return jax.nn.softmax(jnp.concatenate([sink_col, scores], axis=-1), axis=-1)[..., 1:]


def key_band(t, nb):
    b = t.shape[0]
    tb = t.reshape((b, nb, ATT_BLOCK) + t.shape[2:])
    tp = jnp.pad(tb, ((0, 0), (1, 1), (0, 0), (0, 0), (0, 0)))
    return jnp.concatenate([tp[:, :-2], tp[:, 1:-1], tp[:, 2:]], axis=2)


def banded_window_attention(q, k, v, k_c, v_c, sink):
    b, s = q.shape[:2]
    nb = s // ATT_BLOCK
    scale = ATT_HEAD_DIM ** -0.5
    qb = q.reshape(b, nb, ATT_BLOCK, ATT_KV_HEADS, ATT_GQA, ATT_HEAD_DIM)
    kb, vb = key_band(k, nb), key_band(v, nb)
    s_loc = jnp.einsum('bnqkgd,bnjkd->bnkgqj', qb, kb).astype(jnp.float32) * scale
    s_ctx = jnp.einsum('bnqkgd,bmkd->bnkgqm', qb, k_c).astype(jnp.float32) * scale
    r = jnp.arange(ATT_BLOCK)
    cidx = jnp.arange(3 * ATT_BLOCK)
    blk = jnp.arange(nb)
    in_window = jnp.abs(r[:, None] + ATT_BLOCK - cidx[None, :]) <= ATT_WINDOW
    kpos = (blk[:, None] - 1) * ATT_BLOCK + cidx[None, :]
    valid = in_window[None] & ((kpos >= 0) & (kpos < s))[:, None, :]
    s_loc = jnp.where(valid[None, :, None, None], s_loc, NEG_INF)
    p = sink_softmax(jnp.concatenate([s_loc, s_ctx], axis=-1), sink).astype(v.dtype)
    n_loc = 3 * ATT_BLOCK
    o = (jnp.einsum('bnkgqj,bnjkd->bnqkgd', p[..., :n_loc], vb)
         + jnp.einsum('bnkgqm,bmkd->bnqkgd', p[..., n_loc:], v_c))
    return o.reshape(b, s, ATT_Q_DIM)


def context_attention(q_c, k_c, v_c, sink):
    b, l = q_c.shape[:2]
    scores = jnp.einsum('blkgd,bmkd->bkglm', q_c, k_c).astype(jnp.float32) * (ATT_HEAD_DIM ** -0.5)
    p = sink_softmax(scores, sink).astype(v_c.dtype)
    return jnp.einsum('bkglm,bmkd->blkgd', p, v_c).reshape(b, l, ATT_Q_DIM)


def attention_mixer(u, u_c, w_in, w_out, sink, rope_cos, rope_sin, need_ctx):
    b, s, _ = u.shape
    lc = u_c.shape[1]
    kvh, dh = ATT_KV_HEADS, ATT_HEAD_DIM
    proj = u @ w_in
    k = apply_axial_rope(proj[..., :ATT_KV_DIM].reshape(b, s, kvh, dh), rope_cos, rope_sin)
    v = proj[..., ATT_KV_DIM:2 * ATT_KV_DIM].reshape(b, s, kvh, dh)
    q = apply_axial_rope(proj[..., 2 * ATT_KV_DIM:].reshape(b, s, kvh, ATT_GQA, dh), rope_cos, rope_sin)
    proj_c = u_c @ (w_in if need_ctx else w_in[:, :2 * ATT_KV_DIM])
    k_c = proj_c[..., :ATT_KV_DIM].reshape(b, lc, kvh, dh)
    v_c = proj_c[..., ATT_KV_DIM:2 * ATT_KV_DIM].reshape(b, lc, kvh, dh)
    sink = sink.astype(jnp.float32).reshape(kvh, ATT_GQA)
    y = banded_window_attention(q, k, v, k_c, v_c, sink) @ w_out
    if not need_ctx:
        return y, None
    q_c = proj_c[..., 2 * ATT_KV_DIM:].reshape(b, lc, kvh, ATT_GQA, dh)
    return y, context_attention(q_c, k_c, v_c, sink) @ w_out


def centred_depthwise_conv(t, w, bias):
    pad = SSD_CONV_W // 2
    l = t.shape[1]
    tp = jnp.pad(t, ((0, 0), (pad, pad), (0, 0)))
    out = bias + tp[:, 0:l] * w[0]
    for tap in range(1, SSD_CONV_W):
        out = out + tp[:, tap:tap + l] * w[tap]
    return out


def ssd_chunk_scan(x, dt, a, bm, cm, s0, want_y):
    b, l, nh, p = x.shape
    g, n = bm.shape[-2:]
    e = nh // g
    nc = l // SSD_CHUNK
    q = SSD_CHUNK
    acum = jnp.cumsum((dt * a).reshape(b, nc, q, g, e), axis=2).transpose(0, 1, 3, 4, 2)
    xdt = (x * dt[..., None]).reshape(b, nc, q, g, e, p)
    bc = bm.reshape(b, nc, q, g, n)
    cc = cm.reshape(b, nc, q, g, n)
    decay_end = jnp.exp(acum[..., -1:] - acum)
    states = jnp.einsum('bcqgn,bcgeq,bcqgep->bcgepn', bc, decay_end, xdt)
    chunk_decay = jnp.exp(acum[..., -1])

    def step(s, inp):
        st, dec = inp
        return dec[..., None, None] * s + st, s

    s_fin, s_start = lax.scan(step, s0, (jnp.moveaxis(states, 1, 0), jnp.moveaxis(chunk_decay, 1, 0)))
    if not want_y:
        return None, s_fin
    tril = jnp.tril(jnp.ones((q, q), dtype=bool))
    seg = jnp.where(tril, acum[..., :, None] - acum[..., None, :], -jnp.inf)
    cb = jnp.einsum('bcign,bcjgn->bcgij', cc, bc)
    y_diag = jnp.einsum('bcgij,bcgeij,bcjgep->bcigep', cb, jnp.exp(seg), xdt)
    y_off = jnp.einsum('bcign,cbgepn,bcgei->bcigep', cc, s_start, jnp.exp(acum))
    return (y_diag + y_off).reshape(b, l, nh, p).astype(x.dtype), s_fin


def gated_group_rms_norm(y, z, w):
    b, l = y.shape[:2]
    t = y.reshape(b, l, SSD_D_INNER) * jax.nn.silu(z)
    t = _rms(t.reshape(b, l, SSD_GROUPS, SSD_D_INNER // SSD_GROUPS)).reshape(b, l, SSD_D_INNER)
    return t * w


def ssd_mixer(u, u_c, w_in, conv_w, conv_b, dt_bias, a_log, d_skip, norm_w, w_out, need_ctx):
    b = u.shape[0]
    n_state = SSD_CONV_DIM + 2 * SSD_HEADS
    a = -jnp.exp(a_log.astype(jnp.float32))

    def split_inputs(t, w):
        bt, lt, _ = t.shape
        proj = t @ w
        xbc = jax.nn.silu(centred_depthwise_conv(proj[..., :SSD_CONV_DIM], conv_w, conv_b))
        xs = xbc[..., :SSD_D_INNER].reshape(bt, lt, SSD_HEADS, SSD_HEAD_DIM)
        bm = xbc[..., SSD_D_INNER:SSD_D_INNER + SSD_GN].reshape(bt, lt, SSD_GROUPS, SSD_STATE)
        cm = xbc[..., SSD_D_INNER + SSD_GN:].reshape(bt, lt, SSD_GROUPS, SSD_STATE)
        dt = jax.nn.softplus(proj[..., SSD_CONV_DIM:n_state].astype(jnp.float32).reshape(bt, lt, 2, SSD_HEADS)
                             + dt_bias.astype(jnp.float32))
        return proj[..., n_state:], xs, bm, cm, dt

    def bidir(xs, bm, cm, dt, s_f, s_b, want_y):
        y_f, fin_f = ssd_chunk_scan(xs, dt[:, :, 0], a[0], bm, cm, s_f, want_y)
        y_b, fin_b = ssd_chunk_scan(_flip(xs), _flip(dt[:, :, 1]), a[1], _flip(bm), _flip(cm), s_b, want_y)
        y = (y_f + _flip(y_b) + xs * d_skip[:, None]) if want_y else None
        return y, fin_f, fin_b

    zero = jnp.zeros((b, SSD_GROUPS, SSD_HEADS // SSD_GROUPS, SSD_HEAD_DIM, SSD_STATE), jnp.float32)
    z_c, x_c, b_c, c_c, dt_c = split_inputs(u_c, w_in if need_ctx else w_in[:, :n_state])
    y_c, s_f, s_b = bidir(x_c, b_c, c_c, dt_c, zero, zero, need_ctx)
    z, xs, bm, cm, dt = split_inputs(u, w_in)
    y, _, _ = bidir(xs, bm, cm, dt, s_f, s_b, True)
    y = gated_group_rms_norm(y, z, norm_w) @ w_out
    if not need_ctx:
        return y, None
    return y, gated_group_rms_norm(y_c, z_c, norm_w) @ w_out


def hgrn2_lower_bounds(lb_raw):
    p = jax.nn.softmax(lb_raw.astype(jnp.float32), axis=1)
    return jnp.cumsum(p, axis=1) - p[:, :1]


def hgrn2_chunk_scan(k, v, log_g, s0, q=None):
    b, l, nh, dk = k.shape
    nc = l // HG_CHUNK

    def chunks(t):
        return t.reshape(b, nc, HG_CHUNK, nh, t.shape[-1]).transpose(1, 0, 3, 2, 4)

    tril = jnp.tril(jnp.ones((HG_CHUNK, HG_CHUNK), dtype=bool))

    def step(state, inp):
        kc, vc, gc = inp[0], inp[1], inp[2]
        gcum = jnp.cumsum(gc, axis=2)
        g_last = gcum[:, :, -1:]
        new_state = (jnp.exp(g_last[:, :, 0])[..., None] * state
                     + jnp.einsum('bhsd,bhsv->bhdv', kc * jnp.exp(g_last - gcum), vc))
        if q is None:
            return new_state, None
        qc = inp[3]
        rel = jnp.where(tril[:, :, None], gcum[:, :, :, None] - gcum[:, :, None], -jnp.inf)
        scores = jnp.einsum('bhtd,bhsd,bhtsd->bhts', qc, kc, jnp.exp(rel))
        o = (jnp.einsum('bhts,bhsv->bhtv', scores, vc)
             + jnp.einsum('bhtd,bhdv->bhtv', qc * jnp.exp(gcum), state))
        return new_state, o

    xs = (chunks(k), chunks(v), chunks(log_g)) + (() if q is None else (chunks(q),))
    s_fin, o = lax.scan(step, s0, xs)
    if q is None:
        return None, s_fin
    return o.transpose(1, 0, 3, 2, 4).reshape(b, l, nh, v.shape[-1]), s_fin


def hgrn2_mixer(u, u_c, w_in, lb, norm_w, w_out, need_ctx):
    b = u.shape[0]
    n_state = 2 * HG_KEY + HG_VAL

    def split_inputs(t, w):
        bt, lt, _ = t.shape
        proj = t @ w
        f = proj[..., :2 * HG_KEY].astype(jnp.float32).reshape(bt, lt, 2, HG_KEY)
        g = lb + (1.0 - lb) * jax.nn.sigmoid(f)
        log_g = jnp.log(g).reshape(bt, lt, 2, HG_HEADS, HG_EXPAND)
        k = (1.0 - g).astype(t.dtype).reshape(bt, lt, 2, HG_HEADS, HG_EXPAND)
        v = proj[..., 2 * HG_KEY:n_state].reshape(bt, lt, HG_HEADS, HG_HEAD_V)
        return proj[..., n_state:], k, log_g, v

    def query(rest):
        bt, lt = rest.shape[:2]
        return jax.nn.silu(rest[..., :HG_KEY]).reshape(bt, lt, HG_HEADS, HG_EXPAND)

    def bidir(q, k, log_g, v, s_f, s_b):
        o_f, fin_f = hgrn2_chunk_scan(k[:, :, 0], v, log_g[:, :, 0], s_f, q)
        o_b, fin_b = hgrn2_chunk_scan(_flip(k[:, :, 1]), _flip(v), _flip(log_g[:, :, 1]), s_b,
                                      None if q is None else _flip(q))
        o = None if q is None else o_f + _flip(o_b)
        return o, fin_f, fin_b

    def readout(o, rest):
        bt, lt = o.shape[:2]
        o = (_rms(o.astype(u.dtype)) * norm_w.reshape(HG_HEADS, HG_HEAD_V)).reshape(bt, lt, HG_VAL)
        return (o * jax.nn.silu(rest[..., HG_KEY:])) @ w_out

    zero = jnp.zeros((b, HG_HEADS, HG_EXPAND, HG_HEAD_V), jnp.float32)
    rest_c, k_c, lg_c, v_c = split_inputs(u_c, w_in if need_ctx else w_in[:, :n_state])
    o_c, s_f, s_b = bidir(query(rest_c) if need_ctx else None, k_c, lg_c, v_c, zero, zero)
    rest, k, lg, v = split_inputs(u, w_in)
    o, _, _ = bidir(query(rest), k, lg, v, s_f, s_b)
    y = readout(o, rest)
    if not need_ctx:
        return y, None
    return y, readout(o_c, rest_c)


def setup_inputs(seed: int = 0) -> dict:
    key = jax.random.key(seed)
    ks = iter(jax.random.split(key, 32))

    def nrm(shape, scale):
        return jax.random.normal(next(ks), shape, jnp.float32) * scale

    n_a, n_b, n_c = [len(range(m, DEPTH, N_MIXERS)) for m in range(N_MIXERS)]
    d = D_MODEL
    x = nrm((BATCH, SEQ, d), 1.0)
    c = nrm((BATCH, d), 1.0)
    ctx = nrm((BATCH, CTX_LEN, d), 1.0)
    c_ctx = nrm((d,), 1.0)
    ada_w = nrm((DEPTH, d, N_MOD * d), 0.5 * d ** -0.5)
    ada_b = nrm((DEPTH, N_MOD * d), 0.02)
    norm_g = 1.0 + nrm((DEPTH, 4, d), 0.02)
    ffn_w_in = nrm((DEPTH, d, 2 * FFN_DIM), d ** -0.5)
    ffn_w_out = nrm((DEPTH, FFN_DIM, d), FFN_DIM ** -0.5)
    attn_w_in = nrm((n_a, d, ATT_IN), d ** -0.5)
    attn_w_out = nrm((n_a, ATT_Q_DIM, d), ATT_Q_DIM ** -0.5)
    attn_sink = nrm((n_a, ATT_HEADS), 0.5)
    ssd_w_in = nrm((n_b, d, SSD_IN), d ** -0.5)
    ssd_conv_w = nrm((n_b, SSD_CONV_W, SSD_CONV_DIM), SSD_CONV_W ** -0.5)
    ssd_conv_b = nrm((n_b, SSD_CONV_DIM), 0.02)
    dt0 = jnp.exp(jax.random.uniform(next(ks), (n_b, 2, SSD_HEADS), jnp.float32,
                                     minval=math.log(1e-3), maxval=math.log(1e-1)))
    ssd_dt_bias = dt0 + jnp.log(-jnp.expm1(-dt0))
    ssd_a_log = jnp.log(jax.random.uniform(next(ks), (n_b, 2, SSD_HEADS), jnp.float32, minval=1.0, maxval=16.0))
    ssd_d = 1.0 + nrm((n_b, SSD_HEADS), 0.1)
    ssd_norm_w = 1.0 + nrm((n_b, SSD_D_INNER), 0.02)
    ssd_w_out = nrm((n_b, SSD_D_INNER, d), SSD_D_INNER ** -0.5)
    hgrn_w_in = nrm((n_c, d, HG_IN), d ** -0.5)
    hgrn_lb = nrm((2, DEPTH, HG_KEY), 0.1)
    hgrn_norm_w = 1.0 + nrm((n_c, HG_VAL), 0.02)
    hgrn_w_out = nrm((n_c, HG_VAL, d), HG_VAL ** -0.5)
    return {'x': x, 'c': c, 'ctx': ctx, 'c_ctx': c_ctx,
            'ada_w': ada_w, 'ada_b': ada_b, 'norm_g': norm_g,
            'ffn_w_in': ffn_w_in, 'ffn_w_out': ffn_w_out,
            'attn_w_in': attn_w_in, 'attn_w_out': attn_w_out, 'attn_sink': attn_sink,
            'ssd_w_in': ssd_w_in, 'ssd_conv_w': ssd_conv_w, 'ssd_conv_b': ssd_conv_b,
            'ssd_dt_bias': ssd_dt_bias, 'ssd_a_log': ssd_a_log, 'ssd_d': ssd_d,
            'ssd_norm_w': ssd_norm_w, 'ssd_w_out': ssd_w_out,
            'hgrn_w_in': hgrn_w_in, 'hgrn_lb': hgrn_lb, 'hgrn_norm_w': hgrn_norm_w, 'hgrn_w_out': hgrn_w_out}


def reference(x, c, ctx, c_ctx, ada_w, ada_b, norm_g, ffn_w_in, ffn_w_out,
              attn_w_in, attn_w_out, attn_sink,
              ssd_w_in, ssd_conv_w, ssd_conv_b, ssd_dt_bias, ssd_a_log, ssd_d, ssd_norm_w, ssd_w_out,
              hgrn_w_in, hgrn_lb, hgrn_norm_w, hgrn_w_out):
    b = x.shape[0]
    rope_cos, rope_sin = axial_rope_tables(x.shape[1])
    lower_bounds = hgrn2_lower_bounds(hgrn_lb)
    silu_c = jax.nn.silu(c)
    silu_cc = jax.nn.silu(c_ctx)
    h = ctx
    for i in range(DEPTH):
        kind, j = i % N_MIXERS, i // N_MIXERS
        need_ctx = i < DEPTH - 1
        mod = (silu_c @ ada_w[i] + ada_b[i]).reshape(b, N_MOD, 1, D_MODEL)
        mod_c = (silu_cc @ ada_w[i] + ada_b[i]).reshape(N_MOD, D_MODEL)
        u = modulate(x, norm_g[i, 0], mod[:, 0], mod[:, 1])
        u_c = modulate(h, norm_g[i, 0], mod_c[0], mod_c[1])
        if kind == 0:
            y, y_c = attention_mixer(u, u_c, attn_w_in[j], attn_w_out[j], attn_sink[j],
                                     rope_cos, rope_sin, need_ctx)
        elif kind == 1:
            y, y_c = ssd_mixer(u, u_c, ssd_w_in[j], ssd_conv_w[j], ssd_conv_b[j], ssd_dt_bias[j],
                               ssd_a_log[j], ssd_d[j], ssd_norm_w[j], ssd_w_out[j], need_ctx)
        else:
            y, y_c = hgrn2_mixer(u, u_c, hgrn_w_in[j], lower_bounds[:, i], hgrn_norm_w[j],
                                 hgrn_w_out[j], need_ctx)
        x = x + mod[:, 2] * rms_norm(y, norm_g[i, 1])
        f = swiglu(modulate(x, norm_g[i, 2], mod[:, 3], mod[:, 4]), ffn_w_in[i], ffn_w_out[i])
        x = x + mod[:, 5] * rms_norm(f, norm_g[i, 3])
        if need_ctx:
            h = h + mod_c[2] * rms_norm(y_c, norm_g[i, 1])
            f_c = swiglu(modulate(h, norm_g[i, 2], mod_c[3], mod_c[4]), ffn_w_in[i], ffn_w_out[i])
            h = h + mod_c[5] * rms_norm(f_c, norm_g[i, 3])
    return x
```

```python
import functools
import math

import jax
import jax.numpy as jnp
from jax import lax
from jax.experimental import pallas as pl
from jax.experimental.pallas import tpu as pltpu

F32 = jnp.float32
BF16 = jnp.bfloat16

D_MODEL = 2048
DEPTH = 4
N_MIXERS = 3
GRID_W = 64
RMS_EPS = 1e-6
N_MOD = 6

ATT_HEADS = 32
ATT_KV_HEADS = 4
ATT_GQA = ATT_HEADS // ATT_KV_HEADS
ATT_HEAD_DIM = 64
ATT_WINDOW = 128
ATT_BLOCK = 128
ATT_Q_DIM = ATT_HEADS * ATT_HEAD_DIM
ATT_KV_DIM = ATT_KV_HEADS * ATT_HEAD_DIM
ROPE_THETA = 10000.0
ROPE_PAIRS = ATT_HEAD_DIM // 4
NEG_INF = -1e30

SSD_D_INNER = 2 * D_MODEL
SSD_HEAD_DIM = 64
SSD_HEADS = SSD_D_INNER // SSD_HEAD_DIM
SSD_GROUPS = 8
SSD_GROUP_HEADS = SSD_HEADS // SSD_GROUPS
SSD_GROUP_DIM = SSD_D_INNER // SSD_GROUPS
SSD_STATE = 128
SSD_GN = SSD_GROUPS * SSD_STATE
SSD_CONV_DIM = SSD_D_INNER + 2 * SSD_GN
SSD_CHUNK = 64

HG_EXPAND = 128
HG_HEADS = D_MODEL // HG_EXPAND
HG_KEY = HG_HEADS * HG_EXPAND
HG_VAL = D_MODEL
HG_CHUNK = 64
HG_SUB = 8

FFN_DIM = -(-8 * D_MODEL // (3 * 256)) * 256

LANES = 128
VMEM_LIMIT_BYTES = 56 * 1024 * 1024
MAX_TOKEN_TILE = 768


def _params(*sem):
    return pltpu.CompilerParams(dimension_semantics=sem, vmem_limit_bytes=VMEM_LIMIT_BYTES)


def _token_tile(n_ctx, n_lat):
    rows = n_ctx + n_lat
    best = n_ctx
    for mult in range(1, rows // n_ctx + 1):
        t = mult * n_ctx
        if rows % t == 0 and t <= MAX_TOKEN_TILE:
            best = t
    assert rows % best == 0
    return best


def _silu(t):
    return t * jax.nn.sigmoid(t)


def _rms_rows(t):
    return t * lax.rsqrt(jnp.mean(t * t, axis=-1, keepdims=True) + RMS_EPS)


def _ada_kernel(c_ref, w_ref, b_ref, o_ref):
    s = _silu(c_ref[...]).astype(BF16)
    o_ref[...] = jnp.dot(s, w_ref[...].astype(BF16), preferred_element_type=F32) + b_ref[...]


def _ada_mods(c_rows, ada_w, ada_b):
    r = c_rows.shape[0]
    n = ada_w.shape[-1]
    tn = 1024
    out = pl.pallas_call(
        _ada_kernel,
        grid=(DEPTH, n // tn),
        in_specs=[pl.BlockSpec((r, D_MODEL), lambda l, j: (0, 0)),
                  pl.BlockSpec((None, D_MODEL, tn), lambda l, j: (l, 0, j)),
                  pl.BlockSpec((None, 1, tn), lambda l, j: (l, 0, j))],
        out_specs=pl.BlockSpec((None, r, tn), lambda l, j: (l, 0, j)),
        out_shape=jax.ShapeDtypeStruct((DEPTH, r, n), F32),
        compiler_params=_params("parallel", "parallel"),
        name="ada_mods",
    )(c_rows, ada_w, ada_b.reshape(DEPTH, 1, n))
    return out.reshape(DEPTH, r, N_MOD, D_MODEL)


def _mod_row(mod_ref, modc_ref, row, first, r0):
    m = mod_ref[row:row + 1, :]
    if r0 == 0:
        m = jnp.where(first, modc_ref[row:row + 1, :], m)
    return m


def _modulate_into(x_ref, u_ref, mod_ref, modc_ref, g_ref, shift_row, first, n_ctx):
    g = g_ref[...]
    for r0 in range(0, x_ref.shape[0], n_ctx):
        shift = _mod_row(mod_ref, modc_ref, shift_row, first, r0)
        scale = _mod_row(mod_ref, modc_ref, shift_row + 1, first, r0)
        x = x_ref[r0:r0 + n_ctx, :]
        u_ref[r0:r0 + n_ctx, :] = (_rms_rows(x) * (g * (1.0 + scale)) + shift).astype(u_ref.dtype)


def _gated_residual_into(o_ref, x_ref, mod_ref, modc_ref, g_ref, gate_row, first, n_ctx):
    g = g_ref[...]
    for r0 in range(0, x_ref.shape[0], n_ctx):
        gate = _mod_row(mod_ref, modc_ref, gate_row, first, r0)
        f = o_ref[r0:r0 + n_ctx, :]
        o_ref[r0:r0 + n_ctx, :] = x_ref[r0:r0 + n_ctx, :] + gate * (_rms_rows(f) * g)


def _mod_specs(tiles_per_batch, n_batch):
    return [pl.BlockSpec((None, N_MOD, D_MODEL), lambda i, j: (i // tiles_per_batch, 0, 0)),
            pl.BlockSpec((None, N_MOD, D_MODEL), lambda i, j: (n_batch, 0, 0))]


def _modproj_kernel(x_ref, mod_ref, modc_ref, g_ref, w_ref, o_ref, u_ref, *, n_ctx, tiles_per_batch):
    first = pl.program_id(0) % tiles_per_batch == 0

    @pl.when(pl.program_id(1) == 0)
    def _():
        _modulate_into(x_ref, u_ref, mod_ref, modc_ref, g_ref, 0, first, n_ctx)

    o_ref[...] = jnp.dot(u_ref[...], w_ref[...], preferred_element_type=F32).astype(o_ref.dtype)


def _modproj(x, mod, g, w, tn, n_ctx, tm, n_batch):
    t, d = x.shape
    n = w.shape[1]
    tpb = t // n_batch // tm
    return pl.pallas_call(
        functools.partial(_modproj_kernel, n_ctx=n_ctx, tiles_per_batch=tpb),
        grid=(t // tm, n // tn),
        in_specs=[pl.BlockSpec((tm, d), lambda i, j: (i, 0))] + _mod_specs(tpb, n_batch)
        + [pl.BlockSpec((1, d), lambda i, j: (0, 0)),
           pl.BlockSpec((d, tn), lambda i, j: (0, j))],
        out_specs=pl.BlockSpec((tm, tn), lambda i, j: (i, j)),
        out_shape=jax.ShapeDtypeStruct((t, n), BF16),
        scratch_shapes=[pltpu.VMEM((tm, d), BF16)],
        compiler_params=_params("parallel", "arbitrary"),
        name="modproj",
    )(x, mod, mod, g.reshape(1, d), w)


def _outproj_kernel(y_ref, w_ref, x_ref, mod_ref, modc_ref, g_ref, o_ref, *, n_ctx, tiles_per_batch, nk):
    k = pl.program_id(1)
    part = jnp.dot(y_ref[...], w_ref[...], preferred_element_type=F32)

    @pl.when(k == 0)
    def _():
        o_ref[...] = part

    @pl.when(k > 0)
    def _():
        o_ref[...] += part

    @pl.when(k == nk - 1)
    def _():
        first = pl.program_id(0) % tiles_per_batch == 0
        _gated_residual_into(o_ref, x_ref, mod_ref, modc_ref, g_ref, 2, first, n_ctx)


def _outproj(y, w, x, mod, g, n_ctx, tm, n_batch):
    t, d = x.shape
    kdim = y.shape[1]
    tk = min(kdim, 2048)
    nk = kdim // tk
    tpb = t // n_batch // tm
    return pl.pallas_call(
        functools.partial(_outproj_kernel, n_ctx=n_ctx, tiles_per_batch=tpb, nk=nk),
        grid=(t // tm, nk),
        in_specs=[pl.BlockSpec((tm, tk), lambda i, k: (i, k)),
                  pl.BlockSpec((tk, d), lambda i, k: (k, 0)),
                  pl.BlockSpec((tm, d), lambda i, k: (i, 0))] + _mod_specs(tpb, n_batch)
        + [pl.BlockSpec((1, d), lambda i, k: (0, 0))],
        out_specs=pl.BlockSpec((tm, d), lambda i, k: (i, 0)),
        out_shape=jax.ShapeDtypeStruct((t, d), F32),
        compiler_params=_params("parallel", "arbitrary"),
        name="outproj",
    )(y, w, x, mod, mod, g.reshape(1, d))


def _ffn_kernel(x_ref, mod_ref, modc_ref, g2_ref, g3_ref, wg_ref, wu_ref, wo_ref, o_ref, u_ref,
                *, n_ctx, tiles_per_batch, nf):
    j = pl.program_id(1)
    first = pl.program_id(0) % tiles_per_batch == 0

    @pl.when(j == 0)
    def _():
        _modulate_into(x_ref, u_ref, mod_ref, modc_ref, g2_ref, 3, first, n_ctx)

    u = u_ref[...]
    gate = jnp.dot(u, wg_ref[...], preferred_element_type=F32)
    up = jnp.dot(u, wu_ref[...], preferred_element_type=F32)
    h = (_silu(gate) * up).astype(BF16)
    part = jnp.dot(h, wo_ref[...], preferred_element_type=F32)

    @pl.when(j == 0)
    def _():
        o_ref[...] = part

    @pl.when(j > 0)
    def _():
        o_ref[...] += part

    @pl.when(j == nf - 1)
    def _():
        _gated_residual_into(o_ref, x_ref, mod_ref, modc_ref, g3_ref, 5, first, n_ctx)


def _ffn(x, mod, g2, g3, w_in, w_out, n_ctx, tm, n_batch):
    t, d = x.shape
    f = w_out.shape[0]
    tf = 512
    nf = f // tf
    tpb = t // n_batch // tm
    return pl.pallas_call(
        functools.partial(_ffn_kernel, n_ctx=n_ctx, tiles_per_batch=tpb, nf=nf),
        grid=(t // tm, nf),
        in_specs=[pl.BlockSpec((tm, d), lambda i, j: (i, 0))] + _mod_specs(tpb, n_batch)
        + [pl.BlockSpec((1, d), lambda i, j: (0, 0)),
           pl.BlockSpec((1, d), lambda i, j: (0, 0)),
           pl.BlockSpec((d, tf), lambda i, j: (0, j)),
           pl.BlockSpec((d, tf), lambda i, j: (0, nf + j)),
           pl.BlockSpec((tf, d), lambda i, j: (j, 0))],
        out_specs=pl.BlockSpec((tm, d), lambda i, j: (i, 0)),
        out_shape=jax.ShapeDtypeStruct((t, d), F32),
        scratch_shapes=[pltpu.VMEM((tm, d), BF16)],
        compiler_params=_params("parallel", "arbitrary"),
        name="ffn",
    )(x, mod, mod, g2.reshape(1, d), g3.reshape(1, d), w_in, w_in, w_out)


def _rope_tables(n_ctx, n_lat):
    pos = jnp.arange(n_lat)
    row = (pos // GRID_W).astype(F32)
    col = (pos % GRID_W).astype(F32)
    inv_freq = ROPE_THETA ** (-jnp.arange(ROPE_PAIRS, dtype=F32) / ROPE_PAIRS)
    ang_row = row[:, None] * inv_freq
    ang_col = col[:, None] * inv_freq
    zero = jnp.zeros_like(ang_row)
    cos = jnp.concatenate([jnp.cos(ang_row)] * 2 + [jnp.cos(ang_col)] * 2, axis=1)
    s_lo = jnp.concatenate([-jnp.sin(ang_row), zero, -jnp.sin(ang_col), zero], axis=1)
    s_hi = jnp.concatenate([zero, jnp.sin(ang_row), zero, jnp.sin(ang_col)], axis=1)

    def full(tab, ctx_val):
        tab = jnp.concatenate([jnp.full((n_ctx, ATT_HEAD_DIM), ctx_val, F32), tab], axis=0)
        return jnp.concatenate([tab, tab], axis=1)

    return full(cos, 1.0), full(s_lo, 0.0), full(s_hi, 0.0)


def _rope(x, c, s_lo, s_hi):
    return x * c + pltpu.roll(x, LANES - ROPE_PAIRS, 1) * s_lo + pltpu.roll(x, ROPE_PAIRS, 1) * s_hi


def _attn_kernel(sink_ref, pq_ref, pk_ref, pv_ref, cq_ref, slq_ref, shq_ref, ck_ref, slk_ref, shk_ref,
                 o_ref, kx_ref, vx_ref, q_ref, *, n_ctx, n_lat):
    t = pl.program_id(1)
    rows = n_ctx + n_lat
    blk = ATT_BLOCK

    @pl.when(t == 0)
    def _():
        for c0 in range(0, ATT_KV_DIM, LANES):
            k = pk_ref[:, c0:c0 + LANES].astype(F32)
            kx_ref[0:rows, c0:c0 + LANES] = _rope(k, ck_ref[...], slk_ref[...], shk_ref[...]).astype(BF16)
        kx_ref[rows:rows + blk, :] = jnp.zeros((blk, ATT_KV_DIM), BF16)
        vx_ref[0:rows, :] = pv_ref[...]
        vx_ref[rows:rows + blk, :] = jnp.zeros((blk, ATT_KV_DIM), BF16)

    scale = ATT_HEAD_DIM ** -0.5
    for c0 in range(0, ATT_Q_DIM, LANES):
        q = pq_ref[:, 2 * ATT_KV_DIM + c0:2 * ATT_KV_DIM + c0 + LANES].astype(F32)
        q_ref[:, c0:c0 + LANES] = (_rope(q, cq_ref[...], slq_ref[...], shq_ref[...]) * scale).astype(BF16)

    n = t - n_ctx // blk
    off = pl.multiple_of(jnp.maximum(n_ctx + (n - 1) * blk, 0), blk)
    r = lax.broadcasted_iota(jnp.int32, (blk, 3 * blk), 0)
    cidx = lax.broadcasted_iota(jnp.int32, (blk, 3 * blk), 1)
    kpos = (n - 1) * blk + cidx
    valid = (jnp.abs(r + blk - cidx) <= ATT_WINDOW) & (kpos >= 0) & (kpos < n_lat) & (n >= 0)

    nt = (((1,), (1,)), ((), ()))
    for kv in range(ATT_KV_HEADS):
        c0 = kv * ATT_HEAD_DIM
        k_ctx = kx_ref[0:n_ctx, c0:c0 + ATT_HEAD_DIM]
        v_ctx = vx_ref[0:n_ctx, c0:c0 + ATT_HEAD_DIM]
        k_loc = kx_ref[pl.ds(off, 3 * blk), c0:c0 + ATT_HEAD_DIM]
        v_loc = vx_ref[pl.ds(off, 3 * blk), c0:c0 + ATT_HEAD_DIM]
        outs = []
        for g in range(ATT_GQA):
            h = kv * ATT_GQA + g
            qh = q_ref[:, h * ATT_HEAD_DIM:(h + 1) * ATT_HEAD_DIM]
            s_ctx = lax.dot_general(qh, k_ctx, nt, preferred_element_type=F32)
            s_loc = lax.dot_general(qh, k_loc, nt, preferred_element_type=F32)
            s_loc = jnp.where(valid, s_loc, NEG_INF)
            sink = sink_ref[h]
            m = jnp.maximum(jnp.maximum(jnp.max(s_ctx, axis=-1, keepdims=True),
                                        jnp.max(s_loc, axis=-1, keepdims=True)), sink)
            p_ctx = jnp.exp(s_ctx - m)
            p_loc = jnp.exp(s_loc - m)
            den = (jnp.sum(p_ctx, axis=-1, keepdims=True) + jnp.sum(p_loc, axis=-1, keepdims=True)
                   + jnp.exp(sink - m))
            o = (jnp.dot(p_ctx.astype(BF16), v_ctx, preferred_element_type=F32)
                 + jnp.dot(p_loc.astype(BF16), v_loc, preferred_element_type=F32))
            outs.append(o / den)
        o_ref[:, kv * ATT_GQA * ATT_HEAD_DIM:(kv + 1) * ATT_GQA * ATT_HEAD_DIM] = (
            jnp.concatenate(outs, axis=1).astype(o_ref.dtype))


def _attention(proj, sink, tables, n_ctx, n_lat, n_batch):
    rows = n_ctx + n_lat
    nq = rows // ATT_BLOCK
    cos, s_lo, s_hi = tables
    qtab = pl.BlockSpec((ATT_BLOCK, LANES), lambda b, t: (t, 0))
    ktab = pl.BlockSpec((rows, LANES), lambda b, t: (0, 0))
    return pl.pallas_call(
        functools.partial(_attn_kernel, n_ctx=n_ctx, n_lat=n_lat),
        grid=(n_batch, nq),
        in_specs=[pl.BlockSpec(memory_space=pltpu.SMEM),
                  pl.BlockSpec((ATT_BLOCK, proj.shape[1]), lambda b, t: (b * nq + t, 0)),
                  pl.BlockSpec((rows, ATT_KV_DIM), lambda b, t: (b, 0)),
                  pl.BlockSpec((rows, ATT_KV_DIM), lambda b, t: (b, 1)),
                  qtab, qtab, qtab, ktab, ktab, ktab],
        out_specs=pl.BlockSpec((ATT_BLOCK, ATT_Q_DIM), lambda b, t: (b * nq + t, 0)),
        out_shape=jax.ShapeDtypeStruct((proj.shape[0], ATT_Q_DIM), BF16),
        scratch_shapes=[pltpu.VMEM((rows + ATT_BLOCK, ATT_KV_DIM), BF16),
                        pltpu.VMEM((rows + ATT_BLOCK, ATT_KV_DIM), BF16),
                        pltpu.VMEM((ATT_BLOCK, ATT_Q_DIM), BF16)],
        compiler_params=_params("parallel", "arbitrary"),
        name="attention",
    )(sink.astype(F32), proj, proj, proj, cos, s_lo, s_hi, cos, s_lo, s_hi)


def _softplus(t):
    return jnp.maximum(t, 0.0) + jnp.log(1.0 + jnp.exp(-jnp.abs(t)))


def _lane_expand(cols, width):
    rows = cols[0].shape[0]
    return jnp.concatenate([jnp.broadcast_to(v, (rows, width)) for v in cols], axis=1)


def _conv_silu_into(src_ref, pad_ref, w_ref, b_ref, dst_ref, n_ctx):
    rows, width = src_ref.shape
    pad_ref[0:8, 0:width] = jnp.zeros((8, width), F32)
    pad_ref[8:8 + rows, 0:width] = src_ref[...].astype(F32)
    pad_ref[8 + rows:16 + rows, 0:width] = jnp.zeros((8, width), F32)
    w = w_ref[...]
    b = b_ref[...]
    row = lax.broadcasted_iota(jnp.int32, (n_ctx, width), 0)
    for ci, r0 in enumerate(range(0, rows, n_ctx)):
        prv = pad_ref[7 + r0:7 + r0 + n_ctx, 0:width]
        cur = pad_ref[8 + r0:8 + r0 + n_ctx, 0:width]
        nxt = pad_ref[9 + r0:9 + r0 + n_ctx, 0:width]
        if ci == 0:
            nxt = jnp.where(row == n_ctx - 1, 0.0, nxt)
        if ci == 1:
            prv = jnp.where(row == 0, 0.0, prv)
        out = b + prv * w[0:1, :] + cur * w[1:2, :] + nxt * w[2:3, :]
        dst_ref[r0:r0 + n_ctx, :] = _silu(out).astype(dst_ref.dtype)


def _ssd_chunk(c, rev, xa_ref, ba_ref, ca_ref, dtv_ref, da_ref, y_ref, s_ref):
    q = SSD_CHUNK
    p = SSD_HEAD_DIM
    r0 = pl.multiple_of(c * q, q)
    xa = xa_ref[pl.ds(r0, q), :]
    bc = ba_ref[pl.ds(r0, q), :]
    cc = ca_ref[pl.ds(r0, q), :]
    dtc = dtv_ref[pl.ds(r0, q), :]
    dac = da_ref[pl.ds(r0, q), :]
    ri = lax.broadcasted_iota(jnp.int32, (q, q), 0)
    ci = lax.broadcasted_iota(jnp.int32, (q, q), 1)
    mask = (ri <= ci) if rev else (ri >= ci)
    cum = jnp.dot(mask.astype(F32), dac, precision=lax.Precision.HIGHEST, preferred_element_type=F32)
    cum_t = cum.T
    total = cum[0:1, :] if rev else cum[q - 1:q, :]
    cb = lax.dot_general(cc, bc, (((1,), (1,)), ((), ())), preferred_element_type=F32)
    lane0 = SSD_GROUP_HEADS if rev else 0
    y_diag, xdec, enter, whole = [], [], [], []
    for e in range(SSD_GROUP_HEADS):
        le = lane0 + e
        a_col = cum[:, le:le + 1]
        a_row = cum_t[le:le + 1, :]
        decay = jnp.exp(jnp.where(mask, a_col - a_row, NEG_INF))
        xdt = xa[:, e * p:(e + 1) * p] * dtc[:, le:le + 1]
        y_diag.append(jnp.dot((cb * decay).astype(BF16), xdt.astype(BF16), preferred_element_type=F32))
        tot = total[:, le:le + 1]
        xdec.append((xdt * jnp.exp(tot - a_col)).astype(BF16))
        enter.append(jnp.exp(a_col))
        whole.append(jnp.exp(tot))
    state = s_ref[...]
    y_off = jnp.dot(cc, state.astype(BF16), preferred_element_type=F32) * _lane_expand(enter, p)
    y = jnp.concatenate(y_diag, axis=1) + y_off
    if rev:
        y_ref[pl.ds(r0, q), :] += y
    else:
        y_ref[pl.ds(r0, q), :] = y
    upd = lax.dot_general(bc, jnp.concatenate(xdec, axis=1), (((0,), (0,)), ((), ())),
                          preferred_element_type=F32)
    s_ref[...] = state * _lane_expand(whole, p) + upd


def _ssd_kernel(x_ref, z_ref, b_ref, c_ref, dt_ref, cwx_ref, cwb_ref, cwc_ref, cbx_ref, cbb_ref, cbc_ref,
                dtb_ref, alog_ref, dskip_ref, nw_ref, o_ref,
                pad_ref, xa_ref, ba_ref, ca_ref, dtv_ref, da_ref, y_ref, s_ref, *, n_ctx, n_lat):
    _conv_silu_into(x_ref, pad_ref, cwx_ref, cbx_ref, xa_ref, n_ctx)
    _conv_silu_into(b_ref, pad_ref, cwb_ref, cbb_ref, ba_ref, n_ctx)
    _conv_silu_into(c_ref, pad_ref, cwc_ref, cbc_ref, ca_ref, n_ctx)
    dtv = _softplus(dt_ref[...].astype(F32) + dtb_ref[...])
    dtv_ref[...] = dtv
    da_ref[...] = dtv * (-jnp.exp(alog_ref[...]))

    scratch = (xa_ref, ba_ref, ca_ref, dtv_ref, da_ref, y_ref, s_ref)
    n_c = n_ctx // SSD_CHUNK
    n_l = n_lat // SSD_CHUNK

    def fwd(c, carry):
        _ssd_chunk(c, False, *scratch)
        return carry

    def bwd_ctx(s, carry):
        _ssd_chunk(n_c - 1 - s, True, *scratch)
        return carry

    def bwd_lat(s, carry):
        _ssd_chunk(n_c + n_l - 1 - s, True, *scratch)
        return carry

    s_ref[...] = jnp.zeros(s_ref.shape, F32)
    lax.fori_loop(0, n_c + n_l, fwd, 0)
    s_ref[...] = jnp.zeros(s_ref.shape, F32)
    lax.fori_loop(0, n_c, bwd_ctx, 0)
    lax.fori_loop(0, n_l, bwd_lat, 0)

    for r0 in range(0, n_ctx + n_lat, n_ctx):
        y = y_ref[r0:r0 + n_ctx, :] + xa_ref[r0:r0 + n_ctx, :] * dskip_ref[...]
        t = y * _silu(z_ref[r0:r0 + n_ctx, :].astype(F32))
        o_ref[r0:r0 + n_ctx, :] = (_rms_rows(t) * nw_ref[...]).astype(o_ref.dtype)


def _ssd_layout(w_in, conv_w, conv_b, dt_bias, a_log, d_skip):
    n_state = SSD_CONV_DIM + 2 * SSD_HEADS
    gh = SSD_GROUP_HEADS

    def dt_blocks(t):
        lead = t.shape[:-1]
        t = t.reshape(lead + (2, SSD_GROUPS, gh))
        t = jnp.moveaxis(t, -3, -2).reshape(lead + (SSD_GROUPS, 2 * gh))
        t = jnp.pad(t, [(0, 0)] * (len(lead) + 1) + [(0, LANES - 2 * gh)])
        return t.reshape(lead + (SSD_GROUPS * LANES,))

    w = jnp.concatenate([w_in[:, :SSD_D_INNER], w_in[:, n_state:], w_in[:, SSD_D_INNER:SSD_CONV_DIM],
                         dt_blocks(w_in[:, SSD_CONV_DIM:n_state])], axis=1)
    return (w.astype(BF16), conv_w, conv_b.reshape(1, -1), dt_blocks(dt_bias.reshape(-1)).reshape(1, -1),
            dt_blocks(a_log.reshape(-1)).reshape(1, -1), jnp.repeat(d_skip, SSD_HEAD_DIM).reshape(1, -1))


def _ssd(proj, conv_w, conv_b, dt_bias, a_log, d_skip, norm_w, n_ctx, n_lat, n_batch):
    rows = n_ctx + n_lat
    gd = SSD_GROUP_DIM
    xb = SSD_D_INNER // gd
    bb = 2 * SSD_D_INNER // LANES
    sec = lambda width, off: pl.BlockSpec((rows, width), lambda b, g: (b, off + g))
    par = lambda r, width, off: pl.BlockSpec((r, width), lambda b, g: (0, off + g))
    return pl.pallas_call(
        functools.partial(_ssd_kernel, n_ctx=n_ctx, n_lat=n_lat),
        grid=(n_batch, SSD_GROUPS),
        in_specs=[sec(gd, 0), sec(gd, xb), sec(LANES, bb), sec(LANES, bb + SSD_GROUPS),
                  sec(LANES, bb + 2 * SSD_GROUPS),
                  par(3, gd, 0), par(3, LANES, SSD_D_INNER // LANES), par(3, LANES, SSD_D_INNER // LANES + SSD_GROUPS),
                  par(1, gd, 0), par(1, LANES, SSD_D_INNER // LANES), par(1, LANES, SSD_D_INNER // LANES + SSD_GROUPS),
                  par(1, LANES, 0), par(1, LANES, 0), par(1, gd, 0), par(1, gd, 0)],
        out_specs=pl.BlockSpec((rows, gd), lambda b, g: (b, g)),
        out_shape=jax.ShapeDtypeStruct((proj.shape[0], SSD_D_INNER), BF16),
        scratch_shapes=[pltpu.VMEM((rows + 16, gd), F32),
                        pltpu.VMEM((rows, gd), F32),
                        pltpu.VMEM((rows, SSD_STATE), BF16),
                        pltpu.VMEM((rows, SSD_STATE), BF16),
                        pltpu.VMEM((rows, LANES), F32),
                        pltpu.VMEM((rows, LANES), F32),
                        pltpu.VMEM((rows, gd), F32),
                        pltpu.VMEM((SSD_STATE, gd), F32)],
        compiler_params=_params("parallel", "parallel"),
        name="ssd",
    )(proj, proj, proj, proj, proj, conv_w, conv_w, conv_w, conv_b, conv_b, conv_b,
      dt_bias, a_log, d_skip, norm_w.reshape(1, -1))


def _hgrn_chunk(c, rev, lg_ref, k_ref, q_ref, v_ref, o_ref, st_ref):
    q_len = HG_CHUNK
    sub = HG_SUB
    nb = q_len // sub
    r0 = pl.multiple_of(c * q_len, q_len)
    lg = lg_ref[pl.ds(r0, q_len), :]
    k = k_ref[pl.ds(r0, q_len), :]
    q = q_ref[pl.ds(r0, q_len), :]
    v = v_ref[pl.ds(r0, q_len), :]
    vb = v.astype(BF16)
    ri = lax.broadcasted_iota(jnp.int32, (q_len, q_len), 0)
    ci = lax.broadcasted_iota(jnp.int32, (q_len, q_len), 1)
    mask = (ri <= ci) if rev else (ri >= ci)
    cum = jnp.dot(mask.astype(F32), lg, precision=lax.Precision.HIGHEST, preferred_element_type=F32)
    total = cum[0:1, :] if rev else cum[q_len - 1:q_len, :]

    st = st_ref[...]
    o = lax.dot_general((q * jnp.exp(cum)).astype(BF16), st.astype(BF16), (((1,), (1,)), ((), ())),
                        preferred_element_type=F32)
    k_end = (k * jnp.exp(total - cum)).astype(BF16)
    st_ref[...] = st * jnp.exp(total) + lax.dot_general(vb, k_end, (((0,), (0,)), ((), ())),
                                                         preferred_element_type=F32)

    def edge(i):
        if rev:
            return cum[(i + 1) * sub:(i + 1) * sub + 1, :] if i < nb - 1 else jnp.zeros((1, LANES), F32)
        return cum[i * sub - 1:i * sub, :] if i > 0 else jnp.zeros((1, LANES), F32)

    edges = [edge(i) for i in range(nb)]
    edge_rows = jnp.concatenate([jnp.broadcast_to(e, (sub, LANES)) for e in edges], axis=0)
    qs = q * jnp.exp(jnp.minimum(cum - edge_rows, 0.0))
    rblk = lax.broadcasted_iota(jnp.int32, (q_len, LANES), 0) // sub
    active = range(0, nb - 1) if rev else range(1, nb)
    q_st = jnp.concatenate([jnp.where(rblk == i, qs, 0.0).astype(BF16) for i in active], axis=1)
    k_st = jnp.concatenate([(k * jnp.exp(jnp.minimum(edges[i] - cum, 0.0))).astype(BF16) for i in active], axis=1)
    a_off = lax.dot_general(q_st, k_st, (((1,), (1,)), ((), ())), preferred_element_type=F32)
    other = (ri // sub < ci // sub) if rev else (ri // sub > ci // sub)
    o = o + jnp.dot(jnp.where(other, a_off, 0.0).astype(BF16), vb, preferred_element_type=F32)

    tr = lax.broadcasted_iota(jnp.int32, (sub, LANES), 0)
    vf = v.astype(F32)
    diag = []
    for i in range(nb):
        sl = slice(i * sub, (i + 1) * sub)
        gi, qi, ki, vi = cum[sl], q[sl], k[sl], vf[sl]
        oi = jnp.zeros((sub, LANES), F32)
        for s in range(sub):
            seen = (tr <= s) if rev else (tr >= s)
            w = jnp.exp(jnp.where(seen, gi - gi[s:s + 1, :], NEG_INF))
            a = jnp.sum(qi * ki[s:s + 1, :] * w, axis=-1, keepdims=True)
            oi = oi + a * vi[s:s + 1, :]
        diag.append(oi)
    o = o + jnp.concatenate(diag, axis=0)
    if rev:
        o_ref[pl.ds(r0, q_len), :] += o
    else:
        o_ref[pl.ds(r0, q_len), :] = o


def _hgrn_kernel(ff_ref, fb_ref, v_ref, q_ref, gate_ref, lb_ref, nw_ref, out_ref,
                 lgf_ref, lgb_ref, kf_ref, kb_ref, qa_ref, o_ref, st_ref, *, n_ctx, n_lat):
    for d, (f_ref, lg_ref, k_ref) in enumerate(((ff_ref, lgf_ref, kf_ref), (fb_ref, lgb_ref, kb_ref))):
        lb = lb_ref[d:d + 1, :]
        g = lb + (1.0 - lb) * jax.nn.sigmoid(f_ref[...].astype(F32))
        lg_ref[...] = jnp.log(g)
        k_ref[...] = 1.0 - g
    qa_ref[...] = _silu(q_ref[...].astype(F32))

    n_c = n_ctx // HG_CHUNK
    n_l = n_lat // HG_CHUNK

    def fwd(c, carry):
        _hgrn_chunk(c, False, lgf_ref, kf_ref, qa_ref, v_ref, o_ref, st_ref)
        return carry

    def bwd_ctx(s, carry):
        _hgrn_chunk(n_c - 1 - s, True, lgb_ref, kb_ref, qa_ref, v_ref, o_ref, st_ref)
        return carry

    def bwd_lat(s, carry):
        _hgrn_chunk(n_c + n_l - 1 - s, True, lgb_ref, kb_ref, qa_ref, v_ref, o_ref, st_ref)
        return carry

    st_ref[...] = jnp.zeros(st_ref.shape, F32)
    lax.fori_loop(0, n_c + n_l, fwd, 0)
    st_ref[...] = jnp.zeros(st_ref.shape, F32)
    lax.fori_loop(0, n_c, bwd_ctx, 0)
    lax.fori_loop(0, n_l, bwd_lat, 0)

    out_ref[...] = (_rms_rows(o_ref[...]) * nw_ref[...] * _silu(gate_ref[...].astype(F32))).astype(out_ref.dtype)


def _hgrn(proj, lower, norm_w, n_ctx, n_lat, n_batch):
    rows = n_ctx + n_lat
    sec = lambda off: pl.BlockSpec((rows, LANES), lambda b, h: (b, off * HG_HEADS + h))
    seq = lambda dt: pltpu.VMEM((rows, LANES), dt)
    return pl.pallas_call(
        functools.partial(_hgrn_kernel, n_ctx=n_ctx, n_lat=n_lat),
        grid=(n_batch, HG_HEADS),
        in_specs=[sec(0), sec(1), sec(2), sec(3), sec(4),
                  pl.BlockSpec((2, LANES), lambda b, h: (0, h)),
                  pl.BlockSpec((1, LANES), lambda b, h: (0, h))],
        out_specs=pl.BlockSpec((rows, LANES), lambda b, h: (b, h)),
        out_shape=jax.ShapeDtypeStruct((proj.shape[0], HG_VAL), BF16),
        scratch_shapes=[seq(F32), seq(F32), seq(F32), seq(F32), seq(F32), seq(F32),
                        pltpu.VMEM((LANES, LANES), F32)],
        compiler_params=_params("parallel", "parallel"),
        name="hgrn2",
    )(proj, proj, proj, proj, proj, lower, norm_w.reshape(1, -1))


def _hgrn_lower_bounds(lb_raw):
    p = jax.nn.softmax(lb_raw.astype(F32), axis=1)
    return jnp.cumsum(p, axis=1) - p[:, :1]


def kernel(x, c, ctx, c_ctx, ada_w, ada_b, norm_g, ffn_w_in, ffn_w_out, attn_w_in, attn_w_out, attn_sink,
           ssd_w_in, ssd_conv_w, ssd_conv_b, ssd_dt_bias, ssd_a_log, ssd_d, ssd_norm_w, ssd_w_out,
           hgrn_w_in, hgrn_lb, hgrn_norm_w, hgrn_w_out):
    n_batch, n_lat, d = x.shape
    n_ctx = ctx.shape[1]
    rows = n_ctx + n_lat
    tm = _token_tile(n_ctx, n_lat)

    r = jnp.concatenate([ctx, x], axis=1).reshape(n_batch * rows, d)
    n_cond = -(-(n_batch + 1) // 8) * 8
    c_rows = jnp.concatenate([c, c_ctx[None, :], jnp.zeros((n_cond - n_batch - 1, d), F32)], axis=0)
    mods = _ada_mods(c_rows, ada_w, ada_b)
    tables = _rope_tables(n_ctx, n_lat)
    lower = _hgrn_lower_bounds(hgrn_lb)

    for i in range(DEPTH):
        kind, j = i % N_MIXERS, i // N_MIXERS
        mod = mods[i]
        if kind == 0:
            proj = _modproj(r, mod, norm_g[i, 0], attn_w_in[j].astype(BF16), 512, n_ctx, tm, n_batch)
            y = _attention(proj, attn_sink[j], tables, n_ctx, n_lat, n_batch)
            w_out = attn_w_out[j]
        elif kind == 1:
            w_in, conv_w, conv_b, dt_bias, a_log, d_skip = _ssd_layout(
                ssd_w_in[j], ssd_conv_w[j], ssd_conv_b[j], ssd_dt_bias[j], ssd_a_log[j], ssd_d[j])
            proj = _modproj(r, mod, norm_g[i, 0], w_in, 1024, n_ctx, tm, n_batch)
            y = _ssd(proj, conv_w, conv_b, dt_bias, a_log, d_skip, ssd_norm_w[j], n_ctx, n_lat, n_batch)
            w_out = ssd_w_out[j]
        else:
            proj = _modproj(r, mod, norm_g[i, 0], hgrn_w_in[j].astype(BF16), 1024, n_ctx, tm, n_batch)
            y = _hgrn(proj, lower[:, i], hgrn_norm_w[j], n_ctx, n_lat, n_batch)
            w_out = hgrn_w_out[j]
        r = _outproj(y, w_out.astype(BF16), r, mod, norm_g[i, 1], n_ctx, tm, n_batch)
        r = _ffn(r, mod, norm_g[i, 2], norm_g[i, 3], ffn_w_in[i].astype(BF16), ffn_w_out[i].astype(BF16),
                 n_ctx, tm, n_batch)
    return r.reshape(n_batch, rows, d)[:, n_ctx:]
```

```python
import functools
import math

import jax
import jax.numpy as jnp
from jax import lax
from jax.experimental import pallas as pl
from jax.experimental.pallas import tpu as pltpu

F32 = jnp.float32
BF16 = jnp.bfloat16

D_MODEL = 2048
DEPTH = 4
N_MIXERS = 3
GRID_W = 64
RMS_EPS = 1e-6
N_MOD = 6

ATT_HEADS = 32
ATT_KV_HEADS = 4
ATT_GQA = ATT_HEADS // ATT_KV_HEADS
ATT_HEAD_DIM = 64
ATT_WINDOW = 128
ATT_BLOCK = 128
ATT_Q_DIM = ATT_HEADS * ATT_HEAD_DIM
ATT_KV_DIM = ATT_KV_HEADS * ATT_HEAD_DIM
ROPE_THETA = 10000.0
ROPE_PAIRS = ATT_HEAD_DIM // 4
NEG_INF = -1e30
LOG2_E = 1.0 / math.log(2.0)

SSD_D_INNER = 2 * D_MODEL
SSD_HEAD_DIM = 64
SSD_HEADS = SSD_D_INNER // SSD_HEAD_DIM
SSD_GROUPS = 8
SSD_GROUP_HEADS = SSD_HEADS // SSD_GROUPS
SSD_GROUP_DIM = SSD_D_INNER // SSD_GROUPS
SSD_STATE = 128
SSD_GN = SSD_GROUPS * SSD_STATE
SSD_CONV_DIM = SSD_D_INNER + 2 * SSD_GN
SSD_CHUNK = 64

HG_EXPAND = 128
HG_HEADS = D_MODEL // HG_EXPAND
HG_KEY = HG_HEADS * HG_EXPAND
HG_VAL = D_MODEL
HG_CHUNK = 64
HG_SUB = 8

FFN_DIM = -(-8 * D_MODEL // (3 * 256)) * 256

LANES = 128
VMEM_LIMIT_BYTES = 56 * 1024 * 1024
MAX_TOKEN_TILE = 768
SCAN_UNROLL = 2


def _params(*sem):
    return pltpu.CompilerParams(dimension_semantics=sem, vmem_limit_bytes=VMEM_LIMIT_BYTES)


def _token_tile(n_ctx, n_lat):
    rows = n_ctx + n_lat
    best = n_ctx
    for mult in range(1, rows // n_ctx + 1):
        t = mult * n_ctx
        if rows % t == 0 and t <= MAX_TOKEN_TILE:
            best = t
    assert rows % best == 0
    return best


def _silu(t):
    return t * jax.nn.sigmoid(t)


def _rms_rows(t):
    return t * lax.rsqrt(jnp.mean(t * t, axis=-1, keepdims=True) + RMS_EPS)


def _ada_kernel(c_ref, w_ref, b_ref, o_ref):
    s = _silu(c_ref[...]).astype(BF16)
    o_ref[...] = jnp.dot(s, w_ref[...].astype(BF16), preferred_element_type=F32) + b_ref[...]


def _ada_mods(c_rows, ada_w, ada_b):
    r = c_rows.shape[0]
    n = ada_w.shape[-1]
    tn = 1024
    out = pl.pallas_call(
        _ada_kernel,
        grid=(DEPTH, n // tn),
        in_specs=[pl.BlockSpec((r, D_MODEL), lambda l, j: (0, 0)),
                  pl.BlockSpec((None, D_MODEL, tn), lambda l, j: (l, 0, j)),
                  pl.BlockSpec((None, 1, tn), lambda l, j: (l, 0, j))],
        out_specs=pl.BlockSpec((None, r, tn), lambda l, j: (l, 0, j)),
        out_shape=jax.ShapeDtypeStruct((DEPTH, r, n), F32),
        compiler_params=_params("parallel", "parallel"),
        name="ada_mods",
    )(c_rows, ada_w, ada_b.reshape(DEPTH, 1, n))
    return out.reshape(DEPTH, r, N_MOD, D_MODEL)


def _mod_row(mod_ref, modc_ref, row, first, r0):
    m = mod_ref[row:row + 1, :]
    if r0 == 0:
        m = jnp.where(first, modc_ref[row:row + 1, :], m)
    return m


def _modulate_into(x_ref, u_ref, mod_ref, modc_ref, g_ref, shift_row, first, n_ctx):
    g = g_ref[...]
    for r0 in range(0, x_ref.shape[0], n_ctx):
        shift = _mod_row(mod_ref, modc_ref, shift_row, first, r0)
        scale = _mod_row(mod_ref, modc_ref, shift_row + 1, first, r0)
        x = x_ref[r0:r0 + n_ctx, :]
        u_ref[r0:r0 + n_ctx, :] = (_rms_rows(x) * (g * (1.0 + scale)) + shift).astype(u_ref.dtype)


def _gated_residual_into(o_ref, x_ref, mod_ref, modc_ref, g_ref, gate_row, first, n_ctx):
    g = g_ref[...]
    for r0 in range(0, x_ref.shape[0], n_ctx):
        gate = _mod_row(mod_ref, modc_ref, gate_row, first, r0)
        f = o_ref[r0:r0 + n_ctx, :]
        o_ref[r0:r0 + n_ctx, :] = x_ref[r0:r0 + n_ctx, :] + gate * (_rms_rows(f) * g)


def _mod_specs(tiles_per_batch, n_batch):
    return [pl.BlockSpec((None, N_MOD, D_MODEL), lambda i, j: (i // tiles_per_batch, 0, 0)),
            pl.BlockSpec((None, N_MOD, D_MODEL), lambda i, j: (n_batch, 0, 0))]


def _modproj_kernel(x_ref, mod_ref, modc_ref, g_ref, w_ref, o_ref, u_ref, *, n_ctx, tiles_per_batch):
    first = pl.program_id(0) % tiles_per_batch == 0

    @pl.when(pl.program_id(1) == 0)
    def _():
        _modulate_into(x_ref, u_ref, mod_ref, modc_ref, g_ref, 0, first, n_ctx)

    o_ref[...] = jnp.dot(u_ref[...], w_ref[...], preferred_element_type=F32).astype(o_ref.dtype)


def _modproj(x, mod, g, w, tn, n_ctx, tm, n_batch):
    t, d = x.shape
    n = w.shape[1]
    tpb = t // n_batch // tm
    return pl.pallas_call(
        functools.partial(_modproj_kernel, n_ctx=n_ctx, tiles_per_batch=tpb),
        grid=(t // tm, n // tn),
        in_specs=[pl.BlockSpec((tm, d), lambda i, j: (i, 0))] + _mod_specs(tpb, n_batch)
        + [pl.BlockSpec((1, d), lambda i, j: (0, 0)),
           pl.BlockSpec((d, tn), lambda i, j: (0, j))],
        out_specs=pl.BlockSpec((tm, tn), lambda i, j: (i, j)),
        out_shape=jax.ShapeDtypeStruct((t, n), BF16),
        scratch_shapes=[pltpu.VMEM((tm, d), BF16)],
        compiler_params=_params("parallel", "arbitrary"),
        name="modproj",
    )(x, mod, mod, g.reshape(1, d), w)


def _outproj_kernel(y_ref, w_ref, x_ref, mod_ref, modc_ref, g_ref, o_ref, *, n_ctx, tiles_per_batch, nk):
    k = pl.program_id(1)
    part = jnp.dot(y_ref[...], w_ref[...], preferred_element_type=F32)

    @pl.when(k == 0)
    def _():
        o_ref[...] = part

    @pl.when(k > 0)
    def _():
        o_ref[...] += part

    @pl.when(k == nk - 1)
    def _():
        first = pl.program_id(0) % tiles_per_batch == 0
        _gated_residual_into(o_ref, x_ref, mod_ref, modc_ref, g_ref, 2, first, n_ctx)


def _outproj(y, w, x, mod, g, n_ctx, tm, n_batch):
    t, d = x.shape
    kdim = y.shape[1]
    tk = min(kdim, 2048)
    nk = kdim // tk
    tpb = t // n_batch // tm
    return pl.pallas_call(
        functools.partial(_outproj_kernel, n_ctx=n_ctx, tiles_per_batch=tpb, nk=nk),
        grid=(t // tm, nk),
        in_specs=[pl.BlockSpec((tm, tk), lambda i, k: (i, k)),
                  pl.BlockSpec((tk, d), lambda i, k: (k, 0)),
                  pl.BlockSpec((tm, d), lambda i, k: (i, 0))] + _mod_specs(tpb, n_batch)
        + [pl.BlockSpec((1, d), lambda i, k: (0, 0))],
        out_specs=pl.BlockSpec((tm, d), lambda i, k: (i, 0)),
        out_shape=jax.ShapeDtypeStruct((t, d), F32),
        compiler_params=_params("parallel", "arbitrary"),
        name="outproj",
    )(y, w, x, mod, mod, g.reshape(1, d))


def _ffn_kernel(x_ref, mod_ref, modc_ref, g2_ref, g3_ref, wg_ref, wu_ref, wo_ref, o_ref, u_ref,
                *, n_ctx, tiles_per_batch, nf):
    j = pl.program_id(1)
    first = pl.program_id(0) % tiles_per_batch == 0

    @pl.when(j == 0)
    def _():
        _modulate_into(x_ref, u_ref, mod_ref, modc_ref, g2_ref, 3, first, n_ctx)

    u = u_ref[...]
    gate = jnp.dot(u, wg_ref[...], preferred_element_type=F32)
    up = jnp.dot(u, wu_ref[...], preferred_element_type=F32)
    h = (_silu(gate) * up).astype(BF16)
    part = jnp.dot(h, wo_ref[...], preferred_element_type=F32)

    @pl.when(j == 0)
    def _():
        o_ref[...] = part

    @pl.when(j > 0)
    def _():
        o_ref[...] += part

    @pl.when(j == nf - 1)
    def _():
        _gated_residual_into(o_ref, x_ref, mod_ref, modc_ref, g3_ref, 5, first, n_ctx)


def _ffn(x, mod, g2, g3, w_in, w_out, n_ctx, tm, n_batch):
    t, d = x.shape
    f = w_out.shape[0]
    tf = 512
    nf = f // tf
    tpb = t // n_batch // tm
    return pl.pallas_call(
        functools.partial(_ffn_kernel, n_ctx=n_ctx, tiles_per_batch=tpb, nf=nf),
        grid=(t // tm, nf),
        in_specs=[pl.BlockSpec((tm, d), lambda i, j: (i, 0))] + _mod_specs(tpb, n_batch)
        + [pl.BlockSpec((1, d), lambda i, j: (0, 0)),
           pl.BlockSpec((1, d), lambda i, j: (0, 0)),
           pl.BlockSpec((d, tf), lambda i, j: (0, j)),
           pl.BlockSpec((d, tf), lambda i, j: (0, nf + j)),
           pl.BlockSpec((tf, d), lambda i, j: (j, 0))],
        out_specs=pl.BlockSpec((tm, d), lambda i, j: (i, 0)),
        out_shape=jax.ShapeDtypeStruct((t, d), F32),
        scratch_shapes=[pltpu.VMEM((tm, d), BF16)],
        compiler_params=_params("parallel", "arbitrary"),
        name="ffn",
    )(x, mod, mod, g2.reshape(1, d), g3.reshape(1, d), w_in, w_in, w_out)


def _rope_tables(n_ctx, n_lat):
    pos = jnp.arange(n_lat)
    row = (pos // GRID_W).astype(F32)
    col = (pos % GRID_W).astype(F32)
    inv_freq = ROPE_THETA ** (-jnp.arange(ROPE_PAIRS, dtype=F32) / ROPE_PAIRS)
    ang_row = row[:, None] * inv_freq
    ang_col = col[:, None] * inv_freq
    zero = jnp.zeros_like(ang_row)
    cos = jnp.concatenate([jnp.cos(ang_row)] * 2 + [jnp.cos(ang_col)] * 2, axis=1)
    s_lo = jnp.concatenate([-jnp.sin(ang_row), zero, -jnp.sin(ang_col), zero], axis=1)
    s_hi = jnp.concatenate([zero, jnp.sin(ang_row), zero, jnp.sin(ang_col)], axis=1)

    def full(tab, ctx_val):
        tab = jnp.concatenate([jnp.full((n_ctx, ATT_HEAD_DIM), ctx_val, F32), tab], axis=0)
        return jnp.concatenate([tab, tab], axis=1)

    return full(cos, 1.0), full(s_lo, 0.0), full(s_hi, 0.0)


def _rope(x, c, s_lo, s_hi):
    return x * c + pltpu.roll(x, LANES - ROPE_PAIRS, 1) * s_lo + pltpu.roll(x, ROPE_PAIRS, 1) * s_hi


def _attn_kernel(sink_ref, pq_ref, pk_ref, pv_ref, cq_ref, slq_ref, shq_ref, ck_ref, slk_ref, shk_ref,
                 o_ref, kx_ref, vx_ref, q_ref, *, n_ctx, n_lat):
    t = pl.program_id(1)
    rows = n_ctx + n_lat
    blk = ATT_BLOCK

    @pl.when(t == 0)
    def _():
        for c0 in range(0, ATT_KV_DIM, LANES):
            k = pk_ref[:, c0:c0 + LANES].astype(F32)
            kx_ref[0:rows, c0:c0 + LANES] = _rope(k, ck_ref[...], slk_ref[...], shk_ref[...]).astype(BF16)
        kx_ref[rows:rows + blk, :] = jnp.zeros((blk, ATT_KV_DIM), BF16)
        vx_ref[0:rows, :] = pv_ref[...]
        vx_ref[rows:rows + blk, :] = jnp.zeros((blk, ATT_KV_DIM), BF16)

    scale = ATT_HEAD_DIM ** -0.5
    for c0 in range(0, ATT_Q_DIM, LANES):
        q = pq_ref[:, 2 * ATT_KV_DIM + c0:2 * ATT_KV_DIM + c0 + LANES].astype(F32)
        q_ref[:, c0:c0 + LANES] = (_rope(q, cq_ref[...], slq_ref[...], shq_ref[...]) * scale).astype(BF16)

    n = t - n_ctx // blk
    off = pl.multiple_of(jnp.maximum(n_ctx + (n - 1) * blk, 0), blk)
    r = lax.broadcasted_iota(jnp.int32, (blk, 3 * blk), 0)
    cidx = lax.broadcasted_iota(jnp.int32, (blk, 3 * blk), 1)
    kpos = (n - 1) * blk + cidx
    valid = (jnp.abs(r + blk - cidx) <= ATT_WINDOW) & (kpos >= 0) & (kpos < n_lat) & (n >= 0)

    nt = (((1,), (1,)), ((), ()))
    for kv in range(ATT_KV_HEADS):
        c0 = kv * ATT_HEAD_DIM
        k_ctx = kx_ref[0:n_ctx, c0:c0 + ATT_HEAD_DIM]
        v_ctx = vx_ref[0:n_ctx, c0:c0 + ATT_HEAD_DIM]
        k_loc = kx_ref[pl.ds(off, 3 * blk), c0:c0 + ATT_HEAD_DIM]
        v_loc = vx_ref[pl.ds(off, 3 * blk), c0:c0 + ATT_HEAD_DIM]
        outs = []
        for g in range(ATT_GQA):
            h = kv * ATT_GQA + g
            qh = q_ref[:, h * ATT_HEAD_DIM:(h + 1) * ATT_HEAD_DIM]
            s_ctx = lax.dot_general(qh, k_ctx, nt, preferred_element_type=F32)
            s_loc = lax.dot_general(qh, k_loc, nt, preferred_element_type=F32)
            s_loc = jnp.where(valid, s_loc, NEG_INF)
            sink = sink_ref[h]
            m = jnp.maximum(jnp.maximum(jnp.max(s_ctx, axis=-1, keepdims=True),
                                        jnp.max(s_loc, axis=-1, keepdims=True)), sink)
            p_ctx = jnp.exp(s_ctx - m)
            p_loc = jnp.exp(s_loc - m)
            den = (jnp.sum(p_ctx, axis=-1, keepdims=True) + jnp.sum(p_loc, axis=-1, keepdims=True)
                   + jnp.exp(sink - m))
            o = (jnp.dot(p_ctx.astype(BF16), v_ctx, preferred_element_type=F32)
                 + jnp.dot(p_loc.astype(BF16), v_loc, preferred_element_type=F32))
            outs.append(o / den)
        o_ref[:, kv * ATT_GQA * ATT_HEAD_DIM:(kv + 1) * ATT_GQA * ATT_HEAD_DIM] = (
            jnp.concatenate(outs, axis=1).astype(o_ref.dtype))


def _attention(proj, sink, tables, n_ctx, n_lat, n_batch):
    rows = n_ctx + n_lat
    nq = rows // ATT_BLOCK
    cos, s_lo, s_hi = tables
    qtab = pl.BlockSpec((ATT_BLOCK, LANES), lambda b, t: (t, 0))
    ktab = pl.BlockSpec((rows, LANES), lambda b, t: (0, 0))
    return pl.pallas_call(
        functools.partial(_attn_kernel, n_ctx=n_ctx, n_lat=n_lat),
        grid=(n_batch, nq),
        in_specs=[pl.BlockSpec(memory_space=pltpu.SMEM),
                  pl.BlockSpec((ATT_BLOCK, proj.shape[1]), lambda b, t: (b * nq + t, 0)),
                  pl.BlockSpec((rows, ATT_KV_DIM), lambda b, t: (b, 0)),
                  pl.BlockSpec((rows, ATT_KV_DIM), lambda b, t: (b, 1)),
                  qtab, qtab, qtab, ktab, ktab, ktab],
        out_specs=pl.BlockSpec((ATT_BLOCK, ATT_Q_DIM), lambda b, t: (b * nq + t, 0)),
        out_shape=jax.ShapeDtypeStruct((proj.shape[0], ATT_Q_DIM), BF16),
        scratch_shapes=[pltpu.VMEM((rows + ATT_BLOCK, ATT_KV_DIM), BF16),
                        pltpu.VMEM((rows + ATT_BLOCK, ATT_KV_DIM), BF16),
                        pltpu.VMEM((ATT_BLOCK, ATT_Q_DIM), BF16)],
        compiler_params=_params("parallel", "arbitrary"),
        name="attention",
    )(sink.astype(F32), proj, proj, proj, cos, s_lo, s_hi, cos, s_lo, s_hi)


def _softplus(t):
    return jnp.maximum(t, 0.0) + jnp.log(1.0 + jnp.exp(-jnp.abs(t)))


def _split_bf16(t):
    hi = t.astype(BF16)
    return hi, (t - hi.astype(F32)).astype(BF16)


def _conv_silu_into(src_ref, pad_ref, w_ref, b_ref, dst_ref, n_ctx):
    rows, width = src_ref.shape
    pad_ref[0:8, 0:width] = jnp.zeros((8, width), F32)
    pad_ref[8:8 + rows, 0:width] = src_ref[...].astype(F32)
    pad_ref[8 + rows:16 + rows, 0:width] = jnp.zeros((8, width), F32)
    w = w_ref[...]
    b = b_ref[...]
    row = lax.broadcasted_iota(jnp.int32, (n_ctx, width), 0)
    for ci, r0 in enumerate(range(0, rows, n_ctx)):
        prv = pad_ref[7 + r0:7 + r0 + n_ctx, 0:width]
        cur = pad_ref[8 + r0:8 + r0 + n_ctx, 0:width]
        nxt = pad_ref[9 + r0:9 + r0 + n_ctx, 0:width]
        if ci == 0:
            nxt = jnp.where(row == n_ctx - 1, 0.0, nxt)
        if ci == 1:
            prv = jnp.where(row == 0, 0.0, prv)
        out = b + prv * w[0:1, :] + cur * w[1:2, :] + nxt * w[2:3, :]
        dst_ref[r0:r0 + n_ctx, :] = _silu(out).astype(dst_ref.dtype)


def _ssd_chunk(c, rev, xa_ref, ba_ref, ca_ref, dtv_ref, da_ref, y_ref, s_ref):
    q = SSD_CHUNK
    p = SSD_HEAD_DIM
    r0 = pl.multiple_of(c * q, q)
    xa = xa_ref[pl.ds(r0, q), :]
    bc = ba_ref[pl.ds(r0, q), :]
    cc = ca_ref[pl.ds(r0, q), :]
    dtc = dtv_ref[pl.ds(r0, q), :]
    dac = da_ref[pl.ds(r0, q), :]
    assert 2 * q == LANES and 2 * p == LANES
    gd = SSD_GROUP_DIM
    lane0 = SSD_GROUP_HEADS if rev else 0
    ri = lax.broadcasted_iota(jnp.int32, (q, q), 0)
    ci = lax.broadcasted_iota(jnp.int32, (q, q), 1)
    tri = ((ri <= ci) if rev else (ri >= ci)).astype(BF16)
    da_hi, da_lo = _split_bf16(dac)
    cum_hi, cum_lo = _split_bf16(jnp.dot(tri, da_hi, preferred_element_type=F32)
                                 + jnp.dot(tri, da_lo, preferred_element_type=F32))
    cum = cum_hi.astype(F32) + cum_lo.astype(F32)
    src = lax.broadcasted_iota(jnp.int32, (LANES, gd), 0)
    dst = lax.broadcasted_iota(jnp.int32, (LANES, gd), 1) // p
    expand = (src == lane0 + dst).astype(BF16)
    dt_hi, dt_lo = _split_bf16(dtc)
    wide = (jnp.dot(jnp.concatenate([cum_hi, dt_hi], axis=0), expand, preferred_element_type=F32)
            + jnp.dot(jnp.concatenate([cum_lo, dt_lo], axis=0), expand, preferred_element_type=F32))
    cum_x = wide[0:q]
    dt_x = wide[q:2 * q]
    tot_x = cum_x[0:1, :] if rev else cum_x[q - 1:q, :]
    xdt = xa * dt_x
    xdec = (xdt * jnp.exp(tot_x - cum_x)).astype(BF16)

    cum_rows = jnp.concatenate([cum, pltpu.roll(cum, LANES - 1, 1)], axis=0).T
    cb2 = lax.dot_general(cc, jnp.concatenate([bc, bc], axis=0), (((1,), (1,)), ((), ())),
                          preferred_element_type=F32)
    row = lax.broadcasted_iota(jnp.int32, (q, LANES), 0)
    lane = lax.broadcasted_iota(jnp.int32, (q, LANES), 1)
    upper = lane >= q
    key = jnp.where(upper, lane - q, lane)
    seen = (row <= key) if rev else (row >= key)
    ys = []
    for pr in range(SSD_GROUP_HEADS // 2):
        sl = slice(pr * LANES, (pr + 1) * LANES)
        le = lane0 + 2 * pr
        decay = jnp.exp(jnp.where(seen, cum_x[:, sl] - cum_rows[le:le + 1, :], NEG_INF))
        xp = xdt[:, sl]
        x2 = jnp.concatenate([jnp.where(upper, 0.0, xp), jnp.where(upper, xp, 0.0)], axis=0).astype(BF16)
        ys.append(jnp.dot((cb2 * decay).astype(BF16), x2, preferred_element_type=F32))
    state = s_ref[...]
    y = jnp.concatenate(ys, axis=1) + jnp.dot(cc, state.astype(BF16), preferred_element_type=F32) * jnp.exp(cum_x)
    y_ref[pl.ds(r0, q), :] = y
    upd = lax.dot_general(bc, xdec, (((0,), (0,)), ((), ())), preferred_element_type=F32)
    s_ref[...] = state * jnp.exp(tot_x) + upd


def _ssd_kernel(x_ref, z_ref, b_ref, c_ref, dt_ref, cwx_ref, cwb_ref, cwc_ref, cbx_ref, cbb_ref, cbc_ref,
                dtb_ref, alog_ref, dskip_ref, nw_ref, o_ref,
                pad_ref, xa_ref, ba_ref, ca_ref, dtv_ref, da_ref, y_ref, yb_ref, s_ref, sb_ref, *, n_ctx, n_lat):
    _conv_silu_into(x_ref, pad_ref, cwx_ref, cbx_ref, xa_ref, n_ctx)
    _conv_silu_into(b_ref, pad_ref, cwb_ref, cbb_ref, ba_ref, n_ctx)
    _conv_silu_into(c_ref, pad_ref, cwc_ref, cbc_ref, ca_ref, n_ctx)
    dtv = _softplus(dt_ref[...].astype(F32) + dtb_ref[...])
    dtv_ref[...] = dtv
    da_ref[...] = dtv * (-jnp.exp(alog_ref[...]))

    seq = (xa_ref, ba_ref, ca_ref, dtv_ref, da_ref)
    n_c = n_ctx // SSD_CHUNK
    n_l = n_lat // SSD_CHUNK

    def ctx_step(s, carry):
        _ssd_chunk(s, False, *seq, y_ref, s_ref)
        _ssd_chunk(n_c - 1 - s, True, *seq, yb_ref, sb_ref)
        return carry

    def lat_step(s, carry):
        _ssd_chunk(n_c + s, False, *seq, y_ref, s_ref)
        _ssd_chunk(n_c + n_l - 1 - s, True, *seq, yb_ref, sb_ref)
        return carry

    s_ref[...] = jnp.zeros(s_ref.shape, F32)
    sb_ref[...] = jnp.zeros(sb_ref.shape, F32)
    lax.fori_loop(0, n_c, ctx_step, 0, unroll=SCAN_UNROLL)
    lax.fori_loop(0, n_l, lat_step, 0, unroll=SCAN_UNROLL)

    for r0 in range(0, n_ctx + n_lat, n_ctx):
        y = y_ref[r0:r0 + n_ctx, :] + yb_ref[r0:r0 + n_ctx, :] + xa_ref[r0:r0 + n_ctx, :] * dskip_ref[...]
        t = y * _silu(z_ref[r0:r0 + n_ctx, :].astype(F32))
        o_ref[r0:r0 + n_ctx, :] = (_rms_rows(t) * nw_ref[...]).astype(o_ref.dtype)


def _ssd_layout(w_in, conv_w, conv_b, dt_bias, a_log, d_skip):
    n_state = SSD_CONV_DIM + 2 * SSD_HEADS
    gh = SSD_GROUP_HEADS

    def dt_blocks(t):
        lead = t.shape[:-1]
        t = t.reshape(lead + (2, SSD_GROUPS, gh))
        t = jnp.moveaxis(t, -3, -2).reshape(lead + (SSD_GROUPS, 2 * gh))
        t = jnp.pad(t, [(0, 0)] * (len(lead) + 1) + [(0, LANES - 2 * gh)])
        return t.reshape(lead + (SSD_GROUPS * LANES,))

    w = jnp.concatenate([w_in[:, :SSD_D_INNER], w_in[:, n_state:], w_in[:, SSD_D_INNER:SSD_CONV_DIM],
                         dt_blocks(w_in[:, SSD_CONV_DIM:n_state])], axis=1)
    return (w.astype(BF16), conv_w, conv_b.reshape(1, -1), dt_blocks(dt_bias.reshape(-1)).reshape(1, -1),
            dt_blocks(a_log.reshape(-1)).reshape(1, -1), jnp.repeat(d_skip, SSD_HEAD_DIM).reshape(1, -1))


def _ssd(proj, conv_w, conv_b, dt_bias, a_log, d_skip, norm_w, n_ctx, n_lat, n_batch):
    rows = n_ctx + n_lat
    gd = SSD_GROUP_DIM
    xb = SSD_D_INNER // gd
    bb = 2 * SSD_D_INNER // LANES
    sec = lambda width, off: pl.BlockSpec((rows, width), lambda b, g: (b, off + g))
    par = lambda r, width, off: pl.BlockSpec((r, width), lambda b, g: (0, off + g))
    return pl.pallas_call(
        functools.partial(_ssd_kernel, n_ctx=n_ctx, n_lat=n_lat),
        grid=(n_batch, SSD_GROUPS),
        in_specs=[sec(gd, 0), sec(gd, xb), sec(LANES, bb), sec(LANES, bb + SSD_GROUPS),
                  sec(LANES, bb + 2 * SSD_GROUPS),
                  par(3, gd, 0), par(3, LANES, SSD_D_INNER // LANES), par(3, LANES, SSD_D_INNER // LANES + SSD_GROUPS),
                  par(1, gd, 0), par(1, LANES, SSD_D_INNER // LANES), par(1, LANES, SSD_D_INNER // LANES + SSD_GROUPS),
                  par(1, LANES, 0), par(1, LANES, 0), par(1, gd, 0), par(1, gd, 0)],
        out_specs=pl.BlockSpec((rows, gd), lambda b, g: (b, g)),
        out_shape=jax.ShapeDtypeStruct((proj.shape[0], SSD_D_INNER), BF16),
        scratch_shapes=[pltpu.VMEM((rows + 16, gd), F32),
                        pltpu.VMEM((rows, gd), F32),
                        pltpu.VMEM((rows, SSD_STATE), BF16),
                        pltpu.VMEM((rows, SSD_STATE), BF16),
                        pltpu.VMEM((rows, LANES), F32),
                        pltpu.VMEM((rows, LANES), F32),
                        pltpu.VMEM((rows, gd), F32),
                        pltpu.VMEM((rows, gd), F32),
                        pltpu.VMEM((SSD_STATE, gd), F32),
                        pltpu.VMEM((SSD_STATE, gd), F32)],
        compiler_params=_params("parallel", "parallel"),
        name="ssd",
    )(proj, proj, proj, proj, proj, conv_w, conv_w, conv_w, conv_b, conv_b, conv_b,
      dt_bias, a_log, d_skip, norm_w.reshape(1, -1))


def _rows_from(t, idx, height):
    return jnp.concatenate([jnp.zeros((height, LANES), t.dtype) if r is None
                            else jnp.broadcast_to(t[r:r + 1, :], (height, LANES)) for r in idx], axis=0)


_NT = (((1,), (1,)), ((), ()))
_TN = (((0,), (0,)), ((), ()))


def _hgrn_intra(t, tr, rev, lg_ref, k_ref, q_ref, v_ref, o_ref, qst_ref, upd_ref, dec_ref):
    ch, sub = HG_CHUNK, HG_SUB
    npc, nb, nblk = tr // ch, ch // sub, tr // sub
    r0 = pl.multiple_of(t * tr, tr)
    lg = lg_ref[pl.ds(r0, tr), :]
    k = k_ref[pl.ds(r0, tr), :]
    q = q_ref[pl.ds(r0, tr), :]
    v = v_ref[pl.ds(r0, tr), :]
    ri = lax.broadcasted_iota(jnp.int32, (tr, tr), 0)
    ci = lax.broadcasted_iota(jnp.int32, (tr, tr), 1)
    same = (ri // ch) == (ci // ch)
    tri = (same & ((ri <= ci) if rev else (ri >= ci))).astype(BF16)
    lg_hi, lg_lo = _split_bf16(lg)
    cum = jnp.dot(tri, lg_hi, preferred_element_type=F32) + jnp.dot(tri, lg_lo, preferred_element_type=F32)

    last = [c * ch + (0 if rev else ch - 1) for c in range(npc)]
    qst_ref[pl.ds(r0, tr), :] = (q * jnp.exp2(cum)).astype(BF16)
    k_end = (k * jnp.exp2(_rows_from(cum, last, ch) - cum)).astype(BF16)
    for c in range(npc):
        chunk = t * npc + c
        dec_ref[pl.ds(chunk, 1), :] = jnp.exp2(cum[last[c]:last[c] + 1, :])
        upd_ref[pl.ds(pl.multiple_of(chunk * LANES, LANES), LANES), :] = lax.dot_general(
            v[c * ch:(c + 1) * ch], k_end[c * ch:(c + 1) * ch], _TN, preferred_element_type=F32)

    def own_edge(j):
        i = j % nb
        if rev:
            return None if i == nb - 1 else (j + 1) * sub
        return None if i == 0 else j * sub - 1

    qs = q * jnp.exp2(cum - _rows_from(cum, [own_edge(j) for j in range(nblk)], sub))
    zero8 = jnp.zeros((sub, LANES), F32)
    q_slabs, k_slabs = [], []
    for i in (range(nb - 1) if rev else range(1, nb)):
        q_slabs.append(jnp.concatenate([qs[j * sub:(j + 1) * sub] if j % nb == i else zero8
                                        for j in range(nblk)], axis=0).astype(BF16))
        pieces = []
        for c in range(npc):
            base = c * ch
            e = base + ((i + 1) * sub if rev else i * sub - 1)
            lo, hi = ((i + 1) * sub, ch) if rev else (0, i * sub)
            part = k[base + lo:base + hi] * jnp.exp2(cum[e:e + 1, :] - cum[base + lo:base + hi])
            pad = jnp.zeros((ch - (hi - lo), LANES), F32)
            pieces += [pad, part] if rev else [part, pad]
        k_slabs.append(jnp.concatenate(pieces, axis=0).astype(BF16))
    a_off = lax.dot_general(jnp.concatenate(q_slabs, axis=1), jnp.concatenate(k_slabs, axis=1), _NT,
                            preferred_element_type=F32)
    rb = (ri % ch) // sub
    cb = (ci % ch) // sub
    other = same & ((rb < cb) if rev else (rb > cb))
    o = jnp.dot(jnp.where(other, a_off, 0.0).astype(BF16), v, preferred_element_type=F32)

    tsub = lax.broadcasted_iota(jnp.int32, (tr, LANES), 0) % sub
    vf = v.astype(F32)
    prods = []
    for s in range(sub):
        idx = [j * sub + s for j in range(nblk)]
        seen = (tsub <= s) if rev else (tsub >= s)
        w = jnp.exp2(jnp.where(seen, cum - _rows_from(cum, idx, sub), NEG_INF))
        prods.append((q * _rows_from(k, idx, sub) * w).astype(BF16))
    pair = jnp.dot(jnp.concatenate(prods, axis=0), jnp.ones((LANES, LANES), BF16), preferred_element_type=F32)
    for s in range(sub):
        o = o + pair[s * tr:(s + 1) * tr, :] * _rows_from(vf, [j * sub + s for j in range(nblk)], sub)
    o_ref[pl.ds(r0, tr), :] = o


def _hgrn_state_step(c, st, qst_ref, upd_ref, dec_ref, o_ref):
    ch = HG_CHUNK
    r0 = pl.multiple_of(c * ch, ch)
    o_ref[pl.ds(r0, ch), :] += lax.dot_general(qst_ref[pl.ds(r0, ch), :], st.astype(BF16), _NT,
                                               preferred_element_type=F32)
    return st * dec_ref[pl.ds(c, 1), :] + upd_ref[pl.ds(pl.multiple_of(c * LANES, LANES), LANES), :]


def _hgrn_kernel(ff_ref, fb_ref, v_ref, q_ref, gate_ref, lb_ref, nw_ref, out_ref,
                 lgf_ref, lgb_ref, kf_ref, kb_ref, qa_ref, o_ref, ob_ref, qstf_ref, qstb_ref,
                 updf_ref, updb_ref, decf_ref, decb_ref, *, n_ctx, n_lat, tile):
    for d, (f_ref, lg_ref, k_ref) in enumerate(((ff_ref, lgf_ref, kf_ref), (fb_ref, lgb_ref, kb_ref))):
        lb = lb_ref[d:d + 1, :]
        g = lb + (1.0 - lb) * jax.nn.sigmoid(f_ref[...].astype(F32))
        lg_ref[...] = jnp.log(g) * LOG2_E
        k_ref[...] = 1.0 - g
    qa_ref[...] = _silu(q_ref[...].astype(F32))

    fwd = (qstf_ref, updf_ref, decf_ref)
    bwd = (qstb_ref, updb_ref, decb_ref)

    def intra_step(t, carry):
        _hgrn_intra(t, tile, False, lgf_ref, kf_ref, qa_ref, v_ref, o_ref, *fwd)
        _hgrn_intra(t, tile, True, lgb_ref, kb_ref, qa_ref, v_ref, ob_ref, *bwd)
        return carry

    lax.fori_loop(0, (n_ctx + n_lat) // tile, intra_step, 0)

    n_c = n_ctx // HG_CHUNK
    n_l = n_lat // HG_CHUNK

    def ctx_step(s, carry):
        return (_hgrn_state_step(s, carry[0], *fwd, o_ref),
                _hgrn_state_step(n_c - 1 - s, carry[1], *bwd, ob_ref))

    def lat_step(s, carry):
        return (_hgrn_state_step(n_c + s, carry[0], *fwd, o_ref),
                _hgrn_state_step(n_c + n_l - 1 - s, carry[1], *bwd, ob_ref))

    zero = jnp.zeros((LANES, LANES), F32)
    carry = lax.fori_loop(0, n_c, ctx_step, (zero, zero), unroll=SCAN_UNROLL)
    lax.fori_loop(0, n_l, lat_step, carry, unroll=SCAN_UNROLL)

    o = o_ref[...] + ob_ref[...]
    out_ref[...] = (_rms_rows(o) * nw_ref[...] * _silu(gate_ref[...].astype(F32))).astype(out_ref.dtype)


def _hgrn(proj, lower, norm_w, n_ctx, n_lat, n_batch):
    rows = n_ctx + n_lat
    sec = lambda off: pl.BlockSpec((rows, LANES), lambda b, h: (b, off * HG_HEADS + h))
    seq = lambda dt: pltpu.VMEM((rows, LANES), dt)
    n_chunks = rows // HG_CHUNK
    upd = pltpu.VMEM((n_chunks * LANES, LANES), F32)
    dec = pltpu.VMEM((-(-n_chunks // 8) * 8, LANES), F32)
    tile = next(t for t in (256, 128, HG_CHUNK) if rows % t == 0)
    return pl.pallas_call(
        functools.partial(_hgrn_kernel, n_ctx=n_ctx, n_lat=n_lat, tile=tile),
        grid=(n_batch, HG_HEADS),
        in_specs=[sec(0), sec(1), sec(2), sec(3), sec(4),
                  pl.BlockSpec((2, LANES), lambda b, h: (0, h)),
                  pl.BlockSpec((1, LANES), lambda b, h: (0, h))],
        out_specs=pl.BlockSpec((rows, LANES), lambda b, h: (b, h)),
        out_shape=jax.ShapeDtypeStruct((proj.shape[0], HG_VAL), BF16),
        scratch_shapes=[seq(F32), seq(F32), seq(F32), seq(F32), seq(F32), seq(F32), seq(F32),
                        seq(BF16), seq(BF16), upd, upd, dec, dec],
        compiler_params=_params("parallel", "parallel"),
        name="hgrn2",
    )(proj, proj, proj, proj, proj, lower, norm_w.reshape(1, -1))


def _hgrn_lower_bounds(lb_raw):
    p = jax.nn.softmax(lb_raw.astype(F32), axis=1)
    return jnp.cumsum(p, axis=1) - p[:, :1]


def kernel(x, c, ctx, c_ctx, ada_w, ada_b, norm_g, ffn_w_in, ffn_w_out, attn_w_in, attn_w_out, attn_sink,
           ssd_w_in, ssd_conv_w, ssd_conv_b, ssd_dt_bias, ssd_a_log, ssd_d, ssd_norm_w, ssd_w_out,
           hgrn_w_in, hgrn_lb, hgrn_norm_w, hgrn_w_out):
    n_batch, n_lat, d = x.shape
    n_ctx = ctx.shape[1]
    rows = n_ctx + n_lat
    tm = _token_tile(n_ctx, n_lat)

    r = jnp.concatenate([ctx, x], axis=1).reshape(n_batch * rows, d)
    n_cond = -(-(n_batch + 1) // 8) * 8
    c_rows = jnp.concatenate([c, c_ctx[None, :], jnp.zeros((n_cond - n_batch - 1, d), F32)], axis=0)
    mods = _ada_mods(c_rows, ada_w, ada_b)
    tables = _rope_tables(n_ctx, n_lat)
    lower = _hgrn_lower_bounds(hgrn_lb)

    for i in range(DEPTH):
        kind, j = i % N_MIXERS, i // N_MIXERS
        mod = mods[i]
        if kind == 0:
            proj = _modproj(r, mod, norm_g[i, 0], attn_w_in[j].astype(BF16), 512, n_ctx, tm, n_batch)
            y = _attention(proj, attn_sink[j], tables, n_ctx, n_lat, n_batch)
            w_out = attn_w_out[j]
        elif kind == 1:
            w_in, conv_w, conv_b, dt_bias, a_log, d_skip = _ssd_layout(
                ssd_w_in[j], ssd_conv_w[j], ssd_conv_b[j], ssd_dt_bias[j], ssd_a_log[j], ssd_d[j])
            proj = _modproj(r, mod, norm_g[i, 0], w_in, 1024, n_ctx, tm, n_batch)
            y = _ssd(proj, conv_w, conv_b, dt_bias, a_log, d_skip, ssd_norm_w[j], n_ctx, n_lat, n_batch)
            w_out = ssd_w_out[j]
        else:
            proj = _modproj(r, mod, norm_g[i, 0], hgrn_w_in[j].astype(BF16), 1024, n_ctx, tm, n_batch)
            y = _hgrn(proj, lower[:, i], hgrn_norm_w[j], n_ctx, n_lat, n_batch)
            w_out = hgrn_w_out[j]
        r = _outproj(y, w_out.astype(BF16), r, mod, norm_g[i, 1], n_ctx, tm, n_batch)
        r = _ffn(r, mod, norm_g[i, 2], norm_g[i, 3], ffn_w_in[i].astype(BF16), ffn_w_out[i].astype(BF16),
                 n_ctx, tm, n_batch)
    return r.reshape(n_batch, rows, d)[:, n_ctx:]
```

```python
import functools
import math

import jax
import jax.numpy as jnp
from jax import lax
from jax.experimental import pallas as pl
from jax.experimental.pallas import tpu as pltpu

F32 = jnp.float32
BF16 = jnp.bfloat16

D_MODEL = 2048
DEPTH = 4
N_MIXERS = 3
GRID_W = 64
RMS_EPS = 1e-6
N_MOD = 6

ATT_HEADS = 32
ATT_KV_HEADS = 4
ATT_GQA = ATT_HEADS // ATT_KV_HEADS
ATT_HEAD_DIM = 64
ATT_WINDOW = 128
ATT_BLOCK = 128
ATT_Q_DIM = ATT_HEADS * ATT_HEAD_DIM
ATT_KV_DIM = ATT_KV_HEADS * ATT_HEAD_DIM
ROPE_THETA = 10000.0
ROPE_PAIRS = ATT_HEAD_DIM // 4
NEG_INF = -1e30
LOG2_E = 1.0 / math.log(2.0)

SSD_D_INNER = 2 * D_MODEL
SSD_HEAD_DIM = 64
SSD_HEADS = SSD_D_INNER // SSD_HEAD_DIM
SSD_GROUPS = 8
SSD_GROUP_HEADS = SSD_HEADS // SSD_GROUPS
SSD_GROUP_DIM = SSD_D_INNER // SSD_GROUPS
SSD_STATE = 128
SSD_GN = SSD_GROUPS * SSD_STATE
SSD_CONV_DIM = SSD_D_INNER + 2 * SSD_GN
SSD_CHUNK = 64

HG_EXPAND = 128
HG_HEADS = D_MODEL // HG_EXPAND
HG_KEY = HG_HEADS * HG_EXPAND
HG_VAL = D_MODEL
HG_CHUNK = 64
HG_SUB = 8

FFN_DIM = -(-8 * D_MODEL // (3 * 256)) * 256

LANES = 128
VMEM_LIMIT_BYTES = 56 * 1024 * 1024
MAX_TOKEN_TILE = 768
SCAN_UNROLL = 2


def _params(*sem):
    return pltpu.CompilerParams(dimension_semantics=sem, vmem_limit_bytes=VMEM_LIMIT_BYTES)


def _token_tile(n_ctx, n_lat):
    rows = n_ctx + n_lat
    best = n_ctx
    for mult in range(1, rows // n_ctx + 1):
        t = mult * n_ctx
        if rows % t == 0 and t <= MAX_TOKEN_TILE:
            best = t
    assert rows % best == 0
    return best


def _silu(t):
    return t * jax.nn.sigmoid(t)


def _rms_rows(t):
    return t * lax.rsqrt(jnp.mean(t * t, axis=-1, keepdims=True) + RMS_EPS)


def _rows_from(t, idx, height):
    width = t.shape[1]
    return jnp.concatenate([jnp.zeros((height, width), t.dtype) if r is None
                            else jnp.broadcast_to(t[r:r + 1, :], (height, width)) for r in idx], axis=0)


_NT = (((1,), (1,)), ((), ()))
_TN = (((0,), (0,)), ((), ()))


def _scan_tile(rows, chunk):
    return next(t for t in (4 * chunk, 2 * chunk, chunk) if rows % t == 0)


def _ada_kernel(c_ref, w_ref, b_ref, o_ref):
    s = _silu(c_ref[...]).astype(BF16)
    o_ref[...] = jnp.dot(s, w_ref[...].astype(BF16), preferred_element_type=F32) + b_ref[...]


def _ada_mods(c_rows, ada_w, ada_b):
    r = c_rows.shape[0]
    n = ada_w.shape[-1]
    tn = 1024
    out = pl.pallas_call(
        _ada_kernel,
        grid=(DEPTH, n // tn),
        in_specs=[pl.BlockSpec((r, D_MODEL), lambda l, j: (0, 0)),
                  pl.BlockSpec((None, D_MODEL, tn), lambda l, j: (l, 0, j)),
                  pl.BlockSpec((None, 1, tn), lambda l, j: (l, 0, j))],
        out_specs=pl.BlockSpec((None, r, tn), lambda l, j: (l, 0, j)),
        out_shape=jax.ShapeDtypeStruct((DEPTH, r, n), F32),
        compiler_params=_params("parallel", "parallel"),
        name="ada_mods",
    )(c_rows, ada_w, ada_b.reshape(DEPTH, 1, n))
    return out.reshape(DEPTH, r, N_MOD, D_MODEL)


def _mod_row(mod_ref, modc_ref, row, first, r0):
    m = mod_ref[row:row + 1, :]
    if r0 == 0:
        m = jnp.where(first, modc_ref[row:row + 1, :], m)
    return m


def _modulate_into(x_ref, u_ref, mod_ref, modc_ref, g_ref, shift_row, first, n_ctx):
    g = g_ref[...]
    for r0 in range(0, x_ref.shape[0], n_ctx):
        shift = _mod_row(mod_ref, modc_ref, shift_row, first, r0)
        scale = _mod_row(mod_ref, modc_ref, shift_row + 1, first, r0)
        x = x_ref[r0:r0 + n_ctx, :]
        u_ref[r0:r0 + n_ctx, :] = (_rms_rows(x) * (g * (1.0 + scale)) + shift).astype(u_ref.dtype)


def _gated_residual_into(o_ref, f_src, x_ref, mod_ref, modc_ref, g_ref, gate_row, first, n_ctx):
    g = g_ref[...]
    for r0 in range(0, x_ref.shape[0], n_ctx):
        gate = _mod_row(mod_ref, modc_ref, gate_row, first, r0)
        f = f_src[r0:r0 + n_ctx, :]
        o_ref[r0:r0 + n_ctx, :] = x_ref[r0:r0 + n_ctx, :] + gate * (_rms_rows(f) * g)


def _mod_specs(tiles_per_batch, n_batch):
    return [pl.BlockSpec((None, N_MOD, D_MODEL), lambda i, j: (i // tiles_per_batch, 0, 0)),
            pl.BlockSpec((None, N_MOD, D_MODEL), lambda i, j: (n_batch, 0, 0))]


def _modproj_kernel(x_ref, mod_ref, modc_ref, g_ref, w_ref, o_ref, u_ref, *, n_ctx, tiles_per_batch):
    first = pl.program_id(0) % tiles_per_batch == 0

    @pl.when(pl.program_id(1) == 0)
    def _():
        _modulate_into(x_ref, u_ref, mod_ref, modc_ref, g_ref, 0, first, n_ctx)

    o_ref[...] = jnp.dot(u_ref[...], w_ref[...], preferred_element_type=F32).astype(o_ref.dtype)


def _modproj(x, mod, g, w, tn, n_ctx, tm, n_batch):
    t, d = x.shape
    n = w.shape[1]
    tpb = t // n_batch // tm
    return pl.pallas_call(
        functools.partial(_modproj_kernel, n_ctx=n_ctx, tiles_per_batch=tpb),
        grid=(t // tm, n // tn),
        in_specs=[pl.BlockSpec((tm, d), lambda i, j: (i, 0))] + _mod_specs(tpb, n_batch)
        + [pl.BlockSpec((1, d), lambda i, j: (0, 0)),
           pl.BlockSpec((d, tn), lambda i, j: (0, j))],
        out_specs=pl.BlockSpec((tm, tn), lambda i, j: (i, j)),
        out_shape=jax.ShapeDtypeStruct((t, n), BF16),
        scratch_shapes=[pltpu.VMEM((tm, d), BF16)],
        compiler_params=_params("parallel", "arbitrary"),
        name="modproj",
    )(x, mod, mod, g.reshape(1, d), w)


def _outproj_kernel(y_ref, w_ref, x_ref, mod_ref, modc_ref, g_ref, o_ref, *, n_ctx, tiles_per_batch, nk):
    k = pl.program_id(1)
    first = pl.program_id(0) % tiles_per_batch == 0
    part = jnp.dot(y_ref[...], w_ref[...], preferred_element_type=F32)
    if nk == 1:
        _gated_residual_into(o_ref, part, x_ref, mod_ref, modc_ref, g_ref, 2, first, n_ctx)
        return

    @pl.when(k == 0)
    def _():
        o_ref[...] = part

    @pl.when(k > 0)
    def _():
        o_ref[...] += part

    @pl.when(k == nk - 1)
    def _():
        _gated_residual_into(o_ref, o_ref, x_ref, mod_ref, modc_ref, g_ref, 2, first, n_ctx)


def _outproj(y, w, x, mod, g, n_ctx, tm, n_batch):
    t, d = x.shape
    kdim = y.shape[1]
    tk = min(kdim, 2048)
    nk = kdim // tk
    tpb = t // n_batch // tm
    return pl.pallas_call(
        functools.partial(_outproj_kernel, n_ctx=n_ctx, tiles_per_batch=tpb, nk=nk),
        grid=(t // tm, nk),
        in_specs=[pl.BlockSpec((tm, tk), lambda i, k: (i, k)),
                  pl.BlockSpec((tk, d), lambda i, k: (k, 0)),
                  pl.BlockSpec((tm, d), lambda i, k: (i, 0))] + _mod_specs(tpb, n_batch)
        + [pl.BlockSpec((1, d), lambda i, k: (0, 0))],
        out_specs=pl.BlockSpec((tm, d), lambda i, k: (i, 0)),
        out_shape=jax.ShapeDtypeStruct((t, d), F32),
        compiler_params=_params("parallel", "arbitrary"),
        name="outproj",
    )(y, w, x, mod, mod, g.reshape(1, d))


def _ffn_kernel(x_ref, mod_ref, modc_ref, g2_ref, g3_ref, wg_ref, wu_ref, wo_ref, o_ref, u_ref,
                *, n_ctx, tiles_per_batch, nf):
    j = pl.program_id(1)
    first = pl.program_id(0) % tiles_per_batch == 0

    def hidden_tile(assign):
        u = u_ref[...]
        gate = jnp.dot(u, wg_ref[...], preferred_element_type=F32)
        up = jnp.dot(u, wu_ref[...], preferred_element_type=F32)
        h = (_silu(gate) * up).astype(BF16)
        part = jnp.dot(h, wo_ref[...], preferred_element_type=F32)
        if assign:
            o_ref[...] = part
        else:
            o_ref[...] += part

    @pl.when(j == 0)
    def _():
        _modulate_into(x_ref, u_ref, mod_ref, modc_ref, g2_ref, 3, first, n_ctx)
        hidden_tile(True)

    @pl.when(j > 0)
    def _():
        hidden_tile(False)

    @pl.when(j == nf - 1)
    def _():
        _gated_residual_into(o_ref, o_ref, x_ref, mod_ref, modc_ref, g3_ref, 5, first, n_ctx)


def _ffn(x, mod, g2, g3, w_in, w_out, n_ctx, tm, n_batch):
    t, d = x.shape
    f = w_out.shape[0]
    tf = 512
    nf = f // tf
    tpb = t // n_batch // tm
    return pl.pallas_call(
        functools.partial(_ffn_kernel, n_ctx=n_ctx, tiles_per_batch=tpb, nf=nf),
        grid=(t // tm, nf),
        in_specs=[pl.BlockSpec((tm, d), lambda i, j: (i, 0))] + _mod_specs(tpb, n_batch)
        + [pl.BlockSpec((1, d), lambda i, j: (0, 0)),
           pl.BlockSpec((1, d), lambda i, j: (0, 0)),
           pl.BlockSpec((d, tf), lambda i, j: (0, j)),
           pl.BlockSpec((d, tf), lambda i, j: (0, nf + j)),
           pl.BlockSpec((tf, d), lambda i, j: (j, 0))],
        out_specs=pl.BlockSpec((tm, d), lambda i, j: (i, 0)),
        out_shape=jax.ShapeDtypeStruct((t, d), F32),
        scratch_shapes=[pltpu.VMEM((tm, d), BF16)],
        compiler_params=_params("parallel", "arbitrary"),
        name="ffn",
    )(x, mod, mod, g2.reshape(1, d), g3.reshape(1, d), w_in, w_in, w_out)


def _rope_tables(n_ctx, n_lat):
    pos = jnp.arange(n_lat)
    row = (pos // GRID_W).astype(F32)
    col = (pos % GRID_W).astype(F32)
    inv_freq = ROPE_THETA ** (-jnp.arange(ROPE_PAIRS, dtype=F32) / ROPE_PAIRS)
    ang_row = row[:, None] * inv_freq
    ang_col = col[:, None] * inv_freq
    zero = jnp.zeros_like(ang_row)
    cos = jnp.concatenate([jnp.cos(ang_row)] * 2 + [jnp.cos(ang_col)] * 2, axis=1)
    s_lo = jnp.concatenate([-jnp.sin(ang_row), zero, -jnp.sin(ang_col), zero], axis=1)
    s_hi = jnp.concatenate([zero, jnp.sin(ang_row), zero, jnp.sin(ang_col)], axis=1)

    def full(tab, ctx_val):
        tab = jnp.concatenate([jnp.full((n_ctx, ATT_HEAD_DIM), ctx_val, F32), tab], axis=0)
        return jnp.concatenate([tab, tab], axis=1)

    return full(cos, 1.0), full(s_lo, 0.0), full(s_hi, 0.0)


def _rope(x, c, s_lo, s_hi):
    return x * c + pltpu.roll(x, LANES - ROPE_PAIRS, 1) * s_lo + pltpu.roll(x, ROPE_PAIRS, 1) * s_hi


def _attn_kernel(sink_ref, pq_ref, pk_ref, pv_ref, cq_ref, slq_ref, shq_ref, ck_ref, slk_ref, shk_ref,
                 o_ref, kx_ref, vt_ref, qt_ref, ot_ref, *, n_ctx, n_lat):
    t = pl.program_id(1)
    blk = ATT_BLOCK
    n_blocks = (n_ctx + n_lat) // blk
    ctx_blocks = n_ctx // blk
    hd = ATT_HEAD_DIM

    @pl.when(t == 0)
    def _():
        for i in range(n_blocks):
            rs = slice(i * blk, (i + 1) * blk)
            for c0 in range(0, ATT_KV_DIM, LANES):
                k = pk_ref[rs, c0:c0 + LANES].astype(F32)
                kx_ref[i, :, c0:c0 + LANES] = _rope(k, ck_ref[rs, :], slk_ref[rs, :], shk_ref[rs, :]).astype(BF16)
                vt_ref[i, c0:c0 + LANES, :] = pv_ref[rs, c0:c0 + LANES].astype(F32).T.astype(BF16)
        kx_ref[n_blocks] = jnp.zeros((blk, ATT_KV_DIM), BF16)
        vt_ref[n_blocks] = jnp.zeros((ATT_KV_DIM, blk), BF16)

    scale = ATT_HEAD_DIM ** -0.5 * LOG2_E
    for c0 in range(0, ATT_Q_DIM, LANES):
        q = pq_ref[:, 2 * ATT_KV_DIM + c0:2 * ATT_KV_DIM + c0 + LANES].astype(F32)
        qt_ref[c0:c0 + LANES, :] = (_rope(q, cq_ref[...], slq_ref[...], shq_ref[...]) * scale).T.astype(BF16)

    n = t - ctx_blocks
    key = lax.broadcasted_iota(jnp.int32, (blk, 2 * LANES), 0)
    qry = lax.broadcasted_iota(jnp.int32, (blk, 2 * LANES), 1) % blk
    blocks = [(i, None) for i in range(ctx_blocks)] + [
        (jnp.maximum(ctx_blocks + n - 1, 0), (key >= qry) & (n >= 1)),
        (jnp.maximum(ctx_blocks + n, 0), (key >= 0) & (n >= 0)),
        (ctx_blocks + n + 1, (key <= qry) & (n >= 0) & (n + 1 < n_lat // blk))]
    upper = lax.broadcasted_iota(jnp.int32, (1, 2 * LANES), 1) >= LANES
    zero = jnp.zeros((hd, LANES), BF16)

    for pair in range(ATT_KV_HEADS // 2):
        ps = slice(pair * LANES, (pair + 1) * LANES)
        for g in range(ATT_GQA):
            h0 = (2 * pair) * ATT_GQA + g
            h1 = (2 * pair + 1) * ATT_GQA + g
            w = jnp.concatenate([jnp.concatenate([qt_ref[h0 * hd:(h0 + 1) * hd, :], zero], axis=0),
                                 jnp.concatenate([zero, qt_ref[h1 * hd:(h1 + 1) * hd, :]], axis=0)], axis=1)
            sink = jnp.where(upper, sink_ref[h1], sink_ref[h0]) * LOG2_E
            scores = []
            m = sink
            for i, visible in blocks:
                s = jnp.dot(kx_ref[i, :, ps], w, preferred_element_type=F32)
                if visible is not None:
                    s = jnp.where(visible, s, NEG_INF)
                scores.append(s)
                m = jnp.maximum(m, jnp.max(s, axis=0, keepdims=True))
            den = jnp.exp2(sink - m)
            acc = jnp.zeros((LANES, 2 * LANES), F32)
            for (i, _), s in zip(blocks, scores):
                p = jnp.exp2(s - m)
                den = den + jnp.sum(p, axis=0, keepdims=True)
                acc = acc + jnp.dot(vt_ref[i, ps, :], p.astype(BF16), preferred_element_type=F32)
            ot_ref[h0 * hd:(h0 + 1) * hd, :] = acc[0:hd, 0:LANES] / den[:, 0:LANES]
            ot_ref[h1 * hd:(h1 + 1) * hd, :] = acc[hd:2 * hd, LANES:2 * LANES] / den[:, LANES:2 * LANES]

    for c0 in range(0, ATT_Q_DIM, LANES):
        o_ref[:, c0:c0 + LANES] = ot_ref[c0:c0 + LANES, :].T.astype(o_ref.dtype)


def _attention(proj, sink, tables, n_ctx, n_lat, n_batch):
    rows = n_ctx + n_lat
    nq = rows // ATT_BLOCK
    cos, s_lo, s_hi = tables
    qtab = pl.BlockSpec((ATT_BLOCK, LANES), lambda b, t: (t, 0))
    ktab = pl.BlockSpec((rows, LANES), lambda b, t: (0, 0))
    return pl.pallas_call(
        functools.partial(_attn_kernel, n_ctx=n_ctx, n_lat=n_lat),
        grid=(n_batch, nq),
        in_specs=[pl.BlockSpec(memory_space=pltpu.SMEM),
                  pl.BlockSpec((ATT_BLOCK, proj.shape[1]), lambda b, t: (b * nq + t, 0)),
                  pl.BlockSpec((rows, ATT_KV_DIM), lambda b, t: (b, 0)),
                  pl.BlockSpec((rows, ATT_KV_DIM), lambda b, t: (b, 1)),
                  qtab, qtab, qtab, ktab, ktab, ktab],
        out_specs=pl.BlockSpec((ATT_BLOCK, ATT_Q_DIM), lambda b, t: (b * nq + t, 0)),
        out_shape=jax.ShapeDtypeStruct((proj.shape[0], ATT_Q_DIM), BF16),
        scratch_shapes=[pltpu.VMEM((nq + 1, ATT_BLOCK, ATT_KV_DIM), BF16),
                        pltpu.VMEM((nq + 1, ATT_KV_DIM, ATT_BLOCK), BF16),
                        pltpu.VMEM((ATT_Q_DIM, ATT_BLOCK), BF16),
                        pltpu.VMEM((ATT_Q_DIM, ATT_BLOCK), F32)],
        compiler_params=_params("parallel", "arbitrary"),
        name="attention",
    )(sink.astype(F32), proj, proj, proj, cos, s_lo, s_hi, cos, s_lo, s_hi)


def _softplus(t):
    return jnp.maximum(t, 0.0) + jnp.log(1.0 + jnp.exp(-jnp.abs(t)))


def _split_bf16(t):
    hi = t.astype(BF16)
    return hi, (t - hi.astype(F32)).astype(BF16)


def _conv_silu_into(src_ref, pad_ref, w_ref, b_ref, dst_ref, n_ctx):
    rows, width = src_ref.shape
    pad_ref[0:8, 0:width] = jnp.zeros((8, width), F32)
    pad_ref[8:8 + rows, 0:width] = src_ref[...].astype(F32)
    pad_ref[8 + rows:16 + rows, 0:width] = jnp.zeros((8, width), F32)
    w = w_ref[...]
    b = b_ref[...]
    row = lax.broadcasted_iota(jnp.int32, (n_ctx, width), 0)
    for ci, r0 in enumerate(range(0, rows, n_ctx)):
        prv = pad_ref[7 + r0:7 + r0 + n_ctx, 0:width]
        cur = pad_ref[8 + r0:8 + r0 + n_ctx, 0:width]
        nxt = pad_ref[9 + r0:9 + r0 + n_ctx, 0:width]
        if ci == 0:
            nxt = jnp.where(row == n_ctx - 1, 0.0, nxt)
        if ci == 1:
            prv = jnp.where(row == 0, 0.0, prv)
        out = b + prv * w[0:1, :] + cur * w[1:2, :] + nxt * w[2:3, :]
        dst_ref[r0:r0 + n_ctx, :] = _silu(out).astype(dst_ref.dtype)


def _ssd_intra(t, tr, rev, add, xa_ref, ba_ref, ca_ref, dtv_ref, da_ref, y_ref, cumx_ref, xdec_ref):
    q = SSD_CHUNK
    p = SSD_HEAD_DIM
    assert 2 * q == LANES and 2 * p == LANES
    npc = tr // q
    r0 = pl.multiple_of(t * tr, tr)
    xa = xa_ref[pl.ds(r0, tr), :]
    bc = ba_ref[pl.ds(r0, tr), :]
    cc = ca_ref[pl.ds(r0, tr), :]
    dtc = dtv_ref[pl.ds(r0, tr), :]
    dac = da_ref[pl.ds(r0, tr), :]
    gd = SSD_GROUP_DIM
    lane0 = SSD_GROUP_HEADS if rev else 0
    ri = lax.broadcasted_iota(jnp.int32, (tr, tr), 0)
    ci = lax.broadcasted_iota(jnp.int32, (tr, tr), 1)
    tri = (((ri // q) == (ci // q)) & ((ri <= ci) if rev else (ri >= ci))).astype(BF16)
    da_hi, da_lo = _split_bf16(dac)
    cum_hi, cum_lo = _split_bf16(jnp.dot(tri, da_hi, preferred_element_type=F32)
                                 + jnp.dot(tri, da_lo, preferred_element_type=F32))
    cum = cum_hi.astype(F32) + cum_lo.astype(F32)
    src = lax.broadcasted_iota(jnp.int32, (LANES, gd), 0)
    dst = lax.broadcasted_iota(jnp.int32, (LANES, gd), 1) // p
    expand = (src == lane0 + dst).astype(BF16)
    dt_hi, dt_lo = _split_bf16(dtc)
    wide = (jnp.dot(jnp.concatenate([cum_hi, dt_hi], axis=0), expand, preferred_element_type=F32)
            + jnp.dot(jnp.concatenate([cum_lo, dt_lo], axis=0), expand, preferred_element_type=F32))
    cum_x = wide[0:tr]
    dt_x = wide[tr:2 * tr]
    tot_x = _rows_from(cum_x, [c * q + (0 if rev else q - 1) for c in range(npc)], q)
    xdt = xa * dt_x
    cumx_ref[pl.ds(r0, tr), :] = cum_x
    xdec_ref[pl.ds(r0, tr), :] = (xdt * jnp.exp2(tot_x - cum_x)).astype(BF16)

    row = lax.broadcasted_iota(jnp.int32, (q, LANES), 0)
    lane = lax.broadcasted_iota(jnp.int32, (q, LANES), 1)
    upper = lane >= q
    key = jnp.where(upper, lane - q, lane)
    seen = (row <= key) if rev else (row >= key)
    for c in range(npc):
        cs = slice(c * q, (c + 1) * q)
        cum_rows = jnp.concatenate([cum[cs], pltpu.roll(cum[cs], LANES - 1, 1)], axis=0).T
        cb2 = lax.dot_general(cc[cs], jnp.concatenate([bc[cs], bc[cs]], axis=0), _NT, preferred_element_type=F32)
        ys = []
        for pr in range(SSD_GROUP_HEADS // 2):
            sl = slice(pr * LANES, (pr + 1) * LANES)
            le = lane0 + 2 * pr
            decay = jnp.exp2(jnp.where(seen, cum_x[cs, sl] - cum_rows[le:le + 1, :], NEG_INF))
            xp = xdt[cs, sl]
            x2 = jnp.concatenate([jnp.where(upper, 0.0, xp), jnp.where(upper, xp, 0.0)], axis=0).astype(BF16)
            ys.append(jnp.dot((cb2 * decay).astype(BF16), x2, preferred_element_type=F32))
        y = jnp.concatenate(ys, axis=1)
        if add:
            y_ref[pl.ds(r0 + c * q, q), :] += y
        else:
            y_ref[pl.ds(r0 + c * q, q), :] = y


def _ssd_state_step(c, rev, ba_ref, ca_ref, y_ref, cumx_ref, xdec_ref, s_ref):
    q = SSD_CHUNK
    r0 = pl.multiple_of(c * q, q)
    cum_x = cumx_ref[pl.ds(r0, q), :]
    tot_x = cum_x[0:1, :] if rev else cum_x[q - 1:q, :]
    state = s_ref[...]
    y_ref[pl.ds(r0, q), :] += (jnp.dot(ca_ref[pl.ds(r0, q), :], state.astype(BF16), preferred_element_type=F32)
                               * jnp.exp2(cum_x))
    upd = lax.dot_general(ba_ref[pl.ds(r0, q), :], xdec_ref[pl.ds(r0, q), :], _TN, preferred_element_type=F32)
    s_ref[...] = state * jnp.exp2(tot_x) + upd


def _ssd_kernel(x_ref, z_ref, b_ref, c_ref, dt_ref, cwx_ref, cwb_ref, cwc_ref, cbx_ref, cbb_ref, cbc_ref,
                dtb_ref, alog_ref, dskip_ref, nw_ref, o_ref,
                xa_ref, ba_ref, ca_ref, dtv_ref, da_ref, y_ref, cumxf_ref, cumxb_ref, xdecf_ref, xdecb_ref,
                s_ref, sb_ref, *, n_ctx, n_lat, tile):
    _conv_silu_into(x_ref, y_ref, cwx_ref, cbx_ref, xa_ref, n_ctx)
    _conv_silu_into(b_ref, y_ref, cwb_ref, cbb_ref, ba_ref, n_ctx)
    _conv_silu_into(c_ref, y_ref, cwc_ref, cbc_ref, ca_ref, n_ctx)
    dtv = _softplus(dt_ref[...].astype(F32) + dtb_ref[...])
    dtv_ref[...] = dtv
    da_ref[...] = dtv * (-jnp.exp(alog_ref[...]) * LOG2_E)

    seq = (xa_ref, ba_ref, ca_ref, dtv_ref, da_ref, y_ref)
    fwd = (cumxf_ref, xdecf_ref)
    bwd = (cumxb_ref, xdecb_ref)

    def intra_step(t, carry):
        _ssd_intra(t, tile, False, False, *seq, *fwd)
        _ssd_intra(t, tile, True, True, *seq, *bwd)
        return carry

    lax.fori_loop(0, (n_ctx + n_lat) // tile, intra_step, 0)

    n_c = n_ctx // SSD_CHUNK
    n_l = n_lat // SSD_CHUNK

    def ctx_step(s, carry):
        _ssd_state_step(s, False, ba_ref, ca_ref, y_ref, *fwd, s_ref)
        _ssd_state_step(n_c - 1 - s, True, ba_ref, ca_ref, y_ref, *bwd, sb_ref)
        return carry

    def lat_step(s, carry):
        _ssd_state_step(n_c + s, False, ba_ref, ca_ref, y_ref, *fwd, s_ref)
        _ssd_state_step(n_c + n_l - 1 - s, True, ba_ref, ca_ref, y_ref, *bwd, sb_ref)
        return carry

    s_ref[...] = jnp.zeros(s_ref.shape, F32)
    sb_ref[...] = jnp.zeros(sb_ref.shape, F32)
    lax.fori_loop(0, n_c, ctx_step, 0, unroll=SCAN_UNROLL)
    lax.fori_loop(0, n_l, lat_step, 0, unroll=SCAN_UNROLL)

    for r0 in range(0, n_ctx + n_lat, n_ctx):
        y = y_ref[r0:r0 + n_ctx, :] + xa_ref[r0:r0 + n_ctx, :] * dskip_ref[...]
        t = y * _silu(z_ref[r0:r0 + n_ctx, :].astype(F32))
        o_ref[r0:r0 + n_ctx, :] = (_rms_rows(t) * nw_ref[...]).astype(o_ref.dtype)


def _ssd_layout(w_in, conv_w, conv_b, dt_bias, a_log, d_skip):
    n_state = SSD_CONV_DIM + 2 * SSD_HEADS
    gh = SSD_GROUP_HEADS

    def dt_blocks(t):
        lead = t.shape[:-1]
        t = t.reshape(lead + (2, SSD_GROUPS, gh))
        t = jnp.moveaxis(t, -3, -2).reshape(lead + (SSD_GROUPS, 2 * gh))
        t = jnp.pad(t, [(0, 0)] * (len(lead) + 1) + [(0, LANES - 2 * gh)])
        return t.reshape(lead + (SSD_GROUPS * LANES,))

    w = jnp.concatenate([w_in[:, :SSD_D_INNER], w_in[:, n_state:], w_in[:, SSD_D_INNER:SSD_CONV_DIM],
                         dt_blocks(w_in[:, SSD_CONV_DIM:n_state])], axis=1)
    return (w.astype(BF16), conv_w, conv_b.reshape(1, -1), dt_blocks(dt_bias.reshape(-1)).reshape(1, -1),
            dt_blocks(a_log.reshape(-1)).reshape(1, -1), jnp.repeat(d_skip, SSD_HEAD_DIM).reshape(1, -1))


def _ssd(proj, conv_w, conv_b, dt_bias, a_log, d_skip, norm_w, n_ctx, n_lat, n_batch):
    rows = n_ctx + n_lat
    gd = SSD_GROUP_DIM
    xb = SSD_D_INNER // gd
    bb = 2 * SSD_D_INNER // LANES
    sec = lambda width, off: pl.BlockSpec((rows, width), lambda b, g: (b, off + g))
    par = lambda r, width, off: pl.BlockSpec((r, width), lambda b, g: (0, off + g))
    return pl.pallas_call(
        functools.partial(_ssd_kernel, n_ctx=n_ctx, n_lat=n_lat, tile=_scan_tile(rows, SSD_CHUNK)),
        grid=(n_batch, SSD_GROUPS),
        in_specs=[sec(gd, 0), sec(gd, xb), sec(LANES, bb), sec(LANES, bb + SSD_GROUPS),
                  sec(LANES, bb + 2 * SSD_GROUPS),
                  par(3, gd, 0), par(3, LANES, SSD_D_INNER // LANES), par(3, LANES, SSD_D_INNER // LANES + SSD_GROUPS),
                  par(1, gd, 0), par(1, LANES, SSD_D_INNER // LANES), par(1, LANES, SSD_D_INNER // LANES + SSD_GROUPS),
                  par(1, LANES, 0), par(1, LANES, 0), par(1, gd, 0), par(1, gd, 0)],
        out_specs=pl.BlockSpec((rows, gd), lambda b, g: (b, g)),
        out_shape=jax.ShapeDtypeStruct((proj.shape[0], SSD_D_INNER), BF16),
        scratch_shapes=[pltpu.VMEM((rows, gd), F32),
                        pltpu.VMEM((rows, SSD_STATE), BF16),
                        pltpu.VMEM((rows, SSD_STATE), BF16),
                        pltpu.VMEM((rows, LANES), F32),
                        pltpu.VMEM((rows, LANES), F32),
                        pltpu.VMEM((rows + 16, gd), F32),
                        pltpu.VMEM((rows, gd), F32),
                        pltpu.VMEM((rows, gd), F32),
                        pltpu.VMEM((rows, gd), BF16),
                        pltpu.VMEM((rows, gd), BF16),
                        pltpu.VMEM((SSD_STATE, gd), F32),
                        pltpu.VMEM((SSD_STATE, gd), F32)],
        compiler_params=_params("parallel", "parallel"),
        name="ssd",
    )(proj, proj, proj, proj, proj, conv_w, conv_w, conv_w, conv_b, conv_b, conv_b,
      dt_bias, a_log, d_skip, norm_w.reshape(1, -1))


def _hgrn_intra(t, tr, rev, lg_ref, k_ref, q_ref, v_ref, o_ref, qst_ref, upd_ref, dec_ref):
    ch, sub = HG_CHUNK, HG_SUB
    npc, nb, nblk = tr // ch, ch // sub, tr // sub
    r0 = pl.multiple_of(t * tr, tr)
    lg = lg_ref[pl.ds(r0, tr), :]
    k = k_ref[pl.ds(r0, tr), :]
    q = q_ref[pl.ds(r0, tr), :]
    v = v_ref[pl.ds(r0, tr), :]
    ri = lax.broadcasted_iota(jnp.int32, (tr, tr), 0)
    ci = lax.broadcasted_iota(jnp.int32, (tr, tr), 1)
    same = (ri // ch) == (ci // ch)
    tri = (same & ((ri <= ci) if rev else (ri >= ci))).astype(BF16)
    lg_hi, lg_lo = _split_bf16(lg)
    cum = jnp.dot(tri, lg_hi, preferred_element_type=F32) + jnp.dot(tri, lg_lo, preferred_element_type=F32)

    last = [c * ch + (0 if rev else ch - 1) for c in range(npc)]
    qst_ref[pl.ds(r0, tr), :] = (q * jnp.exp2(cum)).astype(BF16)
    k_end = (k * jnp.exp2(_rows_from(cum, last, ch) - cum)).astype(BF16)
    for c in range(npc):
        chunk = t * npc + c
        dec_ref[pl.ds(chunk, 1), :] = jnp.exp2(cum[last[c]:last[c] + 1, :])
        upd_ref[pl.ds(pl.multiple_of(chunk * LANES, LANES), LANES), :] = lax.dot_general(
            v[c * ch:(c + 1) * ch], k_end[c * ch:(c + 1) * ch], _TN, preferred_element_type=F32)

    def own_edge(j):
        i = j % nb
        if rev:
            return None if i == nb - 1 else (j + 1) * sub
        return None if i == 0 else j * sub - 1

    qs = q * jnp.exp2(cum - _rows_from(cum, [own_edge(j) for j in range(nblk)], sub))
    zero8 = jnp.zeros((sub, LANES), F32)
    q_slabs, k_slabs = [], []
    for i in (range(nb - 1) if rev else range(1, nb)):
        q_slabs.append(jnp.concatenate([qs[j * sub:(j + 1) * sub] if j % nb == i else zero8
                                        for j in range(nblk)], axis=0).astype(BF16))
        pieces = []
        for c in range(npc):
            base = c * ch
            e = base + ((i + 1) * sub if rev else i * sub - 1)
            lo, hi = ((i + 1) * sub, ch) if rev else (0, i * sub)
            part = k[base + lo:base + hi] * jnp.exp2(cum[e:e + 1, :] - cum[base + lo:base + hi])
            pad = jnp.zeros((ch - (hi - lo), LANES), F32)
            pieces += [pad, part] if rev else [part, pad]
        k_slabs.append(jnp.concatenate(pieces, axis=0).astype(BF16))
    a_off = lax.dot_general(jnp.concatenate(q_slabs, axis=1), jnp.concatenate(k_slabs, axis=1), _NT,
                            preferred_element_type=F32)
    rb = (ri % ch) // sub
    cb = (ci % ch) // sub
    other = same & ((rb < cb) if rev else (rb > cb))
    o = jnp.dot(jnp.where(other, a_off, 0.0).astype(BF16), v, preferred_element_type=F32)

    tsub = lax.broadcasted_iota(jnp.int32, (tr, LANES), 0) % sub
    vf = v.astype(F32)
    prods = []
    for s in range(sub):
        idx = [j * sub + s for j in range(nblk)]
        seen = (tsub <= s) if rev else (tsub >= s)
        w = jnp.exp2(jnp.where(seen, cum - _rows_from(cum, idx, sub), NEG_INF))
        prods.append((q * _rows_from(k, idx, sub) * w).astype(BF16))
    pair = jnp.dot(jnp.concatenate(prods, axis=0), jnp.ones((LANES, LANES), BF16), preferred_element_type=F32)
    for s in range(sub):
        o = o + pair[s * tr:(s + 1) * tr, :] * _rows_from(vf, [j * sub + s for j in range(nblk)], sub)
    o_ref[pl.ds(r0, tr), :] = o


def _hgrn_state_step(c, st, qst_ref, upd_ref, dec_ref, o_ref):
    ch = HG_CHUNK
    r0 = pl.multiple_of(c * ch, ch)
    o_ref[pl.ds(r0, ch), :] += lax.dot_general(qst_ref[pl.ds(r0, ch), :], st.astype(BF16), _NT,
                                               preferred_element_type=F32)
    return st * dec_ref[pl.ds(c, 1), :] + upd_ref[pl.ds(pl.multiple_of(c * LANES, LANES), LANES), :]


def _hgrn_kernel(ff_ref, fb_ref, v_ref, q_ref, gate_ref, lb_ref, nw_ref, out_ref,
                 lgf_ref, lgb_ref, kf_ref, kb_ref, qa_ref, o_ref, ob_ref, qstf_ref, qstb_ref,
                 updf_ref, updb_ref, decf_ref, decb_ref, *, n_ctx, n_lat, tile):
    for d, (f_ref, lg_ref, k_ref) in enumerate(((ff_ref, lgf_ref, kf_ref), (fb_ref, lgb_ref, kb_ref))):
        lb = lb_ref[d:d + 1, :]
        g = lb + (1.0 - lb) * jax.nn.sigmoid(f_ref[...].astype(F32))
        lg_ref[...] = jnp.log(g) * LOG2_E
        k_ref[...] = 1.0 - g
    qa_ref[...] = _silu(q_ref[...].astype(F32))

    fwd = (qstf_ref, updf_ref, decf_ref)
    bwd = (qstb_ref, updb_ref, decb_ref)

    def intra_step(t, carry):
        _hgrn_intra(t, tile, False, lgf_ref, kf_ref, qa_ref, v_ref, o_ref, *fwd)
        _hgrn_intra(t, tile, True, lgb_ref, kb_ref, qa_ref, v_ref, ob_ref, *bwd)
        return carry

    lax.fori_loop(0, (n_ctx + n_lat) // tile, intra_step, 0)

    n_c = n_ctx // HG_CHUNK
    n_l = n_lat // HG_CHUNK

    def ctx_step(s, carry):
        return (_hgrn_state_step(s, carry[0], *fwd, o_ref),
                _hgrn_state_step(n_c - 1 - s, carry[1], *bwd, ob_ref))

    def lat_step(s, carry):
        return (_hgrn_state_step(n_c + s, carry[0], *fwd, o_ref),
                _hgrn_state_step(n_c + n_l - 1 - s, carry[1], *bwd, ob_ref))

    zero = jnp.zeros((LANES, LANES), F32)
    carry = lax.fori_loop(0, n_c, ctx_step, (zero, zero), unroll=SCAN_UNROLL)
    lax.fori_loop(0, n_l, lat_step, carry, unroll=SCAN_UNROLL)

    o = o_ref[...] + ob_ref[...]
    out_ref[...] = (_rms_rows(o) * nw_ref[...] * _silu(gate_ref[...].astype(F32))).astype(out_ref.dtype)


def _hgrn(proj, lower, norm_w, n_ctx, n_lat, n_batch):
    rows = n_ctx + n_lat
    sec = lambda off: pl.BlockSpec((rows, LANES), lambda b, h: (b, off * HG_HEADS + h))
    seq = lambda dt: pltpu.VMEM((rows, LANES), dt)
    n_chunks = rows // HG_CHUNK
    upd = pltpu.VMEM((n_chunks * LANES, LANES), F32)
    dec = pltpu.VMEM((-(-n_chunks // 8) * 8, LANES), F32)
    return pl.pallas_call(
        functools.partial(_hgrn_kernel, n_ctx=n_ctx, n_lat=n_lat, tile=_scan_tile(rows, HG_CHUNK)),
        grid=(n_batch, HG_HEADS),
        in_specs=[sec(0), sec(1), sec(2), sec(3), sec(4),
                  pl.BlockSpec((2, LANES), lambda b, h: (0, h)),
                  pl.BlockSpec((1, LANES), lambda b, h: (0, h))],
        out_specs=pl.BlockSpec((rows, LANES), lambda b, h: (b, h)),
        out_shape=jax.ShapeDtypeStruct((proj.shape[0], HG_VAL), BF16),
        scratch_shapes=[seq(F32), seq(F32), seq(F32), seq(F32), seq(F32), seq(F32), seq(F32),
                        seq(BF16), seq(BF16), upd, upd, dec, dec],
        compiler_params=_params("parallel", "parallel"),
        name="hgrn2",
    )(proj, proj, proj, proj, proj, lower, norm_w.reshape(1, -1))


def _hgrn_lower_bounds(lb_raw):
    p = jax.nn.softmax(lb_raw.astype(F32), axis=1)
    return jnp.cumsum(p, axis=1) - p[:, :1]


def kernel(x, c, ctx, c_ctx, ada_w, ada_b, norm_g, ffn_w_in, ffn_w_out, attn_w_in, attn_w_out, attn_sink,
           ssd_w_in, ssd_conv_w, ssd_conv_b, ssd_dt_bias, ssd_a_log, ssd_d, ssd_norm_w, ssd_w_out,
           hgrn_w_in, hgrn_lb, hgrn_norm_w, hgrn_w_out):
    n_batch, n_lat, d = x.shape
    n_ctx = ctx.shape[1]
    rows = n_ctx + n_lat
    tm = _token_tile(n_ctx, n_lat)

    r = jnp.concatenate([ctx, x], axis=1).reshape(n_batch * rows, d)
    n_cond = -(-(n_batch + 1) // 8) * 8
    c_rows = jnp.concatenate([c, c_ctx[None, :], jnp.zeros((n_cond - n_batch - 1, d), F32)], axis=0)
    mods = _ada_mods(c_rows, ada_w, ada_b)
    tables = _rope_tables(n_ctx, n_lat)
    lower = _hgrn_lower_bounds(hgrn_lb)

    for i in range(DEPTH):
        kind, j = i % N_MIXERS, i // N_MIXERS
        mod = mods[i]
        if kind == 0:
            proj = _modproj(r, mod, norm_g[i, 0], attn_w_in[j].astype(BF16), 512, n_ctx, tm, n_batch)
            y = _attention(proj, attn_sink[j], tables, n_ctx, n_lat, n_batch)
            w_out = attn_w_out[j]
        elif kind == 1:
            w_in, conv_w, conv_b, dt_bias, a_log, d_skip = _ssd_layout(
                ssd_w_in[j], ssd_conv_w[j], ssd_conv_b[j], ssd_dt_bias[j], ssd_a_log[j], ssd_d[j])
            proj = _modproj(r, mod, norm_g[i, 0], w_in, 1024, n_ctx, tm, n_batch)
            y = _ssd(proj, conv_w, conv_b, dt_bias, a_log, d_skip, ssd_norm_w[j], n_ctx, n_lat, n_batch)
            w_out = ssd_w_out[j]
        else:
            proj = _modproj(r, mod, norm_g[i, 0], hgrn_w_in[j].astype(BF16), 1024, n_ctx, tm, n_batch)
            y = _hgrn(proj, lower[:, i], hgrn_norm_w[j], n_ctx, n_lat, n_batch)
            w_out = hgrn_w_out[j]
        r = _outproj(y, w_out.astype(BF16), r, mod, norm_g[i, 1], n_ctx, tm, n_batch)
        r = _ffn(r, mod, norm_g[i, 2], norm_g[i, 3], ffn_w_in[i].astype(BF16), ffn_w_out[i].astype(BF16),
                 n_ctx, tm, n_batch)
    return r.reshape(n_batch, rows, d)[:, n_ctx:]
```

```python
import functools
import math

import jax
import jax.numpy as jnp
from jax import lax
from jax.experimental import pallas as pl
from jax.experimental.pallas import tpu as pltpu

F32 = jnp.float32
BF16 = jnp.bfloat16

D_MODEL = 2048
DEPTH = 4
N_MIXERS = 3
GRID_W = 64
RMS_EPS = 1e-6
N_MOD = 6

ATT_HEADS = 32
ATT_KV_HEADS = 4
ATT_GQA = ATT_HEADS // ATT_KV_HEADS
ATT_HEAD_DIM = 64
ATT_WINDOW = 128
ATT_BLOCK = 128
ATT_Q_DIM = ATT_HEADS * ATT_HEAD_DIM
ATT_KV_DIM = ATT_KV_HEADS * ATT_HEAD_DIM
ROPE_THETA = 10000.0
ROPE_PAIRS = ATT_HEAD_DIM // 4
NEG_INF = -1e30
LOG2_E = 1.0 / math.log(2.0)

SSD_D_INNER = 2 * D_MODEL
SSD_HEAD_DIM = 64
SSD_HEADS = SSD_D_INNER // SSD_HEAD_DIM
SSD_GROUPS = 8
SSD_GROUP_HEADS = SSD_HEADS // SSD_GROUPS
SSD_GROUP_DIM = SSD_D_INNER // SSD_GROUPS
SSD_STATE = 128
SSD_GN = SSD_GROUPS * SSD_STATE
SSD_CONV_DIM = SSD_D_INNER + 2 * SSD_GN
SSD_CHUNK = 64

HG_EXPAND = 128
HG_HEADS = D_MODEL // HG_EXPAND
HG_KEY = HG_HEADS * HG_EXPAND
HG_VAL = D_MODEL
HG_CHUNK = 64
HG_SUB = 8

FFN_DIM = -(-8 * D_MODEL // (3 * 256)) * 256

LANES = 128
VMEM_LIMIT_BYTES = 56 * 1024 * 1024
MAX_TOKEN_TILE = 768
SCAN_UNROLL = 2


def _params(*sem):
    return pltpu.CompilerParams(dimension_semantics=sem, vmem_limit_bytes=VMEM_LIMIT_BYTES)


def _token_tile(n_ctx, rows):
    best = n_ctx
    for mult in range(1, rows // n_ctx + 1):
        t = mult * n_ctx
        if rows % t == 0 and t <= MAX_TOKEN_TILE:
            best = t
    assert rows % best == 0
    return best


def _silu(t):
    return t * jax.nn.sigmoid(t)


def _rms_rows(t):
    return t * lax.rsqrt(jnp.mean(t * t, axis=-1, keepdims=True) + RMS_EPS)


def _rows_from(t, idx, height):
    width = t.shape[1]
    return jnp.concatenate([jnp.zeros((height, width), t.dtype) if r is None
                            else jnp.broadcast_to(t[r:r + 1, :], (height, width)) for r in idx], axis=0)


_NT = (((1,), (1,)), ((), ()))
_TN = (((0,), (0,)), ((), ()))


def _scan_tile(rows, chunk):
    return next(t for t in (4 * chunk, 2 * chunk) if rows % t == 0 and t % LANES == 0)


def _ada_kernel(c_ref, w_ref, b_ref, o_ref):
    s = _silu(c_ref[...]).astype(BF16)
    o_ref[...] = jnp.dot(s, w_ref[...].astype(BF16), preferred_element_type=F32) + b_ref[...]


def _ada_mods(c_rows, ada_w, ada_b):
    r = c_rows.shape[0]
    n = ada_w.shape[-1]
    tn = 1024
    out = pl.pallas_call(
        _ada_kernel,
        grid=(DEPTH, n // tn),
        in_specs=[pl.BlockSpec((r, D_MODEL), lambda l, j: (0, 0)),
                  pl.BlockSpec((None, D_MODEL, tn), lambda l, j: (l, 0, j)),
                  pl.BlockSpec((None, 1, tn), lambda l, j: (l, 0, j))],
        out_specs=pl.BlockSpec((None, r, tn), lambda l, j: (l, 0, j)),
        out_shape=jax.ShapeDtypeStruct((DEPTH, r, n), F32),
        compiler_params=_params("parallel", "parallel"),
        name="ada_mods",
    )(c_rows, ada_w, ada_b.reshape(DEPTH, 1, n))
    return out.reshape(DEPTH, r, N_MOD, D_MODEL)


def _first_tile(tiles_per_batch, has_ctx):
    return (pl.program_id(0) % tiles_per_batch == 0) if has_ctx else None


def _mod_row(mod_ref, modc_ref, row, first, r0):
    m = mod_ref[row:row + 1, :]
    if r0 == 0 and first is not None:
        m = jnp.where(first, modc_ref[row:row + 1, :], m)
    return m


def _modulate_into(x_ref, u_ref, mod_ref, modc_ref, g_ref, shift_row, first, n_ctx):
    g = g_ref[...]
    for r0 in range(0, x_ref.shape[0], n_ctx):
        shift = _mod_row(mod_ref, modc_ref, shift_row, first, r0)
        scale = _mod_row(mod_ref, modc_ref, shift_row + 1, first, r0)
        x = x_ref[r0:r0 + n_ctx, :]
        u_ref[r0:r0 + n_ctx, :] = (_rms_rows(x) * (g * (1.0 + scale)) + shift).astype(u_ref.dtype)


def _gated_residual_into(o_ref, f_src, x_ref, mod_ref, modc_ref, g_ref, gate_row, first, n_ctx):
    g = g_ref[...]
    for r0 in range(0, x_ref.shape[0], n_ctx):
        gate = _mod_row(mod_ref, modc_ref, gate_row, first, r0)
        f = f_src[r0:r0 + n_ctx, :]
        o_ref[r0:r0 + n_ctx, :] = x_ref[r0:r0 + n_ctx, :] + gate * (_rms_rows(f) * g)


def _mod_specs(tiles_per_batch, n_batch):
    return [pl.BlockSpec((None, N_MOD, D_MODEL), lambda i, j: (i // tiles_per_batch, 0, 0)),
            pl.BlockSpec((None, N_MOD, D_MODEL), lambda i, j: (n_batch, 0, 0))]


def _modproj_kernel(x_ref, mod_ref, modc_ref, g_ref, w_ref, o_ref, u_ref, *, n_ctx, tiles_per_batch,
                    has_ctx=True):
    first = _first_tile(tiles_per_batch, has_ctx)

    @pl.when(pl.program_id(1) == 0)
    def _():
        _modulate_into(x_ref, u_ref, mod_ref, modc_ref, g_ref, 0, first, n_ctx)

    o_ref[...] = jnp.dot(u_ref[...], w_ref[...], preferred_element_type=F32).astype(o_ref.dtype)


def _modproj(x, mod, g, w, tn, n_ctx, tm, n_batch):
    t, d = x.shape
    n = w.shape[1]
    tpb = t // n_batch // tm
    return pl.pallas_call(
        functools.partial(_modproj_kernel, n_ctx=n_ctx, tiles_per_batch=tpb),
        grid=(t // tm, n // tn),
        in_specs=[pl.BlockSpec((tm, d), lambda i, j: (i, 0))] + _mod_specs(tpb, n_batch)
        + [pl.BlockSpec((1, d), lambda i, j: (0, 0)),
           pl.BlockSpec((d, tn), lambda i, j: (0, j))],
        out_specs=pl.BlockSpec((tm, tn), lambda i, j: (i, j)),
        out_shape=jax.ShapeDtypeStruct((t, n), BF16),
        scratch_shapes=[pltpu.VMEM((tm, d), BF16)],
        compiler_params=_params("parallel", "arbitrary"),
        name="modproj",
    )(x, mod, mod, g.reshape(1, d), w)


def _outproj_kernel(y_ref, w_ref, x_ref, mod_ref, modc_ref, g_ref, o_ref, *, n_ctx, tiles_per_batch, nk,
                    has_ctx):
    k = pl.program_id(1)
    first = _first_tile(tiles_per_batch, has_ctx)
    part = jnp.dot(y_ref[...], w_ref[...], preferred_element_type=F32)
    if nk == 1:
        _gated_residual_into(o_ref, part, x_ref, mod_ref, modc_ref, g_ref, 2, first, n_ctx)
        return

    @pl.when(k == 0)
    def _():
        o_ref[...] = part

    @pl.when(k > 0)
    def _():
        o_ref[...] += part

    @pl.when(k == nk - 1)
    def _():
        _gated_residual_into(o_ref, o_ref, x_ref, mod_ref, modc_ref, g_ref, 2, first, n_ctx)


def _outproj(y, w, x, mod, g, n_ctx, tm, n_batch, lat_rows=None):
    t, d = x.shape
    kdim = y.shape[1]
    tk = min(kdim, 2048)
    nk = kdim // tk
    rows = t // n_batch
    if lat_rows is None:
        tpb, t_out = rows // tm, t
        y_spec = pl.BlockSpec((tm, tk), lambda i, k: (i, k))
        x_spec = pl.BlockSpec((tm, d), lambda i, k: (i, 0))
    else:
        tpb, t_out = lat_rows // tm, n_batch * lat_rows
        assert rows % n_ctx == 0 and lat_rows % n_ctx == 0 and tm % n_ctx == 0
        first_row = lambda i: pl.multiple_of(
            ((i // tpb) * (rows // n_ctx) + (rows - lat_rows) // n_ctx + (i % tpb) * (tm // n_ctx)) * n_ctx, n_ctx)
        y_spec = pl.BlockSpec((pl.Element(tm), pl.Element(tk)), lambda i, k: (first_row(i), k * tk))
        x_spec = pl.BlockSpec((pl.Element(tm), pl.Element(d)), lambda i, k: (first_row(i), 0))
    return pl.pallas_call(
        functools.partial(_outproj_kernel, n_ctx=n_ctx, tiles_per_batch=tpb, nk=nk, has_ctx=lat_rows is None),
        grid=(t_out // tm, nk),
        in_specs=[y_spec, pl.BlockSpec((tk, d), lambda i, k: (k, 0)), x_spec] + _mod_specs(tpb, n_batch)
        + [pl.BlockSpec((1, d), lambda i, k: (0, 0))],
        out_specs=pl.BlockSpec((tm, d), lambda i, k: (i, 0)),
        out_shape=jax.ShapeDtypeStruct((t_out, d), F32),
        compiler_params=_params("parallel", "arbitrary"),
        name="outproj",
    )(y, w, x, mod, mod, g.reshape(1, d))


def _ffn_kernel(x_ref, mod_ref, modc_ref, g2_ref, g3_ref, wg_ref, wu_ref, wo_ref, o_ref, u_ref,
                *, n_ctx, tiles_per_batch, nf, has_ctx):
    j = pl.program_id(1)
    first = _first_tile(tiles_per_batch, has_ctx)

    def hidden_tile(assign):
        u = u_ref[...]
        gate = jnp.dot(u, wg_ref[...], preferred_element_type=F32)
        up = jnp.dot(u, wu_ref[...], preferred_element_type=F32)
        h = (_silu(gate) * up).astype(BF16)
        part = jnp.dot(h, wo_ref[...], preferred_element_type=F32)
        if assign:
            o_ref[...] = part
        else:
            o_ref[...] += part

    @pl.when(j == 0)
    def _():
        _modulate_into(x_ref, u_ref, mod_ref, modc_ref, g2_ref, 3, first, n_ctx)
        hidden_tile(True)

    @pl.when(j > 0)
    def _():
        hidden_tile(False)

    @pl.when(j == nf - 1)
    def _():
        _gated_residual_into(o_ref, o_ref, x_ref, mod_ref, modc_ref, g3_ref, 5, first, n_ctx)


def _ffn(x, mod, g2, g3, w_in, w_out, n_ctx, tm, n_batch, has_ctx=True):
    t, d = x.shape
    f = w_out.shape[0]
    tf = 512
    nf = f // tf
    tpb = t // n_batch // tm
    return pl.pallas_call(
        functools.partial(_ffn_kernel, n_ctx=n_ctx, tiles_per_batch=tpb, nf=nf, has_ctx=has_ctx),
        grid=(t // tm, nf),
        in_specs=[pl.BlockSpec((tm, d), lambda i, j: (i, 0))] + _mod_specs(tpb, n_batch)
        + [pl.BlockSpec((1, d), lambda i, j: (0, 0)),
           pl.BlockSpec((1, d), lambda i, j: (0, 0)),
           pl.BlockSpec((d, tf), lambda i, j: (0, j)),
           pl.BlockSpec((d, tf), lambda i, j: (0, nf + j)),
           pl.BlockSpec((tf, d), lambda i, j: (j, 0))],
        out_specs=pl.BlockSpec((tm, d), lambda i, j: (i, 0)),
        out_shape=jax.ShapeDtypeStruct((t, d), F32),
        scratch_shapes=[pltpu.VMEM((tm, d), BF16)],
        compiler_params=_params("parallel", "arbitrary"),
        name="ffn",
    )(x, mod, mod, g2.reshape(1, d), g3.reshape(1, d), w_in, w_in, w_out)


def _rope_tables(n_ctx, n_lat):
    pos = jnp.arange(n_lat)
    row = (pos // GRID_W).astype(F32)
    col = (pos % GRID_W).astype(F32)
    inv_freq = ROPE_THETA ** (-jnp.arange(ROPE_PAIRS, dtype=F32) / ROPE_PAIRS)
    ang_row = row[:, None] * inv_freq
    ang_col = col[:, None] * inv_freq
    zero = jnp.zeros_like(ang_row)
    cos = jnp.concatenate([jnp.cos(ang_row)] * 2 + [jnp.cos(ang_col)] * 2, axis=1)
    s_lo = jnp.concatenate([-jnp.sin(ang_row), zero, -jnp.sin(ang_col), zero], axis=1)
    s_hi = jnp.concatenate([zero, jnp.sin(ang_row), zero, jnp.sin(ang_col)], axis=1)

    def full(tab, ctx_val):
        tab = jnp.concatenate([jnp.full((n_ctx, ATT_HEAD_DIM), ctx_val, F32), tab], axis=0)
        return jnp.concatenate([tab, tab], axis=1)

    return full(cos, 1.0), full(s_lo, 0.0), full(s_hi, 0.0)


def _rope(x, c, s_lo, s_hi):
    return x * c + pltpu.roll(x, LANES - ROPE_PAIRS, 1) * s_lo + pltpu.roll(x, ROPE_PAIRS, 1) * s_hi


def _attn_kernel(sink_ref, pq_ref, pk_ref, pv_ref, cq_ref, slq_ref, shq_ref, ck_ref, slk_ref, shk_ref,
                 o_ref, kx_ref, vt_ref, qt_ref, ot_ref, *, n_ctx, n_lat):
    t = pl.program_id(1)
    blk = ATT_BLOCK
    n_blocks = (n_ctx + n_lat) // blk
    ctx_blocks = n_ctx // blk
    hd = ATT_HEAD_DIM

    @pl.when(t == 0)
    def _():
        for i in range(n_blocks):
            rs = slice(i * blk, (i + 1) * blk)
            for c0 in range(0, ATT_KV_DIM, LANES):
                k = pk_ref[rs, c0:c0 + LANES].astype(F32)
                kx_ref[i, :, c0:c0 + LANES] = _rope(k, ck_ref[rs, :], slk_ref[rs, :], shk_ref[rs, :]).astype(BF16)
                vt_ref[i, c0:c0 + LANES, :] = pv_ref[rs, c0:c0 + LANES].astype(F32).T.astype(BF16)
        kx_ref[n_blocks] = jnp.zeros((blk, ATT_KV_DIM), BF16)
        vt_ref[n_blocks] = jnp.zeros((ATT_KV_DIM, blk), BF16)

    scale = ATT_HEAD_DIM ** -0.5 * LOG2_E
    for c0 in range(0, ATT_Q_DIM, LANES):
        q = pq_ref[:, 2 * ATT_KV_DIM + c0:2 * ATT_KV_DIM + c0 + LANES].astype(F32)
        qt_ref[c0:c0 + LANES, :] = (_rope(q, cq_ref[...], slq_ref[...], shq_ref[...]) * scale).T.astype(BF16)

    n = t - ctx_blocks
    key = lax.broadcasted_iota(jnp.int32, (blk, 2 * LANES), 0)
    qry = lax.broadcasted_iota(jnp.int32, (blk, 2 * LANES), 1) % blk
    blocks = [(i, None) for i in range(ctx_blocks)] + [
        (jnp.maximum(ctx_blocks + n - 1, 0), (key >= qry) & (n >= 1)),
        (jnp.maximum(ctx_blocks + n, 0), (key >= 0) & (n >= 0)),
        (ctx_blocks + n + 1, (key <= qry) & (n >= 0) & (n + 1 < n_lat // blk))]
    upper = lax.broadcasted_iota(jnp.int32, (1, 2 * LANES), 1) >= LANES
    zero = jnp.zeros((hd, LANES), BF16)

    for pair in range(ATT_KV_HEADS // 2):
        ps = slice(pair * LANES, (pair + 1) * LANES)
        for g in range(ATT_GQA):
            h0 = (2 * pair) * ATT_GQA + g
            h1 = (2 * pair + 1) * ATT_GQA + g
            w = jnp.concatenate([jnp.concatenate([qt_ref[h0 * hd:(h0 + 1) * hd, :], zero], axis=0),
                                 jnp.concatenate([zero, qt_ref[h1 * hd:(h1 + 1) * hd, :]], axis=0)], axis=1)
            sink = jnp.where(upper, sink_ref[h1], sink_ref[h0]) * LOG2_E
            scores = []
            m = sink
            for i, visible in blocks:
                s = jnp.dot(kx_ref[i, :, ps], w, preferred_element_type=F32)
                if visible is not None:
                    s = jnp.where(visible, s, NEG_INF)
                scores.append(s)
                m = jnp.maximum(m, jnp.max(s, axis=0, keepdims=True))
            den = jnp.exp2(sink - m)
            acc = jnp.zeros((LANES, 2 * LANES), F32)
            for (i, _), s in zip(blocks, scores):
                p = jnp.exp2(s - m)
                den = den + jnp.sum(p, axis=0, keepdims=True)
                acc = acc + jnp.dot(vt_ref[i, ps, :], p.astype(BF16), preferred_element_type=F32)
            ot_ref[h0 * hd:(h0 + 1) * hd, :] = acc[0:hd, 0:LANES] / den[:, 0:LANES]
            ot_ref[h1 * hd:(h1 + 1) * hd, :] = acc[hd:2 * hd, LANES:2 * LANES] / den[:, LANES:2 * LANES]

    for c0 in range(0, ATT_Q_DIM, LANES):
        o_ref[:, c0:c0 + LANES] = ot_ref[c0:c0 + LANES, :].T.astype(o_ref.dtype)


def _attention(proj, sink, tables, n_ctx, n_lat, n_batch):
    rows = n_ctx + n_lat
    nq = rows // ATT_BLOCK
    cos, s_lo, s_hi = tables
    qtab = pl.BlockSpec((ATT_BLOCK, LANES), lambda b, t: (t, 0))
    ktab = pl.BlockSpec((rows, LANES), lambda b, t: (0, 0))
    return pl.pallas_call(
        functools.partial(_attn_kernel, n_ctx=n_ctx, n_lat=n_lat),
        grid=(n_batch, nq),
        in_specs=[pl.BlockSpec(memory_space=pltpu.SMEM),
                  pl.BlockSpec((ATT_BLOCK, proj.shape[1]), lambda b, t: (b * nq + t, 0)),
                  pl.BlockSpec((rows, ATT_KV_DIM), lambda b, t: (b, 0)),
                  pl.BlockSpec((rows, ATT_KV_DIM), lambda b, t: (b, 1)),
                  qtab, qtab, qtab, ktab, ktab, ktab],
        out_specs=pl.BlockSpec((ATT_BLOCK, ATT_Q_DIM), lambda b, t: (b * nq + t, 0)),
        out_shape=jax.ShapeDtypeStruct((proj.shape[0], ATT_Q_DIM), BF16),
        scratch_shapes=[pltpu.VMEM((nq + 1, ATT_BLOCK, ATT_KV_DIM), BF16),
                        pltpu.VMEM((nq + 1, ATT_KV_DIM, ATT_BLOCK), BF16),
                        pltpu.VMEM((ATT_Q_DIM, ATT_BLOCK), BF16),
                        pltpu.VMEM((ATT_Q_DIM, ATT_BLOCK), F32)],
        compiler_params=_params("parallel", "arbitrary"),
        name="attention",
    )(sink.astype(F32), proj, proj, proj, cos, s_lo, s_hi, cos, s_lo, s_hi)


def _softplus(t):
    return jnp.maximum(t, 0.0) + jnp.log(1.0 + jnp.exp(-jnp.abs(t)))


def _split_bf16(t):
    hi = t.astype(BF16)
    return hi, (t - hi.astype(F32)).astype(BF16)


def _conv_silu_into(src_ref, w_ref, b_ref, dst_ref, n_ctx):
    rows, width = src_ref.shape
    t = n_ctx
    ri = lax.broadcasted_iota(jnp.int32, (t, t), 0)
    ci = lax.broadcasted_iota(jnp.int32, (t, t), 1)
    down = (ri == ci + 1).astype(BF16)
    up = (ri + 1 == ci).astype(BF16)
    row8 = lax.broadcasted_iota(jnp.int32, (8, width), 0)
    w = w_ref[...]
    b = b_ref[...]
    for r0 in range(0, rows, t):
        x = src_ref[r0:r0 + t, :]
        prv = jnp.dot(down, x, preferred_element_type=F32)
        nxt = jnp.dot(up, x, preferred_element_type=F32)
        if r0 not in (0, n_ctx):
            edge = jnp.broadcast_to(src_ref[r0 - 1:r0, :].astype(F32), (8, width))
            prv = jnp.concatenate([jnp.where(row8 == 0, edge, prv[0:8]), prv[8:]], axis=0)
        if r0 + t not in (n_ctx, rows):
            edge = jnp.broadcast_to(src_ref[r0 + t:r0 + t + 1, :].astype(F32), (8, width))
            nxt = jnp.concatenate([nxt[:t - 8], jnp.where(row8 == 7, edge, nxt[t - 8:])], axis=0)
        out = b + prv * w[0:1, :] + x.astype(F32) * w[1:2, :] + nxt * w[2:3, :]
        dst_ref[r0:r0 + t, :] = _silu(out).astype(dst_ref.dtype)


def _ssd_intra(t, tr, rev, add, xa_ref, ba_ref, ca_ref, dtv_ref, da_ref, y_ref, cumx_ref, xdec_ref):
    q = SSD_CHUNK
    p = SSD_HEAD_DIM
    assert 2 * q == LANES and 2 * p == LANES
    npc = tr // q
    r0 = pl.multiple_of(t * tr, tr)
    xa = xa_ref[pl.ds(r0, tr), :]
    bc = ba_ref[pl.ds(r0, tr), :]
    cc = ca_ref[pl.ds(r0, tr), :]
    dtc = dtv_ref[pl.ds(r0, tr), :]
    dac = da_ref[pl.ds(r0, tr), :]
    gd = SSD_GROUP_DIM
    lane0 = SSD_GROUP_HEADS if rev else 0
    ri = lax.broadcasted_iota(jnp.int32, (tr, tr), 0)
    ci = lax.broadcasted_iota(jnp.int32, (tr, tr), 1)
    tri = (((ri // q) == (ci // q)) & ((ri <= ci) if rev else (ri >= ci))).astype(BF16)
    da_hi, da_lo = _split_bf16(dac)
    cum_hi, cum_lo = _split_bf16(jnp.dot(tri, da_hi, preferred_element_type=F32)
                                 + jnp.dot(tri, da_lo, preferred_element_type=F32))
    cum = cum_hi.astype(F32) + cum_lo.astype(F32)
    src = lax.broadcasted_iota(jnp.int32, (LANES, gd), 0)
    dst = lax.broadcasted_iota(jnp.int32, (LANES, gd), 1) // p
    expand = (src == lane0 + dst).astype(BF16)
    dt_hi, dt_lo = _split_bf16(dtc)
    wide = (jnp.dot(jnp.concatenate([cum_hi, dt_hi], axis=0), expand, preferred_element_type=F32)
            + jnp.dot(jnp.concatenate([cum_lo, dt_lo], axis=0), expand, preferred_element_type=F32))
    cum_x = wide[0:tr]
    dt_x = wide[tr:2 * tr]
    tot_x = _rows_from(cum_x, [c * q + (0 if rev else q - 1) for c in range(npc)], q)
    xdt = xa * dt_x
    cumx_ref[pl.ds(r0, tr), :] = cum_x
    xdec_ref[pl.ds(r0, tr), :] = (xdt * jnp.exp2(tot_x - cum_x)).astype(BF16)

    row = lax.broadcasted_iota(jnp.int32, (q, LANES), 0)
    lane = lax.broadcasted_iota(jnp.int32, (q, LANES), 1)
    upper = lane >= q
    key = jnp.where(upper, lane - q, lane)
    seen = (row <= key) if rev else (row >= key)
    for c in range(npc):
        cs = slice(c * q, (c + 1) * q)
        cum_rows = jnp.concatenate([cum[cs], pltpu.roll(cum[cs], LANES - 1, 1)], axis=0).T
        cb2 = lax.dot_general(cc[cs], jnp.concatenate([bc[cs], bc[cs]], axis=0), _NT, preferred_element_type=F32)
        ys = []
        for pr in range(SSD_GROUP_HEADS // 2):
            sl = slice(pr * LANES, (pr + 1) * LANES)
            le = lane0 + 2 * pr
            decay = jnp.exp2(jnp.where(seen, cum_x[cs, sl] - cum_rows[le:le + 1, :], NEG_INF))
            xp = xdt[cs, sl]
            x2 = jnp.concatenate([jnp.where(upper, 0.0, xp), jnp.where(upper, xp, 0.0)], axis=0).astype(BF16)
            ys.append(jnp.dot((cb2 * decay).astype(BF16), x2, preferred_element_type=F32))
        y = jnp.concatenate(ys, axis=1)
        if add:
            y_ref[pl.ds(r0 + c * q, q), :] += y
        else:
            y_ref[pl.ds(r0 + c * q, q), :] = y


def _ssd_state_step(c, rev, ba_ref, ca_ref, y_ref, cumx_ref, xdec_ref, s_ref):
    q = SSD_CHUNK
    r0 = pl.multiple_of(c * q, q)
    cum_x = cumx_ref[pl.ds(r0, q), :]
    tot_x = cum_x[0:1, :] if rev else cum_x[q - 1:q, :]
    state = s_ref[...]
    y_ref[pl.ds(r0, q), :] += (jnp.dot(ca_ref[pl.ds(r0, q), :], state.astype(BF16), preferred_element_type=F32)
                               * jnp.exp2(cum_x))
    upd = lax.dot_general(ba_ref[pl.ds(r0, q), :], xdec_ref[pl.ds(r0, q), :], _TN, preferred_element_type=F32)
    s_ref[...] = state * jnp.exp2(tot_x) + upd


def _ssd_kernel(x_ref, z_ref, b_ref, c_ref, dt_ref, cwx_ref, cwb_ref, cwc_ref, cbx_ref, cbb_ref, cbc_ref,
                dtb_ref, alog_ref, dskip_ref, nw_ref, o_ref,
                xa_ref, ba_ref, ca_ref, dtv_ref, da_ref, y_ref, cumxf_ref, cumxb_ref, xdecf_ref, xdecb_ref,
                s_ref, sb_ref, *, n_ctx, n_lat, tile):
    _conv_silu_into(x_ref, cwx_ref, cbx_ref, xa_ref, n_ctx)
    _conv_silu_into(b_ref, cwb_ref, cbb_ref, ba_ref, n_ctx)
    _conv_silu_into(c_ref, cwc_ref, cbc_ref, ca_ref, n_ctx)
    dtv = _softplus(dt_ref[...].astype(F32) + dtb_ref[...])
    dtv_ref[...] = dtv
    da_ref[...] = dtv * (-jnp.exp(alog_ref[...]) * LOG2_E)

    seq = (xa_ref, ba_ref, ca_ref, dtv_ref, da_ref, y_ref)
    fwd = (cumxf_ref, xdecf_ref)
    bwd = (cumxb_ref, xdecb_ref)

    def intra_step(t, carry):
        _ssd_intra(t, tile, False, False, *seq, *fwd)
        _ssd_intra(t, tile, True, True, *seq, *bwd)
        return carry

    lax.fori_loop(0, (n_ctx + n_lat) // tile, intra_step, 0)

    n_c = n_ctx // SSD_CHUNK
    n_l = n_lat // SSD_CHUNK

    def ctx_step(s, carry):
        _ssd_state_step(s, False, ba_ref, ca_ref, y_ref, *fwd, s_ref)
        _ssd_state_step(n_c - 1 - s, True, ba_ref, ca_ref, y_ref, *bwd, sb_ref)
        return carry

    def lat_step(s, carry):
        _ssd_state_step(n_c + s, False, ba_ref, ca_ref, y_ref, *fwd, s_ref)
        _ssd_state_step(n_c + n_l - 1 - s, True, ba_ref, ca_ref, y_ref, *bwd, sb_ref)
        return carry

    s_ref[...] = jnp.zeros(s_ref.shape, F32)
    sb_ref[...] = jnp.zeros(sb_ref.shape, F32)
    lax.fori_loop(0, n_c, ctx_step, 0, unroll=SCAN_UNROLL)
    lax.fori_loop(0, n_l, lat_step, 0, unroll=SCAN_UNROLL)

    for r0 in range(0, n_ctx + n_lat, n_ctx):
        y = y_ref[r0:r0 + n_ctx, :] + xa_ref[r0:r0 + n_ctx, :] * dskip_ref[...]
        t = y * _silu(z_ref[r0:r0 + n_ctx, :].astype(F32))
        o_ref[r0:r0 + n_ctx, :] = (_rms_rows(t) * nw_ref[...]).astype(o_ref.dtype)


def _ssd_layout(w_in, conv_w, conv_b, dt_bias, a_log, d_skip):
    n_state = SSD_CONV_DIM + 2 * SSD_HEADS
    gh = SSD_GROUP_HEADS

    def dt_blocks(t):
        lead = t.shape[:-1]
        t = t.reshape(lead + (2, SSD_GROUPS, gh))
        t = jnp.moveaxis(t, -3, -2).reshape(lead + (SSD_GROUPS, 2 * gh))
        t = jnp.pad(t, [(0, 0)] * (len(lead) + 1) + [(0, LANES - 2 * gh)])
        return t.reshape(lead + (SSD_GROUPS * LANES,))

    w = jnp.concatenate([w_in[:, :SSD_D_INNER], w_in[:, n_state:], w_in[:, SSD_D_INNER:SSD_CONV_DIM],
                         dt_blocks(w_in[:, SSD_CONV_DIM:n_state])], axis=1)
    return (w.astype(BF16), conv_w, conv_b.reshape(1, -1), dt_blocks(dt_bias.reshape(-1)).reshape(1, -1),
            dt_blocks(a_log.reshape(-1)).reshape(1, -1), jnp.repeat(d_skip, SSD_HEAD_DIM).reshape(1, -1))


def _ssd(proj, conv_w, conv_b, dt_bias, a_log, d_skip, norm_w, n_ctx, n_lat, n_batch):
    rows = n_ctx + n_lat
    gd = SSD_GROUP_DIM
    xb = SSD_D_INNER // gd
    bb = 2 * SSD_D_INNER // LANES
    sec = lambda width, off: pl.BlockSpec((rows, width), lambda b, g: (b, off + g))
    par = lambda r, width, off: pl.BlockSpec((r, width), lambda b, g: (0, off + g))
    return pl.pallas_call(
        functools.partial(_ssd_kernel, n_ctx=n_ctx, n_lat=n_lat, tile=_scan_tile(rows, SSD_CHUNK)),
        grid=(n_batch, SSD_GROUPS),
        in_specs=[sec(gd, 0), sec(gd, xb), sec(LANES, bb), sec(LANES, bb + SSD_GROUPS),
                  sec(LANES, bb + 2 * SSD_GROUPS),
                  par(3, gd, 0), par(3, LANES, SSD_D_INNER // LANES), par(3, LANES, SSD_D_INNER // LANES + SSD_GROUPS),
                  par(1, gd, 0), par(1, LANES, SSD_D_INNER // LANES), par(1, LANES, SSD_D_INNER // LANES + SSD_GROUPS),
                  par(1, LANES, 0), par(1, LANES, 0), par(1, gd, 0), par(1, gd, 0)],
        out_specs=pl.BlockSpec((rows, gd), lambda b, g: (b, g)),
        out_shape=jax.ShapeDtypeStruct((proj.shape[0], SSD_D_INNER), BF16),
        scratch_shapes=[pltpu.VMEM((rows, gd), F32),
                        pltpu.VMEM((rows, SSD_STATE), BF16),
                        pltpu.VMEM((rows, SSD_STATE), BF16),
                        pltpu.VMEM((rows, LANES), F32),
                        pltpu.VMEM((rows, LANES), F32),
                        pltpu.VMEM((rows, gd), F32),
                        pltpu.VMEM((rows, gd), F32),
                        pltpu.VMEM((rows, gd), F32),
                        pltpu.VMEM((rows, gd), BF16),
                        pltpu.VMEM((rows, gd), BF16),
                        pltpu.VMEM((SSD_STATE, gd), F32),
                        pltpu.VMEM((SSD_STATE, gd), F32)],
        compiler_params=_params("parallel", "parallel"),
        name="ssd",
    )(proj, proj, proj, proj, proj, conv_w, conv_w, conv_w, conv_b, conv_b, conv_b,
      dt_bias, a_log, d_skip, norm_w.reshape(1, -1))


HG_OWN, HG_EARLIER, HG_LATER = 1.0, 2.0, 3.0


def _hgrn_fill_tables(tr, tri_ref, kind_ref, place_ref):
    ch, sub = HG_CHUNK, HG_SUB
    ri = lax.broadcasted_iota(jnp.int32, (tr, tr), 0)
    ci = lax.broadcasted_iota(jnp.int32, (tr, tr), 1)
    same = (ri // ch) == (ci // ch)
    tri_ref[0] = (same & (ri >= ci)).astype(BF16)
    tri_ref[1] = (same & (ri <= ci)).astype(BF16)
    rb, cb = ri // sub, ci // sub
    kind_ref[...] = jnp.where(rb == cb, HG_OWN,
                              jnp.where(same & (cb < rb), HG_EARLIER, jnp.where(same & (cb > rb), HG_LATER, 0.0)))
    src_key = lax.broadcasted_iota(jnp.int32, (sub * LANES, LANES), 0) // LANES
    dst_key = lax.broadcasted_iota(jnp.int32, (sub * LANES, LANES), 1) % sub
    place_ref[...] = (src_key == dst_key).astype(BF16)


def _hgrn_intra(t, tr, rev, tables, lg_ref, k_ref, q_ref, v_ref, o_ref, qst_ref, upd_ref, dec_ref):
    ch, sub = HG_CHUNK, HG_SUB
    npc, nb, nblk = tr // ch, ch // sub, tr // sub
    tri_ref, kind_ref, place_ref = tables
    r0 = pl.multiple_of(t * tr, tr)
    lg = lg_ref[pl.ds(r0, tr), :]
    k = k_ref[pl.ds(r0, tr), :]
    q = q_ref[pl.ds(r0, tr), :]
    v = v_ref[pl.ds(r0, tr), :]
    tri = tri_ref[1 if rev else 0]
    lg_hi, lg_lo = _split_bf16(lg)
    cum = jnp.dot(tri, lg_hi, preferred_element_type=F32) + jnp.dot(tri, lg_lo, preferred_element_type=F32)

    last = [c * ch + (0 if rev else ch - 1) for c in range(npc)]
    qst_ref[pl.ds(r0, tr), :] = (q * jnp.exp2(cum)).astype(BF16)
    k_end = (k * jnp.exp2(_rows_from(cum, last, ch) - cum)).astype(BF16)
    for c in range(npc):
        chunk = t * npc + c
        dec_ref[pl.ds(chunk, 1), :] = jnp.exp2(cum[last[c]:last[c] + 1, :])
        upd_ref[pl.ds(pl.multiple_of(chunk * LANES, LANES), LANES), :] = lax.dot_general(
            v[c * ch:(c + 1) * ch], k_end[c * ch:(c + 1) * ch], _TN, preferred_element_type=F32)

    def own_edge(j):
        i = j % nb
        if rev:
            return None if i == nb - 1 else (j + 1) * sub
        return None if i == 0 else j * sub - 1

    qs = q * jnp.exp2(cum - _rows_from(cum, [own_edge(j) for j in range(nblk)], sub))
    zero8 = jnp.zeros((sub, LANES), F32)
    q_slabs, k_slabs = [], []
    for i in (range(nb - 1) if rev else range(1, nb)):
        q_slabs.append(jnp.concatenate([qs[j * sub:(j + 1) * sub] if j % nb == i else zero8
                                        for j in range(nblk)], axis=0).astype(BF16))
        pieces = []
        for c in range(npc):
            base = c * ch
            e = base + ((i + 1) * sub if rev else i * sub - 1)
            lo, hi = ((i + 1) * sub, ch) if rev else (0, i * sub)
            part = k[base + lo:base + hi] * jnp.exp2(cum[e:e + 1, :] - cum[base + lo:base + hi])
            pad = jnp.zeros((ch - (hi - lo), LANES), F32)
            pieces += [pad, part] if rev else [part, pad]
        k_slabs.append(jnp.concatenate(pieces, axis=0).astype(BF16))
    a_off = lax.dot_general(jnp.concatenate(q_slabs, axis=1), jnp.concatenate(k_slabs, axis=1), _NT,
                            preferred_element_type=F32)

    row8 = lax.broadcasted_iota(jnp.int32, (sub, LANES), 0)
    prods = []
    for s in range(sub):
        idx = [j * sub + s for j in range(nblk)]
        unseen = jnp.where((row8 <= s) if rev else (row8 >= s), 0.0, NEG_INF)
        w = jnp.exp2(cum - _rows_from(cum, idx, sub) + jnp.concatenate([unseen] * nblk, axis=0))
        prods.append((q * _rows_from(k, idx, sub) * w).astype(BF16))
    a_own = jnp.dot(jnp.concatenate(prods, axis=1), place_ref[...], preferred_element_type=F32)
    kind = kind_ref[...]
    a = jnp.where(kind == HG_OWN, jnp.concatenate([a_own] * (tr // LANES), axis=1),
                  jnp.where(kind == (HG_LATER if rev else HG_EARLIER), a_off, 0.0))
    o_ref[pl.ds(r0, tr), :] = jnp.dot(a.astype(BF16), v, preferred_element_type=F32)


def _hgrn_state_step(c, st, qst_ref, upd_ref, dec_ref, o_ref):
    ch = HG_CHUNK
    r0 = pl.multiple_of(c * ch, ch)
    o_ref[pl.ds(r0, ch), :] += lax.dot_general(qst_ref[pl.ds(r0, ch), :], st.astype(BF16), _NT,
                                               preferred_element_type=F32)
    return st * dec_ref[pl.ds(c, 1), :] + upd_ref[pl.ds(pl.multiple_of(c * LANES, LANES), LANES), :]


def _hgrn_kernel(ff_ref, fb_ref, v_ref, q_ref, gate_ref, lb_ref, nw_ref, out_ref,
                 lgf_ref, lgb_ref, kf_ref, kb_ref, qa_ref, o_ref, ob_ref, qstf_ref, qstb_ref,
                 updf_ref, updb_ref, decf_ref, decb_ref, tri_ref, kind_ref, place_ref, *, n_ctx, n_lat, tile):
    tables = (tri_ref, kind_ref, place_ref)
    _hgrn_fill_tables(tile, *tables)
    for d, (f_ref, lg_ref, k_ref) in enumerate(((ff_ref, lgf_ref, kf_ref), (fb_ref, lgb_ref, kb_ref))):
        lb = lb_ref[d:d + 1, :]
        g = lb + (1.0 - lb) * jax.nn.sigmoid(f_ref[...].astype(F32))
        lg_ref[...] = jnp.log(g) * LOG2_E
        k_ref[...] = 1.0 - g
    qa_ref[...] = _silu(q_ref[...].astype(F32))

    fwd = (qstf_ref, updf_ref, decf_ref)
    bwd = (qstb_ref, updb_ref, decb_ref)

    def intra_step(t, carry):
        _hgrn_intra(t, tile, False, tables, lgf_ref, kf_ref, qa_ref, v_ref, o_ref, *fwd)
        _hgrn_intra(t, tile, True, tables, lgb_ref, kb_ref, qa_ref, v_ref, ob_ref, *bwd)
        return carry

    lax.fori_loop(0, (n_ctx + n_lat) // tile, intra_step, 0)

    n_c = n_ctx // HG_CHUNK
    n_l = n_lat // HG_CHUNK

    def ctx_step(s, carry):
        return (_hgrn_state_step(s, carry[0], *fwd, o_ref),
                _hgrn_state_step(n_c - 1 - s, carry[1], *bwd, ob_ref))

    def lat_step(s, carry):
        return (_hgrn_state_step(n_c + s, carry[0], *fwd, o_ref),
                _hgrn_state_step(n_c + n_l - 1 - s, carry[1], *bwd, ob_ref))

    zero = jnp.zeros((LANES, LANES), F32)
    carry = lax.fori_loop(0, n_c, ctx_step, (zero, zero), unroll=SCAN_UNROLL)
    lax.fori_loop(0, n_l, lat_step, carry, unroll=SCAN_UNROLL)

    o = o_ref[...] + ob_ref[...]
    out_ref[...] = (_rms_rows(o) * nw_ref[...] * _silu(gate_ref[...].astype(F32))).astype(out_ref.dtype)


def _hgrn(proj, lower, norm_w, n_ctx, n_lat, n_batch):
    rows = n_ctx + n_lat
    sec = lambda off: pl.BlockSpec((rows, LANES), lambda b, h: (b, off * HG_HEADS + h))
    seq = lambda dt: pltpu.VMEM((rows, LANES), dt)
    n_chunks = rows // HG_CHUNK
    upd = pltpu.VMEM((n_chunks * LANES, LANES), F32)
    dec = pltpu.VMEM((-(-n_chunks // 8) * 8, LANES), F32)
    tile = _scan_tile(rows, HG_CHUNK)
    tables = [pltpu.VMEM((2, tile, tile), BF16), pltpu.VMEM((tile, tile), F32),
              pltpu.VMEM((HG_SUB * LANES, LANES), BF16)]
    return pl.pallas_call(
        functools.partial(_hgrn_kernel, n_ctx=n_ctx, n_lat=n_lat, tile=tile),
        grid=(n_batch, HG_HEADS),
        in_specs=[sec(0), sec(1), sec(2), sec(3), sec(4),
                  pl.BlockSpec((2, LANES), lambda b, h: (0, h)),
                  pl.BlockSpec((1, LANES), lambda b, h: (0, h))],
        out_specs=pl.BlockSpec((rows, LANES), lambda b, h: (b, h)),
        out_shape=jax.ShapeDtypeStruct((proj.shape[0], HG_VAL), BF16),
        scratch_shapes=[seq(F32), seq(F32), seq(F32), seq(F32), seq(F32), seq(F32), seq(F32),
                        seq(BF16), seq(BF16), upd, upd, dec, dec] + tables,
        compiler_params=_params("parallel", "parallel"),
        name="hgrn2",
    )(proj, proj, proj, proj, proj, lower, norm_w.reshape(1, -1))


def _hgrn_lower_bounds(lb_raw):
    p = jax.nn.softmax(lb_raw.astype(F32), axis=1)
    return jnp.cumsum(p, axis=1) - p[:, :1]


def kernel(x, c, ctx, c_ctx, ada_w, ada_b, norm_g, ffn_w_in, ffn_w_out, attn_w_in, attn_w_out, attn_sink,
           ssd_w_in, ssd_conv_w, ssd_conv_b, ssd_dt_bias, ssd_a_log, ssd_d, ssd_norm_w, ssd_w_out,
           hgrn_w_in, hgrn_lb, hgrn_norm_w, hgrn_w_out):
    n_batch, n_lat, d = x.shape
    n_ctx = ctx.shape[1]
    rows = n_ctx + n_lat
    tm = _token_tile(n_ctx, n_ctx + n_lat)
    tm_lat = _token_tile(n_ctx, n_lat)

    r = jnp.concatenate([ctx, x], axis=1).reshape(n_batch * rows, d)
    n_cond = -(-(n_batch + 1) // 8) * 8
    c_rows = jnp.concatenate([c, c_ctx[None, :], jnp.zeros((n_cond - n_batch - 1, d), F32)], axis=0)
    mods = _ada_mods(c_rows, ada_w, ada_b)
    tables = _rope_tables(n_ctx, n_lat)
    lower = _hgrn_lower_bounds(hgrn_lb)

    for i in range(DEPTH):
        kind, j = i % N_MIXERS, i // N_MIXERS
        mod = mods[i]
        if kind == 0:
            proj = _modproj(r, mod, norm_g[i, 0], attn_w_in[j].astype(BF16), 512, n_ctx, tm, n_batch)
            y = _attention(proj, attn_sink[j], tables, n_ctx, n_lat, n_batch)
            w_out = attn_w_out[j]
        elif kind == 1:
            w_in, conv_w, conv_b, dt_bias, a_log, d_skip = _ssd_layout(
                ssd_w_in[j], ssd_conv_w[j], ssd_conv_b[j], ssd_dt_bias[j], ssd_a_log[j], ssd_d[j])
            proj = _modproj(r, mod, norm_g[i, 0], w_in, 1024, n_ctx, tm, n_batch)
            y = _ssd(proj, conv_w, conv_b, dt_bias, a_log, d_skip, ssd_norm_w[j], n_ctx, n_lat, n_batch)
            w_out = ssd_w_out[j]
        else:
            proj = _modproj(r, mod, norm_g[i, 0], hgrn_w_in[j].astype(BF16), 1024, n_ctx, tm, n_batch)
            y = _hgrn(proj, lower[:, i], hgrn_norm_w[j], n_ctx, n_lat, n_batch)
            w_out = hgrn_w_out[j]
        ffn_w = (ffn_w_in[i].astype(BF16), ffn_w_out[i].astype(BF16))
        if i < DEPTH - 1:
            r = _outproj(y, w_out.astype(BF16), r, mod, norm_g[i, 1], n_ctx, tm, n_batch)
            r = _ffn(r, mod, norm_g[i, 2], norm_g[i, 3], *ffn_w, n_ctx, tm, n_batch)
        else:
            r = _outproj(y, w_out.astype(BF16), r, mod, norm_g[i, 1], n_ctx, tm_lat, n_batch, lat_rows=n_lat)
            r = _ffn(r, mod, norm_g[i, 2], norm_g[i, 3], *ffn_w, n_ctx, tm_lat, n_batch, has_ctx=False)
    return r.reshape(n_batch, n_lat, d)
```

```python
import functools
import math

import jax
import jax.numpy as jnp
from jax import lax
from jax.experimental import pallas as pl
from jax.experimental.pallas import tpu as pltpu

F32 = jnp.float32
BF16 = jnp.bfloat16

D_MODEL = 2048
DEPTH = 4
N_MIXERS = 3
GRID_W = 64
RMS_EPS = 1e-6
N_MOD = 6

ATT_HEADS = 32
ATT_KV_HEADS = 4
ATT_GQA = ATT_HEADS // ATT_KV_HEADS
ATT_HEAD_DIM = 64
ATT_WINDOW = 128
ATT_BLOCK = 128
ATT_Q_DIM = ATT_HEADS * ATT_HEAD_DIM
ATT_KV_DIM = ATT_KV_HEADS * ATT_HEAD_DIM
ROPE_THETA = 10000.0
ROPE_PAIRS = ATT_HEAD_DIM // 4
NEG_INF = -1e30
LOG2_E = 1.0 / math.log(2.0)

SSD_D_INNER = 2 * D_MODEL
SSD_HEAD_DIM = 64
SSD_HEADS = SSD_D_INNER // SSD_HEAD_DIM
SSD_GROUPS = 8
SSD_GROUP_HEADS = SSD_HEADS // SSD_GROUPS
SSD_GROUP_DIM = SSD_D_INNER // SSD_GROUPS
SSD_STATE = 128
SSD_GN = SSD_GROUPS * SSD_STATE
SSD_CONV_DIM = SSD_D_INNER + 2 * SSD_GN
SSD_CHUNK = 64

HG_EXPAND = 128
HG_HEADS = D_MODEL // HG_EXPAND
HG_KEY = HG_HEADS * HG_EXPAND
HG_VAL = D_MODEL
HG_CHUNK = 64
HG_SUB = 8

FFN_DIM = -(-8 * D_MODEL // (3 * 256)) * 256

LANES = 128
MXU_TILE = 256
VMEM_LIMIT_BYTES = 56 * 1024 * 1024
MAX_TOKEN_TILE = 768
MAX_PROJ_TILE = 2816
SCAN_UNROLL = 2
READ_UNROLL = 6


def _params(*sem):
    return pltpu.CompilerParams(dimension_semantics=sem, vmem_limit_bytes=VMEM_LIMIT_BYTES)


def _token_tile(n_ctx, rows):
    best = n_ctx
    for mult in range(1, rows // n_ctx + 1):
        t = mult * n_ctx
        if rows % t == 0 and t <= MAX_TOKEN_TILE:
            best = t
    assert rows % best == 0
    return best


def _silu(t):
    h = 0.5 * t
    return h + h * jnp.tanh(h)


def _rms_rows(t):
    return t * lax.rsqrt(jnp.mean(t * t, axis=-1, keepdims=True) + RMS_EPS)


def _rows_from(t, idx, height):
    width = t.shape[1]
    return jnp.concatenate([jnp.zeros((height, width), t.dtype) if r is None
                            else jnp.broadcast_to(t[r:r + 1, :], (height, width)) for r in idx], axis=0)


_NT = (((1,), (1,)), ((), ()))
_TN = (((0,), (0,)), ((), ()))


def _scan_tile(rows, chunk):
    return next(t for t in (4 * chunk, 2 * chunk) if rows % t == 0 and t % LANES == 0)


def _ada_kernel(c_ref, w_ref, b_ref, o_ref):
    s = _silu(c_ref[...]).astype(BF16)
    o_ref[...] = jnp.dot(s, w_ref[...].astype(BF16), preferred_element_type=F32) + b_ref[...]


def _ada_mods(c_rows, ada_w, ada_b):
    r = c_rows.shape[0]
    n = ada_w.shape[-1]
    tn = 1024
    out = pl.pallas_call(
        _ada_kernel,
        grid=(DEPTH, n // tn),
        in_specs=[pl.BlockSpec((r, D_MODEL), lambda l, j: (0, 0)),
                  pl.BlockSpec((None, D_MODEL, tn), lambda l, j: (l, 0, j)),
                  pl.BlockSpec((None, 1, tn), lambda l, j: (l, 0, j))],
        out_specs=pl.BlockSpec((None, r, tn), lambda l, j: (l, 0, j)),
        out_shape=jax.ShapeDtypeStruct((DEPTH, r, n), F32),
        compiler_params=_params("parallel", "parallel"),
        name="ada_mods",
    )(c_rows, ada_w, ada_b.reshape(DEPTH, 1, n))
    return out.reshape(DEPTH, r, N_MOD, D_MODEL)


def _first_tile(tiles_per_batch, has_ctx):
    return (pl.program_id(0) % tiles_per_batch == 0) if has_ctx else None


def _mod_row(mod_ref, modc_ref, row, first, r0):
    m = mod_ref[row:row + 1, :]
    if r0 == 0 and first is not None:
        m = jnp.where(first, modc_ref[row:row + 1, :], m)
    return m


def _modulate_into(x_ref, u_ref, mod_ref, modc_ref, g_ref, shift_row, first, n_ctx):
    g = g_ref[...]
    for r0 in range(0, x_ref.shape[0], n_ctx):
        shift = _mod_row(mod_ref, modc_ref, shift_row, first, r0)
        scale = _mod_row(mod_ref, modc_ref, shift_row + 1, first, r0)
        x = x_ref[r0:r0 + n_ctx, :]
        u_ref[r0:r0 + n_ctx, :] = (_rms_rows(x) * (g * (1.0 + scale)) + shift).astype(u_ref.dtype)


def _gated_residual_into(o_ref, f_src, x_ref, mod_ref, modc_ref, g_ref, gate_row, first, n_ctx):
    g = g_ref[...]
    for r0 in range(0, x_ref.shape[0], n_ctx):
        gate = _mod_row(mod_ref, modc_ref, gate_row, first, r0)
        f = f_src[r0:r0 + n_ctx, :]
        o_ref[r0:r0 + n_ctx, :] = x_ref[r0:r0 + n_ctx, :] + gate * (_rms_rows(f) * g)


def _mod_specs(tiles_per_batch, n_batch):
    return [pl.BlockSpec((None, N_MOD, D_MODEL), lambda i, j: (i // tiles_per_batch, 0, 0)),
            pl.BlockSpec((None, N_MOD, D_MODEL), lambda i, j: (n_batch, 0, 0))]


def _modproj_kernel(x_ref, mod_ref, modc_ref, g_ref, w_ref, o_ref, u_ref, *, n_ctx, tiles_per_batch,
                    has_ctx=True):
    first = _first_tile(tiles_per_batch, has_ctx)

    @pl.when(pl.program_id(1) == 0)
    def _():
        _modulate_into(x_ref, u_ref, mod_ref, modc_ref, g_ref, 0, first, n_ctx)

    tn = o_ref.shape[1]
    cut = -(-tn // (2 * MXU_TILE)) * MXU_TILE
    for lo, hi in ((0, cut), (cut, tn)):
        if hi > lo:
            o_ref[:, lo:hi] = jnp.dot(u_ref[...], w_ref[:, lo:hi], preferred_element_type=F32).astype(o_ref.dtype)


def _modproj(x, mod, g, w, n_ctx, tm, n_batch):
    t, d = x.shape
    n = w.shape[1]
    tn = max(c for c in range(LANES, MAX_PROJ_TILE + 1, LANES) if n % c == 0)
    tpb = t // n_batch // tm
    return pl.pallas_call(
        functools.partial(_modproj_kernel, n_ctx=n_ctx, tiles_per_batch=tpb),
        grid=(t // tm, n // tn),
        in_specs=[pl.BlockSpec((tm, d), lambda i, j: (i, 0))] + _mod_specs(tpb, n_batch)
        + [pl.BlockSpec((1, d), lambda i, j: (0, 0)),
           pl.BlockSpec((d, tn), lambda i, j: (0, j))],
        out_specs=pl.BlockSpec((tm, tn), lambda i, j: (i, j)),
        out_shape=jax.ShapeDtypeStruct((t, n), BF16),
        scratch_shapes=[pltpu.VMEM((tm, d), BF16)],
        compiler_params=_params("parallel", "arbitrary"),
        name="modproj",
    )(x, mod, mod, g.reshape(1, d), w)


def _outproj_kernel(y_ref, w_ref, x_ref, mod_ref, modc_ref, g_ref, o_ref, *, n_ctx, tiles_per_batch, nk,
                    has_ctx):
    k = pl.program_id(1)
    first = _first_tile(tiles_per_batch, has_ctx)
    part = jnp.dot(y_ref[...], w_ref[...], preferred_element_type=F32)
    if nk == 1:
        _gated_residual_into(o_ref, part, x_ref, mod_ref, modc_ref, g_ref, 2, first, n_ctx)
        return

    @pl.when(k == 0)
    def _():
        o_ref[...] = part

    @pl.when(k > 0)
    def _():
        o_ref[...] += part

    @pl.when(k == nk - 1)
    def _():
        _gated_residual_into(o_ref, o_ref, x_ref, mod_ref, modc_ref, g_ref, 2, first, n_ctx)


def _outproj(y, w, x, mod, g, n_ctx, tm, n_batch, lat_rows=None):
    t, d = x.shape
    kdim = y.shape[1]
    tk = min(kdim, 2048)
    nk = kdim // tk
    rows = t // n_batch
    if lat_rows is None:
        tpb, t_out = rows // tm, t
        y_spec = pl.BlockSpec((tm, tk), lambda i, k: (i, k))
        x_spec = pl.BlockSpec((tm, d), lambda i, k: (i, 0))
    else:
        tpb, t_out = lat_rows // tm, n_batch * lat_rows
        assert rows % n_ctx == 0 and lat_rows % n_ctx == 0 and tm % n_ctx == 0
        first_row = lambda i: pl.multiple_of(
            ((i // tpb) * (rows // n_ctx) + (rows - lat_rows) // n_ctx + (i % tpb) * (tm // n_ctx)) * n_ctx, n_ctx)
        y_spec = pl.BlockSpec((pl.Element(tm), pl.Element(tk)), lambda i, k: (first_row(i), k * tk))
        x_spec = pl.BlockSpec((pl.Element(tm), pl.Element(d)), lambda i, k: (first_row(i), 0))
    return pl.pallas_call(
        functools.partial(_outproj_kernel, n_ctx=n_ctx, tiles_per_batch=tpb, nk=nk, has_ctx=lat_rows is None),
        grid=(t_out // tm, nk),
        in_specs=[y_spec, pl.BlockSpec((tk, d), lambda i, k: (k, 0)), x_spec] + _mod_specs(tpb, n_batch)
        + [pl.BlockSpec((1, d), lambda i, k: (0, 0))],
        out_specs=pl.BlockSpec((tm, d), lambda i, k: (i, 0)),
        out_shape=jax.ShapeDtypeStruct((t_out, d), F32),
        compiler_params=_params("parallel", "arbitrary"),
        name="outproj",
    )(y, w, x, mod, mod, g.reshape(1, d))


def _ffn_kernel(x_ref, mod_ref, modc_ref, g2_ref, g3_ref, wg_ref, wu_ref, wo_ref, o_ref, u_ref,
                *, n_ctx, tiles_per_batch, nf, has_ctx):
    j = pl.program_id(1)
    first = _first_tile(tiles_per_batch, has_ctx)

    def hidden_tile(assign):
        u = u_ref[...]
        gate = jnp.dot(u, wg_ref[...], preferred_element_type=F32)
        up = jnp.dot(u, wu_ref[...], preferred_element_type=F32)
        h = (_silu(gate) * up).astype(BF16)
        part = jnp.dot(h, wo_ref[...], preferred_element_type=F32)
        if assign:
            o_ref[...] = part
        else:
            o_ref[...] += part

    @pl.when(j == 0)
    def _():
        _modulate_into(x_ref, u_ref, mod_ref, modc_ref, g2_ref, 3, first, n_ctx)
        hidden_tile(True)

    @pl.when(j > 0)
    def _():
        hidden_tile(False)

    @pl.when(j == nf - 1)
    def _():
        _gated_residual_into(o_ref, o_ref, x_ref, mod_ref, modc_ref, g3_ref, 5, first, n_ctx)


def _ffn(x, mod, g2, g3, w_in, w_out, n_ctx, tm, n_batch, has_ctx=True):
    t, d = x.shape
    f = w_out.shape[0]
    tf = 512
    nf = f // tf
    tpb = t // n_batch // tm
    return pl.pallas_call(
        functools.partial(_ffn_kernel, n_ctx=n_ctx, tiles_per_batch=tpb, nf=nf, has_ctx=has_ctx),
        grid=(t // tm, nf),
        in_specs=[pl.BlockSpec((tm, d), lambda i, j: (i, 0))] + _mod_specs(tpb, n_batch)
        + [pl.BlockSpec((1, d), lambda i, j: (0, 0)),
           pl.BlockSpec((1, d), lambda i, j: (0, 0)),
           pl.BlockSpec((d, tf), lambda i, j: (0, j)),
           pl.BlockSpec((d, tf), lambda i, j: (0, nf + j)),
           pl.BlockSpec((tf, d), lambda i, j: (j, 0))],
        out_specs=pl.BlockSpec((tm, d), lambda i, j: (i, 0)),
        out_shape=jax.ShapeDtypeStruct((t, d), F32),
        scratch_shapes=[pltpu.VMEM((tm, d), BF16)],
        compiler_params=_params("parallel", "arbitrary"),
        name="ffn",
    )(x, mod, mod, g2.reshape(1, d), g3.reshape(1, d), w_in, w_in, w_out)


def _rope_tables(n_ctx, n_lat):
    pos = jnp.arange(n_lat)
    row = (pos // GRID_W).astype(F32)
    col = (pos % GRID_W).astype(F32)
    inv_freq = ROPE_THETA ** (-jnp.arange(ROPE_PAIRS, dtype=F32) / ROPE_PAIRS)
    ang_row = row[:, None] * inv_freq
    ang_col = col[:, None] * inv_freq
    zero = jnp.zeros_like(ang_row)
    cos = jnp.concatenate([jnp.cos(ang_row)] * 2 + [jnp.cos(ang_col)] * 2, axis=1)
    s_lo = jnp.concatenate([-jnp.sin(ang_row), zero, -jnp.sin(ang_col), zero], axis=1)
    s_hi = jnp.concatenate([zero, jnp.sin(ang_row), zero, jnp.sin(ang_col)], axis=1)

    def full(tab, ctx_val):
        tab = jnp.concatenate([jnp.full((n_ctx, ATT_HEAD_DIM), ctx_val, F32), tab], axis=0)
        return jnp.concatenate([tab, tab], axis=1)

    return full(cos, 1.0), full(s_lo, 0.0), full(s_hi, 0.0)


def _rope(x, c, s_lo, s_hi):
    return x * c + pltpu.roll(x, LANES - ROPE_PAIRS, 1) * s_lo + pltpu.roll(x, ROPE_PAIRS, 1) * s_hi


def _attn_kernel(sink_ref, pq_ref, pk_ref, pv_ref, cq_ref, slq_ref, shq_ref, ck_ref, slk_ref, shk_ref,
                 o_ref, kx_ref, vt_ref, qt_ref, ot_ref, *, n_ctx, n_lat):
    t = pl.program_id(1)
    blk = ATT_BLOCK
    n_blocks = (n_ctx + n_lat) // blk
    ctx_blocks = n_ctx // blk
    hd = ATT_HEAD_DIM

    @pl.when(t == 0)
    def _():
        for i in range(n_blocks):
            rs = slice(i * blk, (i + 1) * blk)
            for c0 in range(0, ATT_KV_DIM, LANES):
                k = pk_ref[rs, c0:c0 + LANES].astype(F32)
                kx_ref[i, :, c0:c0 + LANES] = _rope(k, ck_ref[rs, :], slk_ref[rs, :], shk_ref[rs, :]).astype(BF16)
                vt_ref[i, c0:c0 + LANES, :] = pv_ref[rs, c0:c0 + LANES].astype(F32).T.astype(BF16)
        kx_ref[n_blocks] = jnp.zeros((blk, ATT_KV_DIM), BF16)
        vt_ref[n_blocks] = jnp.zeros((ATT_KV_DIM, blk), BF16)

    scale = ATT_HEAD_DIM ** -0.5 * LOG2_E
    for c0 in range(0, ATT_Q_DIM, LANES):
        q = pq_ref[:, 2 * ATT_KV_DIM + c0:2 * ATT_KV_DIM + c0 + LANES].astype(F32)
        qt_ref[c0:c0 + LANES, :] = (_rope(q, cq_ref[...], slq_ref[...], shq_ref[...]) * scale).T.astype(BF16)

    n = t - ctx_blocks
    key = lax.broadcasted_iota(jnp.int32, (blk, 2 * LANES), 0)
    qry = lax.broadcasted_iota(jnp.int32, (blk, 2 * LANES), 1) % blk
    blocks = [(i, None) for i in range(ctx_blocks)] + [
        (jnp.maximum(ctx_blocks + n - 1, 0), (key >= qry) & (n >= 1)),
        (jnp.maximum(ctx_blocks + n, 0), (key >= 0) & (n >= 0)),
        (ctx_blocks + n + 1, (key <= qry) & (n >= 0) & (n + 1 < n_lat // blk))]
    upper = lax.broadcasted_iota(jnp.int32, (1, 2 * LANES), 1) >= LANES
    zero = jnp.zeros((hd, LANES), BF16)

    for pair in range(ATT_KV_HEADS // 2):
        ps = slice(pair * LANES, (pair + 1) * LANES)
        for g in range(ATT_GQA):
            h0 = (2 * pair) * ATT_GQA + g
            h1 = (2 * pair + 1) * ATT_GQA + g
            w = jnp.concatenate([jnp.concatenate([qt_ref[h0 * hd:(h0 + 1) * hd, :], zero], axis=0),
                                 jnp.concatenate([zero, qt_ref[h1 * hd:(h1 + 1) * hd, :]], axis=0)], axis=1)
            sink = jnp.where(upper, sink_ref[h1], sink_ref[h0]) * LOG2_E
            scores = []
            m = sink
            for i, visible in blocks:
                s = jnp.dot(kx_ref[i, :, ps], w, preferred_element_type=F32)
                if visible is not None:
                    s = jnp.where(visible, s, NEG_INF)
                scores.append(s)
                m = jnp.maximum(m, jnp.max(s, axis=0, keepdims=True))
            den = jnp.exp2(sink - m)
            acc = jnp.zeros((LANES, 2 * LANES), F32)
            for (i, _), s in zip(blocks, scores):
                p = jnp.exp2(s - m)
                den = den + jnp.sum(p, axis=0, keepdims=True)
                acc = acc + jnp.dot(vt_ref[i, ps, :], p.astype(BF16), preferred_element_type=F32)
            ot_ref[h0 * hd:(h0 + 1) * hd, :] = acc[0:hd, 0:LANES] / den[:, 0:LANES]
            ot_ref[h1 * hd:(h1 + 1) * hd, :] = acc[hd:2 * hd, LANES:2 * LANES] / den[:, LANES:2 * LANES]

    for c0 in range(0, ATT_Q_DIM, LANES):
        o_ref[:, c0:c0 + LANES] = ot_ref[c0:c0 + LANES, :].T.astype(o_ref.dtype)


def _attention(proj, sink, tables, n_ctx, n_lat, n_batch):
    rows = n_ctx + n_lat
    nq = rows // ATT_BLOCK
    cos, s_lo, s_hi = tables
    qtab = pl.BlockSpec((ATT_BLOCK, LANES), lambda b, t: (t, 0))
    ktab = pl.BlockSpec((rows, LANES), lambda b, t: (0, 0))
    return pl.pallas_call(
        functools.partial(_attn_kernel, n_ctx=n_ctx, n_lat=n_lat),
        grid=(n_batch, nq),
        in_specs=[pl.BlockSpec(memory_space=pltpu.SMEM),
                  pl.BlockSpec((ATT_BLOCK, proj.shape[1]), lambda b, t: (b * nq + t, 0)),
                  pl.BlockSpec((rows, ATT_KV_DIM), lambda b, t: (b, 0)),
                  pl.BlockSpec((rows, ATT_KV_DIM), lambda b, t: (b, 1)),
                  qtab, qtab, qtab, ktab, ktab, ktab],
        out_specs=pl.BlockSpec((ATT_BLOCK, ATT_Q_DIM), lambda b, t: (b * nq + t, 0)),
        out_shape=jax.ShapeDtypeStruct((proj.shape[0], ATT_Q_DIM), BF16),
        scratch_shapes=[pltpu.VMEM((nq + 1, ATT_BLOCK, ATT_KV_DIM), BF16),
                        pltpu.VMEM((nq + 1, ATT_KV_DIM, ATT_BLOCK), BF16),
                        pltpu.VMEM((ATT_Q_DIM, ATT_BLOCK), BF16),
                        pltpu.VMEM((ATT_Q_DIM, ATT_BLOCK), F32)],
        compiler_params=_params("parallel", "arbitrary"),
        name="attention",
    )(sink.astype(F32), proj, proj, proj, cos, s_lo, s_hi, cos, s_lo, s_hi)


def _softplus(t):
    return jnp.maximum(t, 0.0) + jnp.log(1.0 + jnp.exp(-jnp.abs(t)))


def _split_bf16(t):
    hi = t.astype(BF16)
    return hi, (t - hi.astype(F32)).astype(BF16)


def _conv_silu_into(src_ref, w_ref, b_ref, dst_ref, n_ctx):
    rows, width = src_ref.shape
    t = n_ctx
    ri = lax.broadcasted_iota(jnp.int32, (t, t), 0)
    ci = lax.broadcasted_iota(jnp.int32, (t, t), 1)
    down = (ri == ci + 1).astype(BF16)
    up = (ri + 1 == ci).astype(BF16)
    row8 = lax.broadcasted_iota(jnp.int32, (8, width), 0)
    w = w_ref[...]
    b = b_ref[...]
    for r0 in range(0, rows, t):
        x = src_ref[r0:r0 + t, :]
        prv = jnp.dot(down, x, preferred_element_type=F32)
        nxt = jnp.dot(up, x, preferred_element_type=F32)
        if r0 not in (0, n_ctx):
            edge = jnp.broadcast_to(src_ref[r0 - 1:r0, :].astype(F32), (8, width))
            prv = jnp.concatenate([jnp.where(row8 == 0, edge, prv[0:8]), prv[8:]], axis=0)
        if r0 + t not in (n_ctx, rows):
            edge = jnp.broadcast_to(src_ref[r0 + t:r0 + t + 1, :].astype(F32), (8, width))
            nxt = jnp.concatenate([nxt[:t - 8], jnp.where(row8 == 7, edge, nxt[t - 8:])], axis=0)
        out = b + prv * w[0:1, :] + x.astype(F32) * w[1:2, :] + nxt * w[2:3, :]
        dst_ref[r0:r0 + t, :] = _silu(out).astype(dst_ref.dtype)


def _ssd_intra(t, tr, rev, add, xa_ref, ba_ref, ca_ref, dtv_ref, da_ref, y_ref, cumx_ref, xdec_ref):
    q = SSD_CHUNK
    p = SSD_HEAD_DIM
    assert 2 * q == LANES and 2 * p == LANES
    npc = tr // q
    r0 = pl.multiple_of(t * tr, tr)
    xa = xa_ref[pl.ds(r0, tr), :]
    bc = ba_ref[pl.ds(r0, tr), :]
    cc = ca_ref[pl.ds(r0, tr), :]
    dtc = dtv_ref[pl.ds(r0, tr), :]
    dac = da_ref[pl.ds(r0, tr), :]
    gd = SSD_GROUP_DIM
    lane0 = SSD_GROUP_HEADS if rev else 0
    ri = lax.broadcasted_iota(jnp.int32, (tr, tr), 0)
    ci = lax.broadcasted_iota(jnp.int32, (tr, tr), 1)
    tri = (((ri // q) == (ci // q)) & ((ri <= ci) if rev else (ri >= ci))).astype(BF16)
    da_hi, da_lo = _split_bf16(dac)
    cum_hi, cum_lo = _split_bf16(jnp.dot(tri, da_hi, preferred_element_type=F32)
                                 + jnp.dot(tri, da_lo, preferred_element_type=F32))
    cum = cum_hi.astype(F32) + cum_lo.astype(F32)
    src = lax.broadcasted_iota(jnp.int32, (LANES, gd), 0)
    dst = lax.broadcasted_iota(jnp.int32, (LANES, gd), 1) // p
    expand = (src == lane0 + dst).astype(BF16)
    dt_hi, dt_lo = _split_bf16(dtc)
    wide = (jnp.dot(jnp.concatenate([cum_hi, dt_hi], axis=0), expand, preferred_element_type=F32)
            + jnp.dot(jnp.concatenate([cum_lo, dt_lo], axis=0), expand, preferred_element_type=F32))
    cum_x = wide[0:tr]
    dt_x = wide[tr:2 * tr]
    tot_x = _rows_from(cum_x, [c * q + (0 if rev else q - 1) for c in range(npc)], q)
    xdt = xa * dt_x
    cumx_ref[pl.ds(r0, tr), :] = cum_x
    xdec_ref[pl.ds(r0, tr), :] = (xdt * jnp.exp2(tot_x - cum_x)).astype(BF16)

    row = lax.broadcasted_iota(jnp.int32, (q, LANES), 0)
    lane = lax.broadcasted_iota(jnp.int32, (q, LANES), 1)
    upper = lane >= q
    key = jnp.where(upper, lane - q, lane)
    seen = (row <= key) if rev else (row >= key)
    for c in range(npc):
        cs = slice(c * q, (c + 1) * q)
        cum_rows = jnp.concatenate([cum[cs], pltpu.roll(cum[cs], LANES - 1, 1)], axis=0).T
        cb2 = lax.dot_general(cc[cs], jnp.concatenate([bc[cs], bc[cs]], axis=0), _NT, preferred_element_type=F32)
        ys = []
        for pr in range(SSD_GROUP_HEADS // 2):
            sl = slice(pr * LANES, (pr + 1) * LANES)
            le = lane0 + 2 * pr
            decay = jnp.exp2(jnp.where(seen, cum_x[cs, sl] - cum_rows[le:le + 1, :], NEG_INF))
            xp = xdt[cs, sl]
            x2 = jnp.concatenate([jnp.where(upper, 0.0, xp), jnp.where(upper, xp, 0.0)], axis=0).astype(BF16)
            ys.append(jnp.dot((cb2 * decay).astype(BF16), x2, preferred_element_type=F32))
        y = jnp.concatenate(ys, axis=1)
        if add:
            y_ref[pl.ds(r0 + c * q, q), :] += y
        else:
            y_ref[pl.ds(r0 + c * q, q), :] = y


def _ssd_state_step(c, rev, ba_ref, ca_ref, y_ref, cumx_ref, xdec_ref, s_ref):
    q = SSD_CHUNK
    r0 = pl.multiple_of(c * q, q)
    cum_x = cumx_ref[pl.ds(r0, q), :]
    tot_x = cum_x[0:1, :] if rev else cum_x[q - 1:q, :]
    state = s_ref[...]
    y_ref[pl.ds(r0, q), :] += (jnp.dot(ca_ref[pl.ds(r0, q), :], state.astype(BF16), preferred_element_type=F32)
                               * jnp.exp2(cum_x))
    upd = lax.dot_general(ba_ref[pl.ds(r0, q), :], xdec_ref[pl.ds(r0, q), :], _TN, preferred_element_type=F32)
    s_ref[...] = state * jnp.exp2(tot_x) + upd


def _ssd_kernel(x_ref, z_ref, b_ref, c_ref, dt_ref, cwx_ref, cwb_ref, cwc_ref, cbx_ref, cbb_ref, cbc_ref,
                dtb_ref, alog_ref, dskip_ref, nw_ref, o_ref,
                xa_ref, ba_ref, ca_ref, dtv_ref, da_ref, y_ref, cumxf_ref, cumxb_ref, xdecf_ref, xdecb_ref,
                s_ref, sb_ref, *, n_ctx, n_lat, tile):
    _conv_silu_into(x_ref, cwx_ref, cbx_ref, xa_ref, n_ctx)
    _conv_silu_into(b_ref, cwb_ref, cbb_ref, ba_ref, n_ctx)
    _conv_silu_into(c_ref, cwc_ref, cbc_ref, ca_ref, n_ctx)
    dtv = _softplus(dt_ref[...].astype(F32) + dtb_ref[...])
    dtv_ref[...] = dtv
    da_ref[...] = dtv * (-jnp.exp(alog_ref[...]) * LOG2_E)

    seq = (xa_ref, ba_ref, ca_ref, dtv_ref, da_ref, y_ref)
    fwd = (cumxf_ref, xdecf_ref)
    bwd = (cumxb_ref, xdecb_ref)

    def intra_step(t, carry):
        _ssd_intra(t, tile, False, False, *seq, *fwd)
        _ssd_intra(t, tile, True, True, *seq, *bwd)
        return carry

    lax.fori_loop(0, (n_ctx + n_lat) // tile, intra_step, 0)

    n_c = n_ctx // SSD_CHUNK
    n_l = n_lat // SSD_CHUNK

    def ctx_step(s, carry):
        _ssd_state_step(s, False, ba_ref, ca_ref, y_ref, *fwd, s_ref)
        _ssd_state_step(n_c - 1 - s, True, ba_ref, ca_ref, y_ref, *bwd, sb_ref)
        return carry

    def lat_step(s, carry):
        _ssd_state_step(n_c + s, False, ba_ref, ca_ref, y_ref, *fwd, s_ref)
        _ssd_state_step(n_c + n_l - 1 - s, True, ba_ref, ca_ref, y_ref, *bwd, sb_ref)
        return carry

    s_ref[...] = jnp.zeros(s_ref.shape, F32)
    sb_ref[...] = jnp.zeros(sb_ref.shape, F32)
    lax.fori_loop(0, n_c, ctx_step, 0, unroll=SCAN_UNROLL)
    lax.fori_loop(0, n_l, lat_step, 0, unroll=SCAN_UNROLL)

    for r0 in range(0, n_ctx + n_lat, n_ctx):
        y = y_ref[r0:r0 + n_ctx, :] + xa_ref[r0:r0 + n_ctx, :] * dskip_ref[...]
        t = y * _silu(z_ref[r0:r0 + n_ctx, :].astype(F32))
        o_ref[r0:r0 + n_ctx, :] = (_rms_rows(t) * nw_ref[...]).astype(o_ref.dtype)


def _ssd_layout(w_in, conv_w, conv_b, dt_bias, a_log, d_skip):
    n_state = SSD_CONV_DIM + 2 * SSD_HEADS
    gh = SSD_GROUP_HEADS

    def dt_blocks(t):
        lead = t.shape[:-1]
        t = t.reshape(lead + (2, SSD_GROUPS, gh))
        t = jnp.moveaxis(t, -3, -2).reshape(lead + (SSD_GROUPS, 2 * gh))
        t = jnp.pad(t, [(0, 0)] * (len(lead) + 1) + [(0, LANES - 2 * gh)])
        return t.reshape(lead + (SSD_GROUPS * LANES,))

    w_in = w_in.astype(BF16)
    w = jnp.concatenate([w_in[:, :SSD_D_INNER], w_in[:, n_state:], w_in[:, SSD_D_INNER:SSD_CONV_DIM],
                         dt_blocks(w_in[:, SSD_CONV_DIM:n_state])], axis=1)
    return (w, conv_w, conv_b.reshape(1, -1), dt_blocks(dt_bias.reshape(-1)).reshape(1, -1),
            dt_blocks(a_log.reshape(-1)).reshape(1, -1), jnp.repeat(d_skip, SSD_HEAD_DIM).reshape(1, -1))


def _ssd(proj, conv_w, conv_b, dt_bias, a_log, d_skip, norm_w, n_ctx, n_lat, n_batch):
    rows = n_ctx + n_lat
    gd = SSD_GROUP_DIM
    xb = SSD_D_INNER // gd
    bb = 2 * SSD_D_INNER // LANES
    sec = lambda width, off: pl.BlockSpec((rows, width), lambda b, g: (b, off + g))
    par = lambda r, width, off: pl.BlockSpec((r, width), lambda b, g: (0, off + g))
    return pl.pallas_call(
        functools.partial(_ssd_kernel, n_ctx=n_ctx, n_lat=n_lat, tile=_scan_tile(rows, SSD_CHUNK)),
        grid=(n_batch, SSD_GROUPS),
        in_specs=[sec(gd, 0), sec(gd, xb), sec(LANES, bb), sec(LANES, bb + SSD_GROUPS),
                  sec(LANES, bb + 2 * SSD_GROUPS),
                  par(3, gd, 0), par(3, LANES, SSD_D_INNER // LANES), par(3, LANES, SSD_D_INNER // LANES + SSD_GROUPS),
                  par(1, gd, 0), par(1, LANES, SSD_D_INNER // LANES), par(1, LANES, SSD_D_INNER // LANES + SSD_GROUPS),
                  par(1, LANES, 0), par(1, LANES, 0), par(1, gd, 0), par(1, gd, 0)],
        out_specs=pl.BlockSpec((rows, gd), lambda b, g: (b, g)),
        out_shape=jax.ShapeDtypeStruct((proj.shape[0], SSD_D_INNER), BF16),
        scratch_shapes=[pltpu.VMEM((rows, gd), F32),
                        pltpu.VMEM((rows, SSD_STATE), BF16),
                        pltpu.VMEM((rows, SSD_STATE), BF16),
                        pltpu.VMEM((rows, LANES), F32),
                        pltpu.VMEM((rows, LANES), F32),
                        pltpu.VMEM((rows, gd), F32),
                        pltpu.VMEM((rows, gd), F32),
                        pltpu.VMEM((rows, gd), F32),
                        pltpu.VMEM((rows, gd), BF16),
                        pltpu.VMEM((rows, gd), BF16),
                        pltpu.VMEM((SSD_STATE, gd), F32),
                        pltpu.VMEM((SSD_STATE, gd), F32)],
        compiler_params=_params("parallel", "parallel"),
        name="ssd",
    )(proj, proj, proj, proj, proj, conv_w, conv_w, conv_w, conv_b, conv_b, conv_b,
      dt_bias, a_log, d_skip, norm_w.reshape(1, -1))


HG_OWN, HG_EARLIER, HG_LATER = 1.0, 2.0, 3.0


def _hgrn_fill_tables(tr, tri_ref, kind_ref, place_ref):
    ch, sub = HG_CHUNK, HG_SUB
    ri = lax.broadcasted_iota(jnp.int32, (tr, tr), 0)
    ci = lax.broadcasted_iota(jnp.int32, (tr, tr), 1)
    same = (ri // ch) == (ci // ch)
    tri_ref[0] = (same & (ri >= ci)).astype(BF16)
    tri_ref[1] = (same & (ri <= ci)).astype(BF16)
    rb, cb = ri // sub, ci // sub
    kind_ref[...] = jnp.where(rb == cb, HG_OWN,
                              jnp.where(same & (cb < rb), HG_EARLIER, jnp.where(same & (cb > rb), HG_LATER, 0.0)))
    src_key = lax.broadcasted_iota(jnp.int32, (sub * LANES, LANES), 0) // LANES
    dst_key = lax.broadcasted_iota(jnp.int32, (sub * LANES, LANES), 1) % sub
    place_ref[...] = (src_key == dst_key).astype(BF16)


def _hgrn_intra(t, tr, rev, tables, lg_ref, k_ref, q_ref, v_ref, o_ref, qst_ref, upd_ref, dec_ref):
    ch, sub = HG_CHUNK, HG_SUB
    npc, nb, nblk = tr // ch, ch // sub, tr // sub
    tri_ref, kind_ref, place_ref = tables
    r0 = pl.multiple_of(t * tr, tr)
    lg = lg_ref[pl.ds(r0, tr), :]
    k = k_ref[pl.ds(r0, tr), :]
    q = q_ref[pl.ds(r0, tr), :]
    v = v_ref[pl.ds(r0, tr), :]
    tri = tri_ref[1 if rev else 0]
    lg_hi, lg_lo = _split_bf16(lg)
    cum = jnp.dot(tri, lg_hi, preferred_element_type=F32) + jnp.dot(tri, lg_lo, preferred_element_type=F32)

    last = [c * ch + (0 if rev else ch - 1) for c in range(npc)]
    qst_ref[pl.ds(r0, tr), :] = (q * jnp.exp2(cum)).astype(BF16)
    k_end = (k * jnp.exp2(_rows_from(cum, last, ch) - cum)).astype(BF16)
    for c in range(npc):
        chunk = t * npc + c
        dec_ref[pl.ds(chunk, 1), :] = jnp.exp2(cum[last[c]:last[c] + 1, :])
        upd_ref[pl.ds(pl.multiple_of(chunk * LANES, LANES), LANES), :] = lax.dot_general(
            v[c * ch:(c + 1) * ch], k_end[c * ch:(c + 1) * ch], _TN, preferred_element_type=F32)

    def own_edge(j):
        i = j % nb
        if rev:
            return None if i == nb - 1 else (j + 1) * sub
        return None if i == 0 else j * sub - 1

    qs = q * jnp.exp2(cum - _rows_from(cum, [own_edge(j) for j in range(nblk)], sub))
    zero8 = jnp.zeros((sub, LANES), F32)
    q_slabs, k_slabs = [], []
    for i in (range(nb - 1) if rev else range(1, nb)):
        q_slabs.append(jnp.concatenate([qs[j * sub:(j + 1) * sub] if j % nb == i else zero8
                                        for j in range(nblk)], axis=0).astype(BF16))
        pieces = []
        for c in range(npc):
            base = c * ch
            e = base + ((i + 1) * sub if rev else i * sub - 1)
            lo, hi = ((i + 1) * sub, ch) if rev else (0, i * sub)
            part = k[base + lo:base + hi] * jnp.exp2(cum[e:e + 1, :] - cum[base + lo:base + hi])
            pad = jnp.zeros((ch - (hi - lo), LANES), F32)
            pieces += [pad, part] if rev else [part, pad]
        k_slabs.append(jnp.concatenate(pieces, axis=0).astype(BF16))
    a_off = lax.dot_general(jnp.concatenate(q_slabs, axis=1), jnp.concatenate(k_slabs, axis=1), _NT,
                            preferred_element_type=F32)

    row8 = lax.broadcasted_iota(jnp.int32, (sub, LANES), 0)
    prods = []
    for s in range(sub):
        idx = [j * sub + s for j in range(nblk)]
        unseen = jnp.where((row8 <= s) if rev else (row8 >= s), 0.0, NEG_INF)
        w = jnp.exp2(cum - _rows_from(cum, idx, sub) + jnp.concatenate([unseen] * nblk, axis=0))
        prods.append((q * _rows_from(k, idx, sub) * w).astype(BF16))
    a_own = jnp.dot(jnp.concatenate(prods, axis=1), place_ref[...], preferred_element_type=F32)
    kind = kind_ref[...]
    a = jnp.where(kind == HG_OWN, jnp.concatenate([a_own] * (tr // LANES), axis=1),
                  jnp.where(kind == (HG_LATER if rev else HG_EARLIER), a_off, 0.0))
    o_ref[pl.ds(r0, tr), :] = jnp.dot(a.astype(BF16), v, preferred_element_type=F32)


def _hgrn_state_step(c, st, upd_ref, dec_ref, sin_ref):
    rows = pl.ds(pl.multiple_of(c * LANES, LANES), LANES)
    sin_ref[rows, :] = st.astype(BF16)
    return st * dec_ref[pl.ds(c, 1), :] + upd_ref[rows, :]


def _hgrn_readout(c, qst_ref, sin_ref):
    ch = HG_CHUNK
    return lax.dot_general(qst_ref[pl.ds(pl.multiple_of(c * ch, ch), ch), :],
                           sin_ref[pl.ds(pl.multiple_of(c * LANES, LANES), LANES), :], _NT,
                           preferred_element_type=F32)


def _hgrn_kernel(ff_ref, fb_ref, v_ref, q_ref, gate_ref, lb_ref, nw_ref, out_ref,
                 lgf_ref, lgb_ref, kf_ref, kb_ref, qa_ref, o_ref, ob_ref, qstf_ref, qstb_ref,
                 updf_ref, updb_ref, decf_ref, decb_ref, sinf_ref, sinb_ref, tri_ref, kind_ref, place_ref,
                 *, n_ctx, n_lat, tile):
    tables = (tri_ref, kind_ref, place_ref)
    _hgrn_fill_tables(tile, *tables)
    for d, (f_ref, lg_ref, k_ref) in enumerate(((ff_ref, lgf_ref, kf_ref), (fb_ref, lgb_ref, kb_ref))):
        lb = lb_ref[d:d + 1, :]
        g = lb + (1.0 - lb) * jax.nn.sigmoid(f_ref[...].astype(F32))
        lg_ref[...] = jnp.log(g) * LOG2_E
        k_ref[...] = 1.0 - g
    qa_ref[...] = _silu(q_ref[...].astype(F32))

    fwd = (qstf_ref, updf_ref, decf_ref)
    bwd = (qstb_ref, updb_ref, decb_ref)
    fwd_scan = (updf_ref, decf_ref, sinf_ref)
    bwd_scan = (updb_ref, decb_ref, sinb_ref)

    def intra_step(t, carry):
        _hgrn_intra(t, tile, False, tables, lgf_ref, kf_ref, qa_ref, v_ref, o_ref, *fwd)
        _hgrn_intra(t, tile, True, tables, lgb_ref, kb_ref, qa_ref, v_ref, ob_ref, *bwd)
        return carry

    lax.fori_loop(0, (n_ctx + n_lat) // tile, intra_step, 0)

    n_c = n_ctx // HG_CHUNK
    n_l = n_lat // HG_CHUNK

    def ctx_step(s, carry):
        return (_hgrn_state_step(s, carry[0], *fwd_scan),
                _hgrn_state_step(n_c - 1 - s, carry[1], *bwd_scan))

    def lat_step(s, carry):
        return (_hgrn_state_step(n_c + s, carry[0], *fwd_scan),
                _hgrn_state_step(n_c + n_l - 1 - s, carry[1], *bwd_scan))

    zero = jnp.zeros((LANES, LANES), F32)
    carry = lax.fori_loop(0, n_c, ctx_step, (zero, zero), unroll=SCAN_UNROLL)
    lax.fori_loop(0, n_l, lat_step, carry, unroll=SCAN_UNROLL)

    def read_step(c, carry):
        rows = pl.ds(pl.multiple_of(c * HG_CHUNK, HG_CHUNK), HG_CHUNK)
        o_ref[rows, :] += _hgrn_readout(c, qstf_ref, sinf_ref) + _hgrn_readout(c, qstb_ref, sinb_ref)
        return carry

    lax.fori_loop(0, n_c + n_l, read_step, 0, unroll=READ_UNROLL)

    o = o_ref[...] + ob_ref[...]
    out_ref[...] = (_rms_rows(o) * nw_ref[...] * _silu(gate_ref[...].astype(F32))).astype(out_ref.dtype)


def _hgrn(proj, lower, norm_w, n_ctx, n_lat, n_batch):
    rows = n_ctx + n_lat
    sec = lambda off: pl.BlockSpec((rows, LANES), lambda b, h: (b, off * HG_HEADS + h))
    seq = lambda dt: pltpu.VMEM((rows, LANES), dt)
    n_chunks = rows // HG_CHUNK
    upd = pltpu.VMEM((n_chunks * LANES, LANES), F32)
    dec = pltpu.VMEM((-(-n_chunks // 8) * 8, LANES), F32)
    sin = pltpu.VMEM((n_chunks * LANES, LANES), BF16)
    tile = _scan_tile(rows, HG_CHUNK)
    tables = [pltpu.VMEM((2, tile, tile), BF16), pltpu.VMEM((tile, tile), F32),
              pltpu.VMEM((HG_SUB * LANES, LANES), BF16)]
    return pl.pallas_call(
        functools.partial(_hgrn_kernel, n_ctx=n_ctx, n_lat=n_lat, tile=tile),
        grid=(n_batch, HG_HEADS),
        in_specs=[sec(0), sec(1), sec(2), sec(3), sec(4),
                  pl.BlockSpec((2, LANES), lambda b, h: (0, h)),
                  pl.BlockSpec((1, LANES), lambda b, h: (0, h))],
        out_specs=pl.BlockSpec((rows, LANES), lambda b, h: (b, h)),
        out_shape=jax.ShapeDtypeStruct((proj.shape[0], HG_VAL), BF16),
        scratch_shapes=[seq(F32), seq(F32), seq(F32), seq(F32), seq(F32), seq(F32), seq(F32),
                        seq(BF16), seq(BF16), upd, upd, dec, dec, sin, sin] + tables,
        compiler_params=_params("parallel", "parallel"),
        name="hgrn2",
    )(proj, proj, proj, proj, proj, lower, norm_w.reshape(1, -1))


def _hgrn_lower_bounds(lb_raw):
    p = jax.nn.softmax(lb_raw.astype(F32), axis=1)
    return jnp.cumsum(p, axis=1) - p[:, :1]


def kernel(x, c, ctx, c_ctx, ada_w, ada_b, norm_g, ffn_w_in, ffn_w_out, attn_w_in, attn_w_out, attn_sink,
           ssd_w_in, ssd_conv_w, ssd_conv_b, ssd_dt_bias, ssd_a_log, ssd_d, ssd_norm_w, ssd_w_out,
           hgrn_w_in, hgrn_lb, hgrn_norm_w, hgrn_w_out):
    n_batch, n_lat, d = x.shape
    n_ctx = ctx.shape[1]
    rows = n_ctx + n_lat
    tm = _token_tile(n_ctx, n_ctx + n_lat)
    tm_lat = _token_tile(n_ctx, n_lat)

    r = jnp.concatenate([ctx, x], axis=1).reshape(n_batch * rows, d)
    n_cond = -(-(n_batch + 1) // 8) * 8
    c_rows = jnp.concatenate([c, c_ctx[None, :], jnp.zeros((n_cond - n_batch - 1, d), F32)], axis=0)
    mods = _ada_mods(c_rows, ada_w, ada_b)
    tables = _rope_tables(n_ctx, n_lat)
    lower = _hgrn_lower_bounds(hgrn_lb)

    for i in range(DEPTH):
        kind, j = i % N_MIXERS, i // N_MIXERS
        mod = mods[i]
        if kind == 0:
            proj = _modproj(r, mod, norm_g[i, 0], attn_w_in[j].astype(BF16), n_ctx, tm, n_batch)
            y = _attention(proj, attn_sink[j], tables, n_ctx, n_lat, n_batch)
            w_out = attn_w_out[j]
        elif kind == 1:
            w_in, conv_w, conv_b, dt_bias, a_log, d_skip = _ssd_layout(
                ssd_w_in[j], ssd_conv_w[j], ssd_conv_b[j], ssd_dt_bias[j], ssd_a_log[j], ssd_d[j])
            proj = _modproj(r, mod, norm_g[i, 0], w_in, n_ctx, tm, n_batch)
            y = _ssd(proj, conv_w, conv_b, dt_bias, a_log, d_skip, ssd_norm_w[j], n_ctx, n_lat, n_batch)
            w_out = ssd_w_out[j]
        else:
            proj = _modproj(r, mod, norm_g[i, 0], hgrn_w_in[j].astype(BF16), n_ctx, tm, n_batch)
            y = _hgrn(proj, lower[:, i], hgrn_norm_w[j], n_ctx, n_lat, n_batch)
            w_out = hgrn_w_out[j]
        ffn_w = (ffn_w_in[i].astype(BF16), ffn_w_out[i].astype(BF16))
        if i < DEPTH - 1:
            r = _outproj(y, w_out.astype(BF16), r, mod, norm_g[i, 1], n_ctx, tm, n_batch)
            r = _ffn(r, mod, norm_g[i, 2], norm_g[i, 3], *ffn_w, n_ctx, tm, n_batch)
        else:
            r = _outproj(y, w_out.astype(BF16), r, mod, norm_g[i, 1], n_ctx, tm_lat, n_batch, lat_rows=n_lat)
            r = _ffn(r, mod, norm_g[i, 2], norm_g[i, 3], *ffn_w, n_ctx, tm_lat, n_batch, has_ctx=False)
    return r.reshape(n_batch, n_lat, d)
```

```python
import functools
import math

import jax
import jax.numpy as jnp
from jax import lax
from jax.experimental import pallas as pl
from jax.experimental.pallas import tpu as pltpu

F32 = jnp.float32
BF16 = jnp.bfloat16

D_MODEL = 2048
DEPTH = 4
N_MIXERS = 3
GRID_W = 64
RMS_EPS = 1e-6
N_MOD = 6

ATT_HEADS = 32
ATT_KV_HEADS = 4
ATT_GQA = ATT_HEADS // ATT_KV_HEADS
ATT_HEAD_DIM = 64
ATT_WINDOW = 128
ATT_BLOCK = 128
ATT_Q_DIM = ATT_HEADS * ATT_HEAD_DIM
ATT_KV_DIM = ATT_KV_HEADS * ATT_HEAD_DIM
ROPE_THETA = 10000.0
ROPE_PAIRS = ATT_HEAD_DIM // 4
NEG_INF = -1e30
LOG2_E = 1.0 / math.log(2.0)

SSD_D_INNER = 2 * D_MODEL
SSD_HEAD_DIM = 64
SSD_HEADS = SSD_D_INNER // SSD_HEAD_DIM
SSD_GROUPS = 8
SSD_GROUP_HEADS = SSD_HEADS // SSD_GROUPS
SSD_GROUP_DIM = SSD_D_INNER // SSD_GROUPS
SSD_STATE = 128
SSD_GN = SSD_GROUPS * SSD_STATE
SSD_CONV_DIM = SSD_D_INNER + 2 * SSD_GN
SSD_CHUNK = 64

HG_EXPAND = 128
HG_HEADS = D_MODEL // HG_EXPAND
HG_KEY = HG_HEADS * HG_EXPAND
HG_VAL = D_MODEL
HG_CHUNK = 64
HG_SUB = 8

FFN_DIM = -(-8 * D_MODEL // (3 * 256)) * 256

LANES = 128
MXU_TILE = 256
VMEM_LIMIT_BYTES = 56 * 1024 * 1024
MAX_TOKEN_TILE = 768
MAX_PROJ_TILE = 2816
SCAN_UNROLL = 2
READ_UNROLL = 6
TILE_UNROLL = 3


def _params(*sem):
    return pltpu.CompilerParams(dimension_semantics=sem, vmem_limit_bytes=VMEM_LIMIT_BYTES)


def _token_tile(n_ctx, rows):
    best = n_ctx
    for mult in range(1, rows // n_ctx + 1):
        t = mult * n_ctx
        if rows % t == 0 and t <= MAX_TOKEN_TILE:
            best = t
    assert rows % best == 0
    return best


def _silu(t):
    h = 0.5 * t
    return h + h * jnp.tanh(h)


def _rms_rows(t):
    return t * lax.rsqrt(jnp.mean(t * t, axis=-1, keepdims=True) + RMS_EPS)


def _rows_from(t, idx, height):
    width = t.shape[1]
    return jnp.concatenate([jnp.zeros((height, width), t.dtype) if r is None
                            else jnp.broadcast_to(t[r:r + 1, :], (height, width)) for r in idx], axis=0)


_NT = (((1,), (1,)), ((), ()))
_TN = (((0,), (0,)), ((), ()))


def _scan_tile(rows, chunk):
    return next(t for t in (4 * chunk, 2 * chunk) if rows % t == 0 and t % LANES == 0)


def _ada_kernel(c_ref, w_ref, b_ref, o_ref):
    s = _silu(c_ref[...]).astype(BF16)
    o_ref[...] = jnp.dot(s, w_ref[...].astype(BF16), preferred_element_type=F32) + b_ref[...]


def _ada_mods(c_rows, ada_w, ada_b):
    r = c_rows.shape[0]
    n = ada_w.shape[-1]
    tn = 1024
    out = pl.pallas_call(
        _ada_kernel,
        grid=(DEPTH, n // tn),
        in_specs=[pl.BlockSpec((r, D_MODEL), lambda l, j: (0, 0)),
                  pl.BlockSpec((None, D_MODEL, tn), lambda l, j: (l, 0, j)),
                  pl.BlockSpec((None, 1, tn), lambda l, j: (l, 0, j))],
        out_specs=pl.BlockSpec((None, r, tn), lambda l, j: (l, 0, j)),
        out_shape=jax.ShapeDtypeStruct((DEPTH, r, n), F32),
        compiler_params=_params("parallel", "parallel"),
        name="ada_mods",
    )(c_rows, ada_w, ada_b.reshape(DEPTH, 1, n))
    return out.reshape(DEPTH, r, N_MOD, D_MODEL)


def _first_tile(tiles_per_batch, has_ctx):
    return (pl.program_id(0) % tiles_per_batch == 0) if has_ctx else None


def _mod_row(mod_ref, modc_ref, row, first, r0):
    m = mod_ref[row:row + 1, :]
    if r0 == 0 and first is not None:
        m = jnp.where(first, modc_ref[row:row + 1, :], m)
    return m


def _modulate_into(x_ref, u_ref, mod_ref, modc_ref, g_ref, shift_row, first, n_ctx):
    g = g_ref[...]
    for r0 in range(0, x_ref.shape[0], n_ctx):
        shift = _mod_row(mod_ref, modc_ref, shift_row, first, r0)
        scale = _mod_row(mod_ref, modc_ref, shift_row + 1, first, r0)
        x = x_ref[r0:r0 + n_ctx, :]
        u_ref[r0:r0 + n_ctx, :] = (_rms_rows(x) * (g * (1.0 + scale)) + shift).astype(u_ref.dtype)


def _gated_residual_into(o_ref, f_src, x_ref, mod_ref, modc_ref, g_ref, gate_row, first, n_ctx):
    g = g_ref[...]
    for r0 in range(0, x_ref.shape[0], n_ctx):
        gate = _mod_row(mod_ref, modc_ref, gate_row, first, r0)
        f = f_src[r0:r0 + n_ctx, :]
        o_ref[r0:r0 + n_ctx, :] = x_ref[r0:r0 + n_ctx, :] + gate * (_rms_rows(f) * g)


def _mod_specs(tiles_per_batch, n_batch):
    return [pl.BlockSpec((None, N_MOD, D_MODEL), lambda i, j: (i // tiles_per_batch, 0, 0)),
            pl.BlockSpec((None, N_MOD, D_MODEL), lambda i, j: (n_batch, 0, 0))]


def _modproj_kernel(x_ref, mod_ref, modc_ref, g_ref, w_ref, o_ref, u_ref, *, n_ctx, tiles_per_batch,
                    has_ctx=True):
    first = _first_tile(tiles_per_batch, has_ctx)

    @pl.when(pl.program_id(1) == 0)
    def _():
        _modulate_into(x_ref, u_ref, mod_ref, modc_ref, g_ref, 0, first, n_ctx)

    tn = o_ref.shape[1]
    cut = -(-tn // (2 * MXU_TILE)) * MXU_TILE
    for lo, hi in ((0, cut), (cut, tn)):
        if hi > lo:
            o_ref[:, lo:hi] = jnp.dot(u_ref[...], w_ref[:, lo:hi], preferred_element_type=F32).astype(o_ref.dtype)


def _modproj(x, mod, g, w, n_ctx, tm, n_batch):
    t, d = x.shape
    n = w.shape[1]
    tn = max(c for c in range(LANES, MAX_PROJ_TILE + 1, LANES) if n % c == 0)
    tpb = t // n_batch // tm
    return pl.pallas_call(
        functools.partial(_modproj_kernel, n_ctx=n_ctx, tiles_per_batch=tpb),
        grid=(t // tm, n // tn),
        in_specs=[pl.BlockSpec((tm, d), lambda i, j: (i, 0))] + _mod_specs(tpb, n_batch)
        + [pl.BlockSpec((1, d), lambda i, j: (0, 0)),
           pl.BlockSpec((d, tn), lambda i, j: (0, j))],
        out_specs=pl.BlockSpec((tm, tn), lambda i, j: (i, j)),
        out_shape=jax.ShapeDtypeStruct((t, n), BF16),
        scratch_shapes=[pltpu.VMEM((tm, d), BF16)],
        compiler_params=_params("parallel", "arbitrary"),
        name="modproj",
    )(x, mod, mod, g.reshape(1, d), w)


def _outproj_kernel(y_ref, w_ref, x_ref, mod_ref, modc_ref, g_ref, o_ref, *, n_ctx, tiles_per_batch, nk,
                    has_ctx):
    k = pl.program_id(1)
    first = _first_tile(tiles_per_batch, has_ctx)
    part = jnp.dot(y_ref[...], w_ref[...], preferred_element_type=F32)
    if nk == 1:
        _gated_residual_into(o_ref, part, x_ref, mod_ref, modc_ref, g_ref, 2, first, n_ctx)
        return

    @pl.when(k == 0)
    def _():
        o_ref[...] = part

    @pl.when(k > 0)
    def _():
        o_ref[...] += part

    @pl.when(k == nk - 1)
    def _():
        _gated_residual_into(o_ref, o_ref, x_ref, mod_ref, modc_ref, g_ref, 2, first, n_ctx)


def _outproj(y, w, x, mod, g, n_ctx, tm, n_batch, lat_rows=None):
    t, d = x.shape
    kdim = y.shape[1]
    tk = min(kdim, 2048)
    nk = kdim // tk
    rows = t // n_batch
    if lat_rows is None:
        tpb, t_out = rows // tm, t
        y_spec = pl.BlockSpec((tm, tk), lambda i, k: (i, k))
        x_spec = pl.BlockSpec((tm, d), lambda i, k: (i, 0))
    else:
        tpb, t_out = lat_rows // tm, n_batch * lat_rows
        assert rows % n_ctx == 0 and lat_rows % n_ctx == 0 and tm % n_ctx == 0
        first_row = lambda i: pl.multiple_of(
            ((i // tpb) * (rows // n_ctx) + (rows - lat_rows) // n_ctx + (i % tpb) * (tm // n_ctx)) * n_ctx, n_ctx)
        y_spec = pl.BlockSpec((pl.Element(tm), pl.Element(tk)), lambda i, k: (first_row(i), k * tk))
        x_spec = pl.BlockSpec((pl.Element(tm), pl.Element(d)), lambda i, k: (first_row(i), 0))
    return pl.pallas_call(
        functools.partial(_outproj_kernel, n_ctx=n_ctx, tiles_per_batch=tpb, nk=nk, has_ctx=lat_rows is None),
        grid=(t_out // tm, nk),
        in_specs=[y_spec, pl.BlockSpec((tk, d), lambda i, k: (k, 0)), x_spec] + _mod_specs(tpb, n_batch)
        + [pl.BlockSpec((1, d), lambda i, k: (0, 0))],
        out_specs=pl.BlockSpec((tm, d), lambda i, k: (i, 0)),
        out_shape=jax.ShapeDtypeStruct((t_out, d), F32),
        compiler_params=_params("parallel", "arbitrary"),
        name="outproj",
    )(y, w, x, mod, mod, g.reshape(1, d))


def _ffn_kernel(x_ref, mod_ref, modc_ref, g2_ref, g3_ref, wg_ref, wu_ref, wo_ref, o_ref, u_ref,
                *, n_ctx, tiles_per_batch, nf, has_ctx):
    j = pl.program_id(1)
    first = _first_tile(tiles_per_batch, has_ctx)

    def hidden_tile(assign):
        u = u_ref[...]
        gate = jnp.dot(u, wg_ref[...], preferred_element_type=F32)
        up = jnp.dot(u, wu_ref[...], preferred_element_type=F32)
        h = (_silu(gate) * up).astype(BF16)
        part = jnp.dot(h, wo_ref[...], preferred_element_type=F32)
        if assign:
            o_ref[...] = part
        else:
            o_ref[...] += part

    @pl.when(j == 0)
    def _():
        _modulate_into(x_ref, u_ref, mod_ref, modc_ref, g2_ref, 3, first, n_ctx)
        hidden_tile(True)

    @pl.when(j > 0)
    def _():
        hidden_tile(False)

    @pl.when(j == nf - 1)
    def _():
        _gated_residual_into(o_ref, o_ref, x_ref, mod_ref, modc_ref, g3_ref, 5, first, n_ctx)


def _ffn(x, mod, g2, g3, w_in, w_out, n_ctx, tm, n_batch, has_ctx=True):
    t, d = x.shape
    f = w_out.shape[0]
    tf = 512
    nf = f // tf
    tpb = t // n_batch // tm
    return pl.pallas_call(
        functools.partial(_ffn_kernel, n_ctx=n_ctx, tiles_per_batch=tpb, nf=nf, has_ctx=has_ctx),
        grid=(t // tm, nf),
        in_specs=[pl.BlockSpec((tm, d), lambda i, j: (i, 0))] + _mod_specs(tpb, n_batch)
        + [pl.BlockSpec((1, d), lambda i, j: (0, 0)),
           pl.BlockSpec((1, d), lambda i, j: (0, 0)),
           pl.BlockSpec((d, tf), lambda i, j: (0, j)),
           pl.BlockSpec((d, tf), lambda i, j: (0, nf + j)),
           pl.BlockSpec((tf, d), lambda i, j: (j, 0))],
        out_specs=pl.BlockSpec((tm, d), lambda i, j: (i, 0)),
        out_shape=jax.ShapeDtypeStruct((t, d), F32),
        scratch_shapes=[pltpu.VMEM((tm, d), BF16)],
        compiler_params=_params("parallel", "arbitrary"),
        name="ffn",
    )(x, mod, mod, g2.reshape(1, d), g3.reshape(1, d), w_in, w_in, w_out)


def _rope_tables(n_ctx, n_lat):
    pos = jnp.arange(n_lat)
    row = (pos // GRID_W).astype(F32)
    col = (pos % GRID_W).astype(F32)
    inv_freq = ROPE_THETA ** (-jnp.arange(ROPE_PAIRS, dtype=F32) / ROPE_PAIRS)
    ang_row = row[:, None] * inv_freq
    ang_col = col[:, None] * inv_freq
    zero = jnp.zeros_like(ang_row)
    cos = jnp.concatenate([jnp.cos(ang_row)] * 2 + [jnp.cos(ang_col)] * 2, axis=1)
    s_lo = jnp.concatenate([-jnp.sin(ang_row), zero, -jnp.sin(ang_col), zero], axis=1)
    s_hi = jnp.concatenate([zero, jnp.sin(ang_row), zero, jnp.sin(ang_col)], axis=1)

    def full(tab, ctx_val):
        tab = jnp.concatenate([jnp.full((n_ctx, ATT_HEAD_DIM), ctx_val, F32), tab], axis=0)
        return jnp.concatenate([tab, tab], axis=1)

    return full(cos, 1.0), full(s_lo, 0.0), full(s_hi, 0.0)


def _rope(x, c, s_lo, s_hi):
    return x * c + pltpu.roll(x, LANES - ROPE_PAIRS, 1) * s_lo + pltpu.roll(x, ROPE_PAIRS, 1) * s_hi


def _attn_kernel(sink_ref, pq_ref, pk_ref, pv_ref, cq_ref, slq_ref, shq_ref, ck_ref, slk_ref, shk_ref,
                 o_ref, kx_ref, vt_ref, qt_ref, ot_ref, *, n_ctx, n_lat):
    t = pl.program_id(1)
    blk = ATT_BLOCK
    n_blocks = (n_ctx + n_lat) // blk
    ctx_blocks = n_ctx // blk
    hd = ATT_HEAD_DIM

    @pl.when(t == 0)
    def _():
        for i in range(n_blocks):
            rs = slice(i * blk, (i + 1) * blk)
            for c0 in range(0, ATT_KV_DIM, LANES):
                k = pk_ref[rs, c0:c0 + LANES].astype(F32)
                kx_ref[i, :, c0:c0 + LANES] = _rope(k, ck_ref[rs, :], slk_ref[rs, :], shk_ref[rs, :]).astype(BF16)
                vt_ref[i, c0:c0 + LANES, :] = pv_ref[rs, c0:c0 + LANES].astype(F32).T.astype(BF16)
        kx_ref[n_blocks] = jnp.zeros((blk, ATT_KV_DIM), BF16)
        vt_ref[n_blocks] = jnp.zeros((ATT_KV_DIM, blk), BF16)

    scale = ATT_HEAD_DIM ** -0.5 * LOG2_E
    for c0 in range(0, ATT_Q_DIM, LANES):
        q = pq_ref[:, 2 * ATT_KV_DIM + c0:2 * ATT_KV_DIM + c0 + LANES].astype(F32)
        qt_ref[c0:c0 + LANES, :] = (_rope(q, cq_ref[...], slq_ref[...], shq_ref[...]) * scale).T.astype(BF16)

    n = t - ctx_blocks
    key = lax.broadcasted_iota(jnp.int32, (blk, 2 * LANES), 0)
    qry = lax.broadcasted_iota(jnp.int32, (blk, 2 * LANES), 1) % blk
    blocks = [(i, None) for i in range(ctx_blocks)] + [
        (jnp.maximum(ctx_blocks + n - 1, 0), (key >= qry) & (n >= 1)),
        (jnp.maximum(ctx_blocks + n, 0), (key >= 0) & (n >= 0)),
        (ctx_blocks + n + 1, (key <= qry) & (n >= 0) & (n + 1 < n_lat // blk))]
    upper = lax.broadcasted_iota(jnp.int32, (1, 2 * LANES), 1) >= LANES
    zero = jnp.zeros((hd, LANES), BF16)

    for pair in range(ATT_KV_HEADS // 2):
        ps = slice(pair * LANES, (pair + 1) * LANES)
        for g in range(ATT_GQA):
            h0 = (2 * pair) * ATT_GQA + g
            h1 = (2 * pair + 1) * ATT_GQA + g
            w = jnp.concatenate([jnp.concatenate([qt_ref[h0 * hd:(h0 + 1) * hd, :], zero], axis=0),
                                 jnp.concatenate([zero, qt_ref[h1 * hd:(h1 + 1) * hd, :]], axis=0)], axis=1)
            sink = jnp.where(upper, sink_ref[h1], sink_ref[h0]) * LOG2_E
            scores = []
            m = sink
            for i, visible in blocks:
                s = jnp.dot(kx_ref[i, :, ps], w, preferred_element_type=F32)
                if visible is not None:
                    s = jnp.where(visible, s, NEG_INF)
                scores.append(s)
                m = jnp.maximum(m, jnp.max(s, axis=0, keepdims=True))
            den = jnp.exp2(sink - m)
            acc = jnp.zeros((LANES, 2 * LANES), F32)
            for (i, _), s in zip(blocks, scores):
                p = jnp.exp2(s - m)
                den = den + jnp.sum(p, axis=0, keepdims=True)
                acc = acc + jnp.dot(vt_ref[i, ps, :], p.astype(BF16), preferred_element_type=F32)
            ot_ref[h0 * hd:(h0 + 1) * hd, :] = acc[0:hd, 0:LANES] / den[:, 0:LANES]
            ot_ref[h1 * hd:(h1 + 1) * hd, :] = acc[hd:2 * hd, LANES:2 * LANES] / den[:, LANES:2 * LANES]

    for c0 in range(0, ATT_Q_DIM, LANES):
        o_ref[:, c0:c0 + LANES] = ot_ref[c0:c0 + LANES, :].T.astype(o_ref.dtype)


def _attention(proj, sink, tables, n_ctx, n_lat, n_batch):
    rows = n_ctx + n_lat
    nq = rows // ATT_BLOCK
    cos, s_lo, s_hi = tables
    qtab = pl.BlockSpec((ATT_BLOCK, LANES), lambda b, t: (t, 0))
    ktab = pl.BlockSpec((rows, LANES), lambda b, t: (0, 0))
    return pl.pallas_call(
        functools.partial(_attn_kernel, n_ctx=n_ctx, n_lat=n_lat),
        grid=(n_batch, nq),
        in_specs=[pl.BlockSpec(memory_space=pltpu.SMEM),
                  pl.BlockSpec((ATT_BLOCK, proj.shape[1]), lambda b, t: (b * nq + t, 0)),
                  pl.BlockSpec((rows, ATT_KV_DIM), lambda b, t: (b, 0)),
                  pl.BlockSpec((rows, ATT_KV_DIM), lambda b, t: (b, 1)),
                  qtab, qtab, qtab, ktab, ktab, ktab],
        out_specs=pl.BlockSpec((ATT_BLOCK, ATT_Q_DIM), lambda b, t: (b * nq + t, 0)),
        out_shape=jax.ShapeDtypeStruct((proj.shape[0], ATT_Q_DIM), BF16),
        scratch_shapes=[pltpu.VMEM((nq + 1, ATT_BLOCK, ATT_KV_DIM), BF16),
                        pltpu.VMEM((nq + 1, ATT_KV_DIM, ATT_BLOCK), BF16),
                        pltpu.VMEM((ATT_Q_DIM, ATT_BLOCK), BF16),
                        pltpu.VMEM((ATT_Q_DIM, ATT_BLOCK), F32)],
        compiler_params=_params("parallel", "arbitrary"),
        name="attention",
    )(sink.astype(F32), proj, proj, proj, cos, s_lo, s_hi, cos, s_lo, s_hi)


def _softplus(t):
    return jnp.maximum(t, 0.0) + jnp.log(1.0 + jnp.exp(-jnp.abs(t)))


def _split_bf16(t):
    hi = t.astype(BF16)
    return hi, (t - hi.astype(F32)).astype(BF16)


def _select_rows(sel, t):
    both = jnp.dot(sel, jnp.concatenate(_split_bf16(t), axis=1), preferred_element_type=F32)
    return both[:, 0:LANES] + both[:, LANES:2 * LANES]


def _conv_silu_into(src_ref, w_ref, b_ref, dst_ref, n_ctx):
    rows, width = src_ref.shape
    t = n_ctx
    ri = lax.broadcasted_iota(jnp.int32, (t, t), 0)
    ci = lax.broadcasted_iota(jnp.int32, (t, t), 1)
    down = (ri == ci + 1).astype(BF16)
    up = (ri + 1 == ci).astype(BF16)
    row8 = lax.broadcasted_iota(jnp.int32, (8, width), 0)
    w = w_ref[...]
    b = b_ref[...]
    for r0 in range(0, rows, t):
        x = src_ref[r0:r0 + t, :]
        prv = jnp.dot(down, x, preferred_element_type=F32)
        nxt = jnp.dot(up, x, preferred_element_type=F32)
        if r0 not in (0, n_ctx):
            edge = jnp.broadcast_to(src_ref[r0 - 1:r0, :].astype(F32), (8, width))
            prv = jnp.concatenate([jnp.where(row8 == 0, edge, prv[0:8]), prv[8:]], axis=0)
        if r0 + t not in (n_ctx, rows):
            edge = jnp.broadcast_to(src_ref[r0 + t:r0 + t + 1, :].astype(F32), (8, width))
            nxt = jnp.concatenate([nxt[:t - 8], jnp.where(row8 == 7, edge, nxt[t - 8:])], axis=0)
        out = b + prv * w[0:1, :] + x.astype(F32) * w[1:2, :] + nxt * w[2:3, :]
        dst_ref[r0:r0 + t, :] = _silu(out).astype(dst_ref.dtype)


def _ssd_intra(t, tr, rev, add, xa_ref, ba_ref, ca_ref, dtv_ref, da_ref, y_ref, cumx_ref, xdec_ref):
    q = SSD_CHUNK
    p = SSD_HEAD_DIM
    assert 2 * q == LANES and 2 * p == LANES
    npc = tr // q
    r0 = pl.multiple_of(t * tr, tr)
    xa = xa_ref[pl.ds(r0, tr), :]
    bc = ba_ref[pl.ds(r0, tr), :]
    cc = ca_ref[pl.ds(r0, tr), :]
    dtc = dtv_ref[pl.ds(r0, tr), :]
    dac = da_ref[pl.ds(r0, tr), :]
    gd = SSD_GROUP_DIM
    lane0 = SSD_GROUP_HEADS if rev else 0
    ri = lax.broadcasted_iota(jnp.int32, (tr, tr), 0)
    ci = lax.broadcasted_iota(jnp.int32, (tr, tr), 1)
    tri = (((ri // q) == (ci // q)) & ((ri <= ci) if rev else (ri >= ci))).astype(BF16)
    cum_hi, cum_lo = _split_bf16(_select_rows(tri, dac))
    cum = cum_hi.astype(F32) + cum_lo.astype(F32)
    src = lax.broadcasted_iota(jnp.int32, (LANES, gd), 0)
    dst = lax.broadcasted_iota(jnp.int32, (LANES, gd), 1) // p
    expand = (src == lane0 + dst).astype(BF16)
    dt_hi, dt_lo = _split_bf16(dtc)
    wide = jnp.dot(jnp.concatenate([jnp.concatenate([cum_hi, cum_lo], axis=1),
                                    jnp.concatenate([dt_hi, dt_lo], axis=1)], axis=0),
                   jnp.concatenate([expand, expand], axis=0), preferred_element_type=F32)
    cum_x = wide[0:tr]
    dt_x = wide[tr:2 * tr]
    tot_x = _rows_from(cum_x, [c * q + (0 if rev else q - 1) for c in range(npc)], q)
    xdt = xa * dt_x
    cumx_ref[pl.ds(r0, tr), :] = cum_x
    xdec_ref[pl.ds(r0, tr), :] = (xdt * jnp.exp2(tot_x - cum_x)).astype(BF16)

    row = lax.broadcasted_iota(jnp.int32, (q, LANES), 0)
    lane = lax.broadcasted_iota(jnp.int32, (q, LANES), 1)
    upper = lane >= q
    key = jnp.where(upper, lane - q, lane)
    seen = (row <= key) if rev else (row >= key)
    for c in range(npc):
        cs = slice(c * q, (c + 1) * q)
        cum_rows = jnp.concatenate([cum[cs], pltpu.roll(cum[cs], LANES - 1, 1)], axis=0).T
        cb2 = lax.dot_general(cc[cs], jnp.concatenate([bc[cs], bc[cs]], axis=0), _NT, preferred_element_type=F32)
        ys = []
        for pr in range(SSD_GROUP_HEADS // 2):
            sl = slice(pr * LANES, (pr + 1) * LANES)
            le = lane0 + 2 * pr
            decay = jnp.exp2(jnp.where(seen, cum_x[cs, sl] - cum_rows[le:le + 1, :], NEG_INF))
            xp = xdt[cs, sl]
            x2 = jnp.concatenate([jnp.where(upper, 0.0, xp), jnp.where(upper, xp, 0.0)], axis=0).astype(BF16)
            ys.append(jnp.dot((cb2 * decay).astype(BF16), x2, preferred_element_type=F32))
        y = jnp.concatenate(ys, axis=1)
        if add:
            y_ref[pl.ds(r0 + c * q, q), :] += y
        else:
            y_ref[pl.ds(r0 + c * q, q), :] = y


def _ssd_state_step(c, rev, ba_ref, ca_ref, y_ref, cumx_ref, xdec_ref, s_ref):
    q = SSD_CHUNK
    r0 = pl.multiple_of(c * q, q)
    cum_x = cumx_ref[pl.ds(r0, q), :]
    tot_x = cum_x[0:1, :] if rev else cum_x[q - 1:q, :]
    state = s_ref[...]
    y_ref[pl.ds(r0, q), :] += (jnp.dot(ca_ref[pl.ds(r0, q), :], state.astype(BF16), preferred_element_type=F32)
                               * jnp.exp2(cum_x))
    upd = lax.dot_general(ba_ref[pl.ds(r0, q), :], xdec_ref[pl.ds(r0, q), :], _TN, preferred_element_type=F32)
    s_ref[...] = state * jnp.exp2(tot_x) + upd


def _ssd_kernel(x_ref, z_ref, b_ref, c_ref, dt_ref, cwx_ref, cwb_ref, cwc_ref, cbx_ref, cbb_ref, cbc_ref,
                dtb_ref, alog_ref, dskip_ref, nw_ref, o_ref,
                xa_ref, ba_ref, ca_ref, dtv_ref, da_ref, y_ref, cumxf_ref, cumxb_ref, xdecf_ref, xdecb_ref,
                s_ref, sb_ref, *, n_ctx, n_lat, tile):
    _conv_silu_into(x_ref, cwx_ref, cbx_ref, xa_ref, n_ctx)
    _conv_silu_into(b_ref, cwb_ref, cbb_ref, ba_ref, n_ctx)
    _conv_silu_into(c_ref, cwc_ref, cbc_ref, ca_ref, n_ctx)
    dtv = _softplus(dt_ref[...].astype(F32) + dtb_ref[...])
    dtv_ref[...] = dtv
    da_ref[...] = dtv * (-jnp.exp(alog_ref[...]) * LOG2_E)

    seq = (xa_ref, ba_ref, ca_ref, dtv_ref, da_ref, y_ref)
    fwd = (cumxf_ref, xdecf_ref)
    bwd = (cumxb_ref, xdecb_ref)

    def intra_step(t, carry):
        _ssd_intra(t, tile, False, False, *seq, *fwd)
        _ssd_intra(t, tile, True, True, *seq, *bwd)
        return carry

    lax.fori_loop(0, (n_ctx + n_lat) // tile, intra_step, 0, unroll=TILE_UNROLL)

    n_c = n_ctx // SSD_CHUNK
    n_l = n_lat // SSD_CHUNK

    def ctx_step(s, carry):
        _ssd_state_step(s, False, ba_ref, ca_ref, y_ref, *fwd, s_ref)
        _ssd_state_step(n_c - 1 - s, True, ba_ref, ca_ref, y_ref, *bwd, sb_ref)
        return carry

    def lat_step(s, carry):
        _ssd_state_step(n_c + s, False, ba_ref, ca_ref, y_ref, *fwd, s_ref)
        _ssd_state_step(n_c + n_l - 1 - s, True, ba_ref, ca_ref, y_ref, *bwd, sb_ref)
        return carry

    s_ref[...] = jnp.zeros(s_ref.shape, F32)
    sb_ref[...] = jnp.zeros(sb_ref.shape, F32)
    lax.fori_loop(0, n_c, ctx_step, 0, unroll=SCAN_UNROLL)
    lax.fori_loop(0, n_l, lat_step, 0, unroll=SCAN_UNROLL)

    for r0 in range(0, n_ctx + n_lat, n_ctx):
        y = y_ref[r0:r0 + n_ctx, :] + xa_ref[r0:r0 + n_ctx, :] * dskip_ref[...]
        t = y * _silu(z_ref[r0:r0 + n_ctx, :].astype(F32))
        o_ref[r0:r0 + n_ctx, :] = (_rms_rows(t) * nw_ref[...]).astype(o_ref.dtype)


def _ssd_layout(w_in, conv_w, conv_b, dt_bias, a_log, d_skip):
    n_state = SSD_CONV_DIM + 2 * SSD_HEADS
    gh = SSD_GROUP_HEADS

    def dt_blocks(t):
        lead = t.shape[:-1]
        t = t.reshape(lead + (2, SSD_GROUPS, gh))
        t = jnp.moveaxis(t, -3, -2).reshape(lead + (SSD_GROUPS, 2 * gh))
        t = jnp.pad(t, [(0, 0)] * (len(lead) + 1) + [(0, LANES - 2 * gh)])
        return t.reshape(lead + (SSD_GROUPS * LANES,))

    w_in = w_in.astype(BF16)
    w = jnp.concatenate([w_in[:, :SSD_D_INNER], w_in[:, n_state:], w_in[:, SSD_D_INNER:SSD_CONV_DIM],
                         dt_blocks(w_in[:, SSD_CONV_DIM:n_state])], axis=1)
    return (w, conv_w, conv_b.reshape(1, -1), dt_blocks(dt_bias.reshape(-1)).reshape(1, -1),
            dt_blocks(a_log.reshape(-1)).reshape(1, -1), jnp.repeat(d_skip, SSD_HEAD_DIM).reshape(1, -1))


def _ssd(proj, conv_w, conv_b, dt_bias, a_log, d_skip, norm_w, n_ctx, n_lat, n_batch):
    rows = n_ctx + n_lat
    gd = SSD_GROUP_DIM
    xb = SSD_D_INNER // gd
    bb = 2 * SSD_D_INNER // LANES
    sec = lambda width, off: pl.BlockSpec((rows, width), lambda b, g: (b, off + g))
    par = lambda r, width, off: pl.BlockSpec((r, width), lambda b, g: (0, off + g))
    return pl.pallas_call(
        functools.partial(_ssd_kernel, n_ctx=n_ctx, n_lat=n_lat, tile=_scan_tile(rows, SSD_CHUNK)),
        grid=(n_batch, SSD_GROUPS),
        in_specs=[sec(gd, 0), sec(gd, xb), sec(LANES, bb), sec(LANES, bb + SSD_GROUPS),
                  sec(LANES, bb + 2 * SSD_GROUPS),
                  par(3, gd, 0), par(3, LANES, SSD_D_INNER // LANES), par(3, LANES, SSD_D_INNER // LANES + SSD_GROUPS),
                  par(1, gd, 0), par(1, LANES, SSD_D_INNER // LANES), par(1, LANES, SSD_D_INNER // LANES + SSD_GROUPS),
                  par(1, LANES, 0), par(1, LANES, 0), par(1, gd, 0), par(1, gd, 0)],
        out_specs=pl.BlockSpec((rows, gd), lambda b, g: (b, g)),
        out_shape=jax.ShapeDtypeStruct((proj.shape[0], SSD_D_INNER), BF16),
        scratch_shapes=[pltpu.VMEM((rows, gd), F32),
                        pltpu.VMEM((rows, SSD_STATE), BF16),
                        pltpu.VMEM((rows, SSD_STATE), BF16),
                        pltpu.VMEM((rows, LANES), F32),
                        pltpu.VMEM((rows, LANES), F32),
                        pltpu.VMEM((rows, gd), F32),
                        pltpu.VMEM((rows, gd), F32),
                        pltpu.VMEM((rows, gd), F32),
                        pltpu.VMEM((rows, gd), BF16),
                        pltpu.VMEM((rows, gd), BF16),
                        pltpu.VMEM((SSD_STATE, gd), F32),
                        pltpu.VMEM((SSD_STATE, gd), F32)],
        compiler_params=_params("parallel", "parallel"),
        name="ssd",
    )(proj, proj, proj, proj, proj, conv_w, conv_w, conv_w, conv_b, conv_b, conv_b,
      dt_bias, a_log, d_skip, norm_w.reshape(1, -1))


HG_OWN, HG_EARLIER, HG_LATER = 1.0, 2.0, 3.0


def _hgrn_fill_tables(tr, tri_ref, kind_ref, place_ref):
    ch, sub = HG_CHUNK, HG_SUB
    ri = lax.broadcasted_iota(jnp.int32, (tr, tr), 0)
    ci = lax.broadcasted_iota(jnp.int32, (tr, tr), 1)
    same = (ri // ch) == (ci // ch)
    tri_ref[0] = (same & (ri >= ci)).astype(BF16)
    tri_ref[1] = (same & (ri <= ci)).astype(BF16)
    rb, cb = ri // sub, ci // sub
    kind_ref[...] = jnp.where(rb == cb, HG_OWN,
                              jnp.where(same & (cb < rb), HG_EARLIER, jnp.where(same & (cb > rb), HG_LATER, 0.0)))
    src_key = lax.broadcasted_iota(jnp.int32, (sub * LANES, LANES), 0) // LANES
    dst_key = lax.broadcasted_iota(jnp.int32, (sub * LANES, LANES), 1) % sub
    place_ref[...] = (src_key == dst_key).astype(BF16)


def _hgrn_intra(t, tr, rev, tables, lg_ref, k_ref, q_ref, v_ref, o_ref, qst_ref, upd_ref, dec_ref):
    ch, sub = HG_CHUNK, HG_SUB
    npc, nb, nblk = tr // ch, ch // sub, tr // sub
    tri_ref, kind_ref, place_ref = tables
    r0 = pl.multiple_of(t * tr, tr)
    lg = lg_ref[pl.ds(r0, tr), :]
    k = k_ref[pl.ds(r0, tr), :]
    q = q_ref[pl.ds(r0, tr), :]
    v = v_ref[pl.ds(r0, tr), :]
    tri = tri_ref[1 if rev else 0]
    cum = _select_rows(tri, lg)

    last = [c * ch + (0 if rev else ch - 1) for c in range(npc)]
    qst_ref[pl.ds(r0, tr), :] = (q * jnp.exp2(cum)).astype(BF16)
    k_end = (k * jnp.exp2(_rows_from(cum, last, ch) - cum)).astype(BF16)
    zero_chunk = jnp.zeros((ch, LANES), BF16)
    k_end_by_chunk = jnp.concatenate(
        [jnp.concatenate([k_end[c2 * ch:(c2 + 1) * ch] if c2 == c else zero_chunk for c2 in range(npc)], axis=0)
         for c in range(npc)], axis=1)
    upd = lax.dot_general(v, k_end_by_chunk, _TN, preferred_element_type=F32)
    for c in range(npc):
        chunk = t * npc + c
        dec_ref[pl.ds(chunk, 1), :] = jnp.exp2(cum[last[c]:last[c] + 1, :])
        upd_ref[pl.ds(pl.multiple_of(chunk * LANES, LANES), LANES), :] = upd[:, c * LANES:(c + 1) * LANES]

    def own_edge(j):
        i = j % nb
        if rev:
            return None if i == nb - 1 else (j + 1) * sub
        return None if i == 0 else j * sub - 1

    qs = q * jnp.exp2(cum - _rows_from(cum, [own_edge(j) for j in range(nblk)], sub))
    weights = {}
    for i in (range(nb - 1) if rev else range(1, nb)):
        q_rows = jnp.concatenate([qs[(c * nb + i) * sub:(c * nb + i + 1) * sub] for c in range(npc)],
                                 axis=0).astype(BF16)
        pieces = []
        for c in range(npc):
            base = c * ch
            e = base + ((i + 1) * sub if rev else i * sub - 1)
            lo, hi = ((i + 1) * sub, ch) if rev else (0, i * sub)
            part = k[base + lo:base + hi] * jnp.exp2(cum[e:e + 1, :] - cum[base + lo:base + hi])
            pad = jnp.zeros((ch - (hi - lo), LANES), F32)
            pieces += [pad, part] if rev else [part, pad]
        weights[i] = lax.dot_general(q_rows, jnp.concatenate(pieces, axis=0).astype(BF16), _NT,
                                     preferred_element_type=F32)
    zero_rows = jnp.zeros((sub, tr), F32)
    a_off = jnp.concatenate([weights[j % nb][(j // nb) * sub:(j // nb + 1) * sub] if j % nb in weights
                             else zero_rows for j in range(nblk)], axis=0)

    row8 = lax.broadcasted_iota(jnp.int32, (sub, LANES), 0)
    prods = []
    for s in range(sub):
        idx = [j * sub + s for j in range(nblk)]
        unseen = jnp.where((row8 <= s) if rev else (row8 >= s), 0.0, NEG_INF)
        w = jnp.exp2(cum - _rows_from(cum, idx, sub) + jnp.concatenate([unseen] * nblk, axis=0))
        prods.append((q * _rows_from(k, idx, sub) * w).astype(BF16))
    a_own = jnp.dot(jnp.concatenate(prods, axis=1), place_ref[...], preferred_element_type=F32)
    kind = kind_ref[...]
    a = jnp.where(kind == HG_OWN, jnp.concatenate([a_own] * (tr // LANES), axis=1),
                  jnp.where(kind == (HG_LATER if rev else HG_EARLIER), a_off, 0.0))
    o_ref[pl.ds(r0, tr), :] = jnp.dot(a.astype(BF16), v, preferred_element_type=F32)


def _hgrn_state_step(c, st, upd_ref, dec_ref, sin_ref):
    rows = pl.ds(pl.multiple_of(c * LANES, LANES), LANES)
    sin_ref[rows, :] = st.astype(BF16)
    return st * dec_ref[pl.ds(c, 1), :] + upd_ref[rows, :]


def _hgrn_readout(c, qst_ref, sin_ref):
    ch = HG_CHUNK
    return lax.dot_general(qst_ref[pl.ds(pl.multiple_of(c * ch, ch), ch), :],
                           sin_ref[pl.ds(pl.multiple_of(c * LANES, LANES), LANES), :], _NT,
                           preferred_element_type=F32)


def _hgrn_kernel(ff_ref, fb_ref, v_ref, q_ref, gate_ref, lb_ref, nw_ref, out_ref,
                 lgf_ref, lgb_ref, kf_ref, kb_ref, qa_ref, o_ref, ob_ref, qstf_ref, qstb_ref,
                 updf_ref, updb_ref, decf_ref, decb_ref, sinf_ref, sinb_ref, tri_ref, kind_ref, place_ref,
                 *, n_ctx, n_lat, tile):
    tables = (tri_ref, kind_ref, place_ref)
    _hgrn_fill_tables(tile, *tables)
    for d, (f_ref, lg_ref, k_ref) in enumerate(((ff_ref, lgf_ref, kf_ref), (fb_ref, lgb_ref, kb_ref))):
        lb = lb_ref[d:d + 1, :]
        g = lb + (1.0 - lb) * jax.nn.sigmoid(f_ref[...].astype(F32))
        lg_ref[...] = jnp.log(g) * LOG2_E
        k_ref[...] = 1.0 - g
    qa_ref[...] = _silu(q_ref[...].astype(F32))

    fwd = (qstf_ref, updf_ref, decf_ref)
    bwd = (qstb_ref, updb_ref, decb_ref)
    fwd_scan = (updf_ref, decf_ref, sinf_ref)
    bwd_scan = (updb_ref, decb_ref, sinb_ref)

    def intra_step(t, carry):
        _hgrn_intra(t, tile, False, tables, lgf_ref, kf_ref, qa_ref, v_ref, o_ref, *fwd)
        _hgrn_intra(t, tile, True, tables, lgb_ref, kb_ref, qa_ref, v_ref, ob_ref, *bwd)
        return carry

    lax.fori_loop(0, (n_ctx + n_lat) // tile, intra_step, 0, unroll=TILE_UNROLL)

    n_c = n_ctx // HG_CHUNK
    n_l = n_lat // HG_CHUNK

    def ctx_step(s, carry):
        return (_hgrn_state_step(s, carry[0], *fwd_scan),
                _hgrn_state_step(n_c - 1 - s, carry[1], *bwd_scan))

    def lat_step(s, carry):
        return (_hgrn_state_step(n_c + s, carry[0], *fwd_scan),
                _hgrn_state_step(n_c + n_l - 1 - s, carry[1], *bwd_scan))

    zero = jnp.zeros((LANES, LANES), F32)
    carry = lax.fori_loop(0, n_c, ctx_step, (zero, zero), unroll=SCAN_UNROLL)
    lax.fori_loop(0, n_l, lat_step, carry, unroll=SCAN_UNROLL)

    def read_step(c, carry):
        rows = pl.ds(pl.multiple_of(c * HG_CHUNK, HG_CHUNK), HG_CHUNK)
        o_ref[rows, :] += _hgrn_readout(c, qstf_ref, sinf_ref) + _hgrn_readout(c, qstb_ref, sinb_ref)
        return carry

    lax.fori_loop(0, n_c + n_l, read_step, 0, unroll=READ_UNROLL)

    o = o_ref[...] + ob_ref[...]
    out_ref[...] = (_rms_rows(o) * nw_ref[...] * _silu(gate_ref[...].astype(F32))).astype(out_ref.dtype)


def _hgrn(proj, lower, norm_w, n_ctx, n_lat, n_batch):
    rows = n_ctx + n_lat
    sec = lambda off: pl.BlockSpec((rows, LANES), lambda b, h: (b, off * HG_HEADS + h))
    seq = lambda dt: pltpu.VMEM((rows, LANES), dt)
    n_chunks = rows // HG_CHUNK
    upd = pltpu.VMEM((n_chunks * LANES, LANES), F32)
    dec = pltpu.VMEM((-(-n_chunks // 8) * 8, LANES), F32)
    sin = pltpu.VMEM((n_chunks * LANES, LANES), BF16)
    tile = _scan_tile(rows, HG_CHUNK)
    tables = [pltpu.VMEM((2, tile, tile), BF16), pltpu.VMEM((tile, tile), F32),
              pltpu.VMEM((HG_SUB * LANES, LANES), BF16)]
    return pl.pallas_call(
        functools.partial(_hgrn_kernel, n_ctx=n_ctx, n_lat=n_lat, tile=tile),
        grid=(n_batch, HG_HEADS),
        in_specs=[sec(0), sec(1), sec(2), sec(3), sec(4),
                  pl.BlockSpec((2, LANES), lambda b, h: (0, h)),
                  pl.BlockSpec((1, LANES), lambda b, h: (0, h))],
        out_specs=pl.BlockSpec((rows, LANES), lambda b, h: (b, h)),
        out_shape=jax.ShapeDtypeStruct((proj.shape[0], HG_VAL), BF16),
        scratch_shapes=[seq(F32), seq(F32), seq(F32), seq(F32), seq(F32), seq(F32), seq(F32),
                        seq(BF16), seq(BF16), upd, upd, dec, dec, sin, sin] + tables,
        compiler_params=_params("parallel", "parallel"),
        name="hgrn2",
    )(proj, proj, proj, proj, proj, lower, norm_w.reshape(1, -1))


def _hgrn_lower_bounds(lb_raw):
    p = jax.nn.softmax(lb_raw.astype(F32), axis=1)
    return jnp.cumsum(p, axis=1) - p[:, :1]


def kernel(x, c, ctx, c_ctx, ada_w, ada_b, norm_g, ffn_w_in, ffn_w_out, attn_w_in, attn_w_out, attn_sink,
           ssd_w_in, ssd_conv_w, ssd_conv_b, ssd_dt_bias, ssd_a_log, ssd_d, ssd_norm_w, ssd_w_out,
           hgrn_w_in, hgrn_lb, hgrn_norm_w, hgrn_w_out):
    n_batch, n_lat, d = x.shape
    n_ctx = ctx.shape[1]
    rows = n_ctx + n_lat
    tm = _token_tile(n_ctx, n_ctx + n_lat)
    tm_lat = _token_tile(n_ctx, n_lat)

    r = jnp.concatenate([ctx, x], axis=1).reshape(n_batch * rows, d)
    n_cond = -(-(n_batch + 1) // 8) * 8
    c_rows = jnp.concatenate([c, c_ctx[None, :], jnp.zeros((n_cond - n_batch - 1, d), F32)], axis=0)
    mods = _ada_mods(c_rows, ada_w, ada_b)
    tables = _rope_tables(n_ctx, n_lat)
    lower = _hgrn_lower_bounds(hgrn_lb)

    for i in range(DEPTH):
        kind, j = i % N_MIXERS, i // N_MIXERS
        mod = mods[i]
        if kind == 0:
            proj = _modproj(r, mod, norm_g[i, 0], attn_w_in[j].astype(BF16), n_ctx, tm, n_batch)
            y = _attention(proj, attn_sink[j], tables, n_ctx, n_lat, n_batch)
            w_out = attn_w_out[j]
        elif kind == 1:
            w_in, conv_w, conv_b, dt_bias, a_log, d_skip = _ssd_layout(
                ssd_w_in[j], ssd_conv_w[j], ssd_conv_b[j], ssd_dt_bias[j], ssd_a_log[j], ssd_d[j])
            proj = _modproj(r, mod, norm_g[i, 0], w_in, n_ctx, tm, n_batch)
            y = _ssd(proj, conv_w, conv_b, dt_bias, a_log, d_skip, ssd_norm_w[j], n_ctx, n_lat, n_batch)
            w_out = ssd_w_out[j]
        else:
            proj = _modproj(r, mod, norm_g[i, 0], hgrn_w_in[j].astype(BF16), n_ctx, tm, n_batch)
            y = _hgrn(proj, lower[:, i], hgrn_norm_w[j], n_ctx, n_lat, n_batch)
            w_out = hgrn_w_out[j]
        ffn_w = (ffn_w_in[i].astype(BF16), ffn_w_out[i].astype(BF16))
        if i < DEPTH - 1:
            r = _outproj(y, w_out.astype(BF16), r, mod, norm_g[i, 1], n_ctx, tm, n_batch)
            r = _ffn(r, mod, norm_g[i, 2], norm_g[i, 3], *ffn_w, n_ctx, tm, n_batch)
        else:
            r = _outproj(y, w_out.astype(BF16), r, mod, norm_g[i, 1], n_ctx, tm_lat, n_batch, lat_rows=n_lat)
            r = _ffn(r, mod, norm_g[i, 2], norm_g[i, 3], *ffn_w, n_ctx, tm_lat, n_batch, has_ctx=False)
    return r.reshape(n_batch, n_lat, d)
```

```python
import functools
import math

import jax
import jax.numpy as jnp
from jax import lax
from jax.experimental import pallas as pl
from jax.experimental.pallas import tpu as pltpu

F32 = jnp.float32
BF16 = jnp.bfloat16

D_MODEL = 2048
DEPTH = 4
N_MIXERS = 3
GRID_W = 64
RMS_EPS = 1e-6
N_MOD = 6

ATT_HEADS = 32
ATT_KV_HEADS = 4
ATT_GQA = ATT_HEADS // ATT_KV_HEADS
ATT_HEAD_DIM = 64
ATT_WINDOW = 128
ATT_BLOCK = 128
ATT_Q_DIM = ATT_HEADS * ATT_HEAD_DIM
ATT_KV_DIM = ATT_KV_HEADS * ATT_HEAD_DIM
ROPE_THETA = 10000.0
ROPE_PAIRS = ATT_HEAD_DIM // 4
NEG_INF = -1e30
LOG2_E = 1.0 / math.log(2.0)

SSD_D_INNER = 2 * D_MODEL
SSD_HEAD_DIM = 64
SSD_HEADS = SSD_D_INNER // SSD_HEAD_DIM
SSD_GROUPS = 8
SSD_GROUP_HEADS = SSD_HEADS // SSD_GROUPS
SSD_GROUP_DIM = SSD_D_INNER // SSD_GROUPS
SSD_STATE = 128
SSD_GN = SSD_GROUPS * SSD_STATE
SSD_CONV_DIM = SSD_D_INNER + 2 * SSD_GN
SSD_CHUNK = 64

HG_EXPAND = 128
HG_HEADS = D_MODEL // HG_EXPAND
HG_KEY = HG_HEADS * HG_EXPAND
HG_VAL = D_MODEL
HG_CHUNK = 64
HG_SUB = 8

FFN_DIM = -(-8 * D_MODEL // (3 * 256)) * 256

LANES = 128
MXU_TILE = 256
VMEM_LIMIT_BYTES = 56 * 1024 * 1024
MAX_TOKEN_TILE = 768
MAX_PROJ_TILE = 2816
SCAN_UNROLL = 2
READ_UNROLL = 6
TILE_UNROLL = 3


def _params(*sem):
    return pltpu.CompilerParams(dimension_semantics=sem, vmem_limit_bytes=VMEM_LIMIT_BYTES)


def _token_tile(n_ctx, rows):
    best = n_ctx
    for mult in range(1, rows // n_ctx + 1):
        t = mult * n_ctx
        if rows % t == 0 and t <= MAX_TOKEN_TILE:
            best = t
    assert rows % best == 0
    return best


def _silu(t):
    h = 0.5 * t
    return h + h * jnp.tanh(h)


def _rms_rows(t):
    return t * lax.rsqrt(jnp.mean(t * t, axis=-1, keepdims=True) + RMS_EPS)


def _rows_from(t, idx, height):
    width = t.shape[1]
    return jnp.concatenate([jnp.zeros((height, width), t.dtype) if r is None
                            else jnp.broadcast_to(t[r:r + 1, :], (height, width)) for r in idx], axis=0)


_NT = (((1,), (1,)), ((), ()))
_TN = (((0,), (0,)), ((), ()))


def _scan_tile(rows, chunk):
    return next(t for t in (4 * chunk, 2 * chunk) if rows % t == 0 and t % LANES == 0)


def _ada_kernel(c_ref, w_ref, b_ref, o_ref):
    s = _silu(c_ref[...]).astype(BF16)
    o_ref[...] = jnp.dot(s, w_ref[...].astype(BF16), preferred_element_type=F32) + b_ref[...]


def _ada_mods(c_rows, ada_w, ada_b):
    r = c_rows.shape[0]
    n = ada_w.shape[-1]
    tn = 1024
    out = pl.pallas_call(
        _ada_kernel,
        grid=(DEPTH, n // tn),
        in_specs=[pl.BlockSpec((r, D_MODEL), lambda l, j: (0, 0)),
                  pl.BlockSpec((None, D_MODEL, tn), lambda l, j: (l, 0, j)),
                  pl.BlockSpec((None, 1, tn), lambda l, j: (l, 0, j))],
        out_specs=pl.BlockSpec((None, r, tn), lambda l, j: (l, 0, j)),
        out_shape=jax.ShapeDtypeStruct((DEPTH, r, n), F32),
        compiler_params=_params("parallel", "parallel"),
        name="ada_mods",
    )(c_rows, ada_w, ada_b.reshape(DEPTH, 1, n))
    return out.reshape(DEPTH, r, N_MOD, D_MODEL)


def _first_tile(tiles_per_batch, has_ctx):
    return (pl.program_id(0) % tiles_per_batch == 0) if has_ctx else None


def _mod_row(mod_ref, modc_ref, row, first, r0):
    m = mod_ref[row:row + 1, :]
    if r0 == 0 and first is not None:
        m = jnp.where(first, modc_ref[row:row + 1, :], m)
    return m


def _modulate_into(x_ref, u_ref, mod_ref, modc_ref, g_ref, shift_row, first, n_ctx):
    g = g_ref[...]
    for r0 in range(0, x_ref.shape[0], n_ctx):
        shift = _mod_row(mod_ref, modc_ref, shift_row, first, r0)
        scale = _mod_row(mod_ref, modc_ref, shift_row + 1, first, r0)
        x = x_ref[r0:r0 + n_ctx, :]
        u_ref[r0:r0 + n_ctx, :] = (_rms_rows(x) * (g * (1.0 + scale)) + shift).astype(u_ref.dtype)


def _gated_residual_into(o_ref, f_src, x_ref, mod_ref, modc_ref, g_ref, gate_row, first, n_ctx):
    g = g_ref[...]
    for r0 in range(0, x_ref.shape[0], n_ctx):
        gate = _mod_row(mod_ref, modc_ref, gate_row, first, r0)
        f = f_src[r0:r0 + n_ctx, :]
        o_ref[r0:r0 + n_ctx, :] = x_ref[r0:r0 + n_ctx, :] + gate * (_rms_rows(f) * g)


def _mod_specs(tiles_per_batch, n_batch):
    return [pl.BlockSpec((None, N_MOD, D_MODEL), lambda i, j: (i // tiles_per_batch, 0, 0)),
            pl.BlockSpec((None, N_MOD, D_MODEL), lambda i, j: (n_batch, 0, 0))]


def _modproj_kernel(x_ref, mod_ref, modc_ref, g_ref, w_ref, o_ref, u_ref, *, n_ctx, tiles_per_batch,
                    has_ctx=True):
    first = _first_tile(tiles_per_batch, has_ctx)

    @pl.when(pl.program_id(1) == 0)
    def _():
        _modulate_into(x_ref, u_ref, mod_ref, modc_ref, g_ref, 0, first, n_ctx)

    tn = o_ref.shape[1]
    cut = -(-tn // (2 * MXU_TILE)) * MXU_TILE
    for lo, hi in ((0, cut), (cut, tn)):
        if hi > lo:
            o_ref[:, lo:hi] = jnp.dot(u_ref[...], w_ref[:, lo:hi], preferred_element_type=F32).astype(o_ref.dtype)


def _modproj(x, mod, g, w, n_ctx, tm, n_batch):
    t, d = x.shape
    n = w.shape[1]
    tn = max(c for c in range(LANES, MAX_PROJ_TILE + 1, LANES) if n % c == 0)
    tpb = t // n_batch // tm
    return pl.pallas_call(
        functools.partial(_modproj_kernel, n_ctx=n_ctx, tiles_per_batch=tpb),
        grid=(t // tm, n // tn),
        in_specs=[pl.BlockSpec((tm, d), lambda i, j: (i, 0))] + _mod_specs(tpb, n_batch)
        + [pl.BlockSpec((1, d), lambda i, j: (0, 0)),
           pl.BlockSpec((d, tn), lambda i, j: (0, j))],
        out_specs=pl.BlockSpec((tm, tn), lambda i, j: (i, j)),
        out_shape=jax.ShapeDtypeStruct((t, n), BF16),
        scratch_shapes=[pltpu.VMEM((tm, d), BF16)],
        compiler_params=_params("parallel", "arbitrary"),
        name="modproj",
    )(x, mod, mod, g.reshape(1, d), w)


def _outproj_kernel(y_ref, w_ref, x_ref, mod_ref, modc_ref, g_ref, o_ref, *, n_ctx, tiles_per_batch, nk,
                    has_ctx):
    k = pl.program_id(1)
    first = _first_tile(tiles_per_batch, has_ctx)
    part = jnp.dot(y_ref[...], w_ref[...], preferred_element_type=F32)
    if nk == 1:
        _gated_residual_into(o_ref, part, x_ref, mod_ref, modc_ref, g_ref, 2, first, n_ctx)
        return

    @pl.when(k == 0)
    def _():
        o_ref[...] = part

    @pl.when(k > 0)
    def _():
        o_ref[...] += part

    @pl.when(k == nk - 1)
    def _():
        _gated_residual_into(o_ref, o_ref, x_ref, mod_ref, modc_ref, g_ref, 2, first, n_ctx)


def _outproj(y, w, x, mod, g, n_ctx, tm, n_batch, lat_rows=None):
    t, d = x.shape
    kdim = y.shape[1]
    tk = min(kdim, 2048)
    nk = kdim // tk
    rows = t // n_batch
    if lat_rows is None:
        tpb, t_out = rows // tm, t
        y_spec = pl.BlockSpec((tm, tk), lambda i, k: (i, k))
        x_spec = pl.BlockSpec((tm, d), lambda i, k: (i, 0))
    else:
        tpb, t_out = lat_rows // tm, n_batch * lat_rows
        assert rows % n_ctx == 0 and lat_rows % n_ctx == 0 and tm % n_ctx == 0
        first_row = lambda i: pl.multiple_of(
            ((i // tpb) * (rows // n_ctx) + (rows - lat_rows) // n_ctx + (i % tpb) * (tm // n_ctx)) * n_ctx, n_ctx)
        y_spec = pl.BlockSpec((pl.Element(tm), pl.Element(tk)), lambda i, k: (first_row(i), k * tk))
        x_spec = pl.BlockSpec((pl.Element(tm), pl.Element(d)), lambda i, k: (first_row(i), 0))
    return pl.pallas_call(
        functools.partial(_outproj_kernel, n_ctx=n_ctx, tiles_per_batch=tpb, nk=nk, has_ctx=lat_rows is None),
        grid=(t_out // tm, nk),
        in_specs=[y_spec, pl.BlockSpec((tk, d), lambda i, k: (k, 0)), x_spec] + _mod_specs(tpb, n_batch)
        + [pl.BlockSpec((1, d), lambda i, k: (0, 0))],
        out_specs=pl.BlockSpec((tm, d), lambda i, k: (i, 0)),
        out_shape=jax.ShapeDtypeStruct((t_out, d), F32),
        compiler_params=_params("parallel", "arbitrary"),
        name="outproj",
    )(y, w, x, mod, mod, g.reshape(1, d))


def _ffn_kernel(x_ref, mod_ref, modc_ref, g2_ref, g3_ref, wg_ref, wu_ref, wo_ref, o_ref, u_ref,
                *, n_ctx, tiles_per_batch, nf, has_ctx):
    j = pl.program_id(1)
    first = _first_tile(tiles_per_batch, has_ctx)

    def hidden_tile(assign):
        u = u_ref[...]
        gate = jnp.dot(u, wg_ref[...], preferred_element_type=F32)
        up = jnp.dot(u, wu_ref[...], preferred_element_type=F32)
        h = (_silu(gate) * up).astype(BF16)
        part = jnp.dot(h, wo_ref[...], preferred_element_type=F32)
        if assign:
            o_ref[...] = part
        else:
            o_ref[...] += part

    @pl.when(j == 0)
    def _():
        _modulate_into(x_ref, u_ref, mod_ref, modc_ref, g2_ref, 3, first, n_ctx)
        hidden_tile(True)

    @pl.when(j > 0)
    def _():
        hidden_tile(False)

    @pl.when(j == nf - 1)
    def _():
        _gated_residual_into(o_ref, o_ref, x_ref, mod_ref, modc_ref, g3_ref, 5, first, n_ctx)


def _ffn(x, mod, g2, g3, w_in, w_out, n_ctx, tm, n_batch, has_ctx=True):
    t, d = x.shape
    f = w_out.shape[0]
    tf = 512
    nf = f // tf
    tpb = t // n_batch // tm
    return pl.pallas_call(
        functools.partial(_ffn_kernel, n_ctx=n_ctx, tiles_per_batch=tpb, nf=nf, has_ctx=has_ctx),
        grid=(t // tm, nf),
        in_specs=[pl.BlockSpec((tm, d), lambda i, j: (i, 0))] + _mod_specs(tpb, n_batch)
        + [pl.BlockSpec((1, d), lambda i, j: (0, 0)),
           pl.BlockSpec((1, d), lambda i, j: (0, 0)),
           pl.BlockSpec((d, tf), lambda i, j: (0, j)),
           pl.BlockSpec((d, tf), lambda i, j: (0, nf + j)),
           pl.BlockSpec((tf, d), lambda i, j: (j, 0))],
        out_specs=pl.BlockSpec((tm, d), lambda i, j: (i, 0)),
        out_shape=jax.ShapeDtypeStruct((t, d), F32),
        scratch_shapes=[pltpu.VMEM((tm, d), BF16)],
        compiler_params=_params("parallel", "arbitrary"),
        name="ffn",
    )(x, mod, mod, g2.reshape(1, d), g3.reshape(1, d), w_in, w_in, w_out)


def _rope_tables(n_ctx, n_lat):
    pos = jnp.arange(n_lat)
    row = (pos // GRID_W).astype(F32)
    col = (pos % GRID_W).astype(F32)
    inv_freq = ROPE_THETA ** (-jnp.arange(ROPE_PAIRS, dtype=F32) / ROPE_PAIRS)
    ang_row = row[:, None] * inv_freq
    ang_col = col[:, None] * inv_freq
    zero = jnp.zeros_like(ang_row)
    cos = jnp.concatenate([jnp.cos(ang_row)] * 2 + [jnp.cos(ang_col)] * 2, axis=1)
    s_lo = jnp.concatenate([-jnp.sin(ang_row), zero, -jnp.sin(ang_col), zero], axis=1)
    s_hi = jnp.concatenate([zero, jnp.sin(ang_row), zero, jnp.sin(ang_col)], axis=1)

    def full(tab, ctx_val):
        tab = jnp.concatenate([jnp.full((n_ctx, ATT_HEAD_DIM), ctx_val, F32), tab], axis=0)
        return jnp.concatenate([tab, tab], axis=1)

    return full(cos, 1.0), full(s_lo, 0.0), full(s_hi, 0.0)


def _rope(x, c, s_lo, s_hi):
    return x * c + pltpu.roll(x, LANES - ROPE_PAIRS, 1) * s_lo + pltpu.roll(x, ROPE_PAIRS, 1) * s_hi


def _attn_kernel(sink_ref, pq_ref, pk_ref, pv_ref, cq_ref, slq_ref, shq_ref, ck_ref, slk_ref, shk_ref,
                 o_ref, kx_ref, vt_ref, qt_ref, ot_ref, *, n_ctx, n_lat):
    t = pl.program_id(1)
    blk = ATT_BLOCK
    n_blocks = (n_ctx + n_lat) // blk
    ctx_blocks = n_ctx // blk
    hd = ATT_HEAD_DIM

    @pl.when(t == 0)
    def _():
        for i in range(n_blocks):
            rs = slice(i * blk, (i + 1) * blk)
            for c0 in range(0, ATT_KV_DIM, LANES):
                k = pk_ref[rs, c0:c0 + LANES].astype(F32)
                kx_ref[i, :, c0:c0 + LANES] = _rope(k, ck_ref[rs, :], slk_ref[rs, :], shk_ref[rs, :]).astype(BF16)
                vt_ref[i, c0:c0 + LANES, :] = pv_ref[rs, c0:c0 + LANES].astype(F32).T.astype(BF16)
        kx_ref[n_blocks] = jnp.zeros((blk, ATT_KV_DIM), BF16)
        vt_ref[n_blocks] = jnp.zeros((ATT_KV_DIM, blk), BF16)

    scale = ATT_HEAD_DIM ** -0.5 * LOG2_E
    for c0 in range(0, ATT_Q_DIM, LANES):
        q = pq_ref[:, 2 * ATT_KV_DIM + c0:2 * ATT_KV_DIM + c0 + LANES].astype(F32)
        qt_ref[c0:c0 + LANES, :] = (_rope(q, cq_ref[...], slq_ref[...], shq_ref[...]) * scale).T.astype(BF16)

    n = t - ctx_blocks
    key = lax.broadcasted_iota(jnp.int32, (blk, 2 * LANES), 0)
    qry = lax.broadcasted_iota(jnp.int32, (blk, 2 * LANES), 1) % blk
    blocks = [(i, None) for i in range(ctx_blocks)] + [
        (jnp.maximum(ctx_blocks + n - 1, 0), (key >= qry) & (n >= 1)),
        (jnp.maximum(ctx_blocks + n, 0), (key >= 0) & (n >= 0)),
        (ctx_blocks + n + 1, (key <= qry) & (n >= 0) & (n + 1 < n_lat // blk))]
    upper = lax.broadcasted_iota(jnp.int32, (1, 2 * LANES), 1) >= LANES
    zero = jnp.zeros((hd, LANES), BF16)

    for pair in range(ATT_KV_HEADS // 2):
        ps = slice(pair * LANES, (pair + 1) * LANES)
        for g in range(ATT_GQA):
            h0 = (2 * pair) * ATT_GQA + g
            h1 = (2 * pair + 1) * ATT_GQA + g
            w = jnp.concatenate([jnp.concatenate([qt_ref[h0 * hd:(h0 + 1) * hd, :], zero], axis=0),
                                 jnp.concatenate([zero, qt_ref[h1 * hd:(h1 + 1) * hd, :]], axis=0)], axis=1)
            sink = jnp.where(upper, sink_ref[h1], sink_ref[h0]) * LOG2_E
            scores = []
            m = sink
            for i, visible in blocks:
                s = jnp.dot(kx_ref[i, :, ps], w, preferred_element_type=F32)
                if visible is not None:
                    s = jnp.where(visible, s, NEG_INF)
                scores.append(s)
                m = jnp.maximum(m, jnp.max(s, axis=0, keepdims=True))
            den = jnp.exp2(sink - m)
            acc = jnp.zeros((LANES, 2 * LANES), F32)
            for (i, _), s in zip(blocks, scores):
                p = jnp.exp2(s - m)
                den = den + jnp.sum(p, axis=0, keepdims=True)
                acc = acc + jnp.dot(vt_ref[i, ps, :], p.astype(BF16), preferred_element_type=F32)
            ot_ref[h0 * hd:(h0 + 1) * hd, :] = acc[0:hd, 0:LANES] / den[:, 0:LANES]
            ot_ref[h1 * hd:(h1 + 1) * hd, :] = acc[hd:2 * hd, LANES:2 * LANES] / den[:, LANES:2 * LANES]

    for c0 in range(0, ATT_Q_DIM, LANES):
        o_ref[:, c0:c0 + LANES] = ot_ref[c0:c0 + LANES, :].T.astype(o_ref.dtype)


def _attention(proj, sink, tables, n_ctx, n_lat, n_batch):
    rows = n_ctx + n_lat
    nq = rows // ATT_BLOCK
    cos, s_lo, s_hi = tables
    qtab = pl.BlockSpec((ATT_BLOCK, LANES), lambda b, t: (t, 0))
    ktab = pl.BlockSpec((rows, LANES), lambda b, t: (0, 0))
    return pl.pallas_call(
        functools.partial(_attn_kernel, n_ctx=n_ctx, n_lat=n_lat),
        grid=(n_batch, nq),
        in_specs=[pl.BlockSpec(memory_space=pltpu.SMEM),
                  pl.BlockSpec((ATT_BLOCK, proj.shape[1]), lambda b, t: (b * nq + t, 0)),
                  pl.BlockSpec((rows, ATT_KV_DIM), lambda b, t: (b, 0)),
                  pl.BlockSpec((rows, ATT_KV_DIM), lambda b, t: (b, 1)),
                  qtab, qtab, qtab, ktab, ktab, ktab],
        out_specs=pl.BlockSpec((ATT_BLOCK, ATT_Q_DIM), lambda b, t: (b * nq + t, 0)),
        out_shape=jax.ShapeDtypeStruct((proj.shape[0], ATT_Q_DIM), BF16),
        scratch_shapes=[pltpu.VMEM((nq + 1, ATT_BLOCK, ATT_KV_DIM), BF16),
                        pltpu.VMEM((nq + 1, ATT_KV_DIM, ATT_BLOCK), BF16),
                        pltpu.VMEM((ATT_Q_DIM, ATT_BLOCK), BF16),
                        pltpu.VMEM((ATT_Q_DIM, ATT_BLOCK), F32)],
        compiler_params=_params("parallel", "arbitrary"),
        name="attention",
    )(sink.astype(F32), proj, proj, proj, cos, s_lo, s_hi, cos, s_lo, s_hi)


def _softplus(t):
    return jnp.maximum(t, 0.0) + jnp.log(1.0 + jnp.exp(-jnp.abs(t)))


def _split_bf16(t):
    hi = t.astype(BF16)
    return hi, (t - hi.astype(F32)).astype(BF16)


def _select_rows(sel, t):
    both = jnp.dot(sel, jnp.concatenate(_split_bf16(t), axis=1), preferred_element_type=F32)
    return both[:, 0:LANES] + both[:, LANES:2 * LANES]


def _conv_silu_into(src_ref, w_ref, b_ref, dst_ref, n_ctx):
    rows, width = src_ref.shape
    t = n_ctx
    ri = lax.broadcasted_iota(jnp.int32, (t, t), 0)
    ci = lax.broadcasted_iota(jnp.int32, (t, t), 1)
    down = (ri == ci + 1).astype(BF16)
    up = (ri + 1 == ci).astype(BF16)
    row8 = lax.broadcasted_iota(jnp.int32, (8, width), 0)
    w = w_ref[...]
    b = b_ref[...]
    for r0 in range(0, rows, t):
        x = src_ref[r0:r0 + t, :]
        prv = jnp.dot(down, x, preferred_element_type=F32)
        nxt = jnp.dot(up, x, preferred_element_type=F32)
        if r0 not in (0, n_ctx):
            edge = jnp.broadcast_to(src_ref[r0 - 1:r0, :].astype(F32), (8, width))
            prv = jnp.concatenate([jnp.where(row8 == 0, edge, prv[0:8]), prv[8:]], axis=0)
        if r0 + t not in (n_ctx, rows):
            edge = jnp.broadcast_to(src_ref[r0 + t:r0 + t + 1, :].astype(F32), (8, width))
            nxt = jnp.concatenate([nxt[:t - 8], jnp.where(row8 == 7, edge, nxt[t - 8:])], axis=0)
        out = b + prv * w[0:1, :] + x.astype(F32) * w[1:2, :] + nxt * w[2:3, :]
        dst_ref[r0:r0 + t, :] = _silu(out).astype(dst_ref.dtype)


def _ssd_fill_tables(tr, tri_ref, expand_ref):
    q = SSD_CHUNK
    ri = lax.broadcasted_iota(jnp.int32, (tr, tr), 0)
    ci = lax.broadcasted_iota(jnp.int32, (tr, tr), 1)
    same = (ri // q) == (ci // q)
    tri_ref[0:tr, :] = (same & (ri >= ci)).astype(BF16)
    tri_ref[tr:2 * tr, :] = (same & (ri <= ci)).astype(BF16)
    src = lax.broadcasted_iota(jnp.int32, (2 * LANES, SSD_GROUP_DIM), 0) % LANES
    dst = lax.broadcasted_iota(jnp.int32, (2 * LANES, SSD_GROUP_DIM), 1) // SSD_HEAD_DIM
    for d in range(2):
        expand_ref[d] = (src == d * SSD_GROUP_HEADS + dst).astype(BF16)


def _ssd_decays(r0, tr, rev, xa, dtc, cum, expand, cumx_ref, xdec_ref):
    q = SSD_CHUNK
    npc = tr // q
    cum_hi, cum_lo = _split_bf16(cum)
    cum = cum_hi.astype(F32) + cum_lo.astype(F32)
    dt_hi, dt_lo = _split_bf16(dtc)
    wide = jnp.dot(jnp.concatenate([jnp.concatenate([cum_hi, cum_lo], axis=1),
                                    jnp.concatenate([dt_hi, dt_lo], axis=1)], axis=0),
                   expand, preferred_element_type=F32)
    cum_x = wide[0:tr]
    dt_x = wide[tr:2 * tr]
    tot_x = _rows_from(cum_x, [c * q + (0 if rev else q - 1) for c in range(npc)], q)
    xdt = xa * dt_x
    cumx_ref[pl.ds(r0, tr), :] = cum_x
    xdec_ref[pl.ds(r0, tr), :] = (xdt * jnp.exp2(tot_x - cum_x)).astype(BF16)
    return cum, cum_x, xdt


def _ssd_intra(t, tr, tables, xa_ref, ba_ref, ca_ref, dtv_ref, da_ref, y_ref, fwd_refs, bwd_refs):
    q = SSD_CHUNK
    assert 2 * q == LANES and 2 * SSD_HEAD_DIM == LANES
    tri_ref, expand_ref = tables
    r0 = pl.multiple_of(t * tr, tr)
    xa = xa_ref[pl.ds(r0, tr), :]
    bc = ba_ref[pl.ds(r0, tr), :]
    cc = ca_ref[pl.ds(r0, tr), :]
    dtc = dtv_ref[pl.ds(r0, tr), :]
    cums = _select_rows(tri_ref[...], da_ref[pl.ds(r0, tr), :])
    scans = [(False, 0) + _ssd_decays(r0, tr, False, xa, dtc, cums[0:tr], expand_ref[0], *fwd_refs),
             (True, SSD_GROUP_HEADS) + _ssd_decays(r0, tr, True, xa, dtc, cums[tr:2 * tr], expand_ref[1], *bwd_refs)]

    row = lax.broadcasted_iota(jnp.int32, (q, LANES), 0)
    lane = lax.broadcasted_iota(jnp.int32, (q, LANES), 1)
    upper = lane >= q
    key = jnp.where(upper, lane - q, lane)
    for c in range(tr // q):
        cs = slice(c * q, (c + 1) * q)
        cb2 = lax.dot_general(cc[cs], jnp.concatenate([bc[cs], bc[cs]], axis=0), _NT, preferred_element_type=F32)
        cum_rows = [jnp.concatenate([cum[cs], pltpu.roll(cum[cs], LANES - 1, 1)], axis=0).T
                    for _, _, cum, _, _ in scans]
        ys = []
        for pr in range(SSD_GROUP_HEADS // 2):
            sl = slice(pr * LANES, (pr + 1) * LANES)
            weights, values = [], []
            for (rev, lane0, _, cum_x, xdt), rows in zip(scans, cum_rows):
                le = lane0 + 2 * pr
                seen = (row <= key) if rev else (row >= key)
                decay = jnp.exp2(jnp.where(seen, cum_x[cs, sl] - rows[le:le + 1, :], NEG_INF))
                weights.append((cb2 * decay).astype(BF16))
                xp = xdt[cs, sl]
                values.append(jnp.concatenate([jnp.where(upper, 0.0, xp), jnp.where(upper, xp, 0.0)],
                                              axis=0).astype(BF16))
            ys.append(jnp.dot(jnp.concatenate(weights, axis=1), jnp.concatenate(values, axis=0),
                              preferred_element_type=F32))
        y_ref[pl.ds(r0 + c * q, q), :] = jnp.concatenate(ys, axis=1)


def _ssd_state_step(c, rev, ba_ref, ca_ref, y_ref, cumx_ref, xdec_ref, s_ref):
    q = SSD_CHUNK
    r0 = pl.multiple_of(c * q, q)
    cum_x = cumx_ref[pl.ds(r0, q), :]
    tot_x = cum_x[0:1, :] if rev else cum_x[q - 1:q, :]
    state = s_ref[...]
    y_ref[pl.ds(r0, q), :] += (jnp.dot(ca_ref[pl.ds(r0, q), :], state.astype(BF16), preferred_element_type=F32)
                               * jnp.exp2(cum_x))
    upd = lax.dot_general(ba_ref[pl.ds(r0, q), :], xdec_ref[pl.ds(r0, q), :], _TN, preferred_element_type=F32)
    s_ref[...] = state * jnp.exp2(tot_x) + upd


def _ssd_kernel(x_ref, z_ref, b_ref, c_ref, dt_ref, cwx_ref, cwb_ref, cwc_ref, cbx_ref, cbb_ref, cbc_ref,
                dtb_ref, alog_ref, dskip_ref, nw_ref, o_ref,
                xa_ref, ba_ref, ca_ref, dtv_ref, da_ref, y_ref, cumxf_ref, cumxb_ref, xdecf_ref, xdecb_ref,
                s_ref, sb_ref, tri_ref, expand_ref, *, n_ctx, n_lat, tile):
    _conv_silu_into(x_ref, cwx_ref, cbx_ref, xa_ref, n_ctx)
    _conv_silu_into(b_ref, cwb_ref, cbb_ref, ba_ref, n_ctx)
    _conv_silu_into(c_ref, cwc_ref, cbc_ref, ca_ref, n_ctx)
    dtv = _softplus(dt_ref[...].astype(F32) + dtb_ref[...])
    dtv_ref[...] = dtv
    da_ref[...] = dtv * (-jnp.exp(alog_ref[...]) * LOG2_E)

    seq = (xa_ref, ba_ref, ca_ref, dtv_ref, da_ref)
    fwd = (cumxf_ref, xdecf_ref)
    bwd = (cumxb_ref, xdecb_ref)

    tables = (tri_ref, expand_ref)
    _ssd_fill_tables(tile, *tables)

    def intra_step(t, carry):
        _ssd_intra(t, tile, tables, *seq, y_ref, fwd, bwd)
        return carry

    lax.fori_loop(0, (n_ctx + n_lat) // tile, intra_step, 0, unroll=TILE_UNROLL)

    n_c = n_ctx // SSD_CHUNK
    n_l = n_lat // SSD_CHUNK

    def ctx_step(s, carry):
        _ssd_state_step(s, False, ba_ref, ca_ref, y_ref, *fwd, s_ref)
        _ssd_state_step(n_c - 1 - s, True, ba_ref, ca_ref, y_ref, *bwd, sb_ref)
        return carry

    def lat_step(s, carry):
        _ssd_state_step(n_c + s, False, ba_ref, ca_ref, y_ref, *fwd, s_ref)
        _ssd_state_step(n_c + n_l - 1 - s, True, ba_ref, ca_ref, y_ref, *bwd, sb_ref)
        return carry

    s_ref[...] = jnp.zeros(s_ref.shape, F32)
    sb_ref[...] = jnp.zeros(sb_ref.shape, F32)
    lax.fori_loop(0, n_c, ctx_step, 0, unroll=SCAN_UNROLL)
    lax.fori_loop(0, n_l, lat_step, 0, unroll=SCAN_UNROLL)

    for r0 in range(0, n_ctx + n_lat, n_ctx):
        y = y_ref[r0:r0 + n_ctx, :] + xa_ref[r0:r0 + n_ctx, :] * dskip_ref[...]
        t = y * _silu(z_ref[r0:r0 + n_ctx, :].astype(F32))
        o_ref[r0:r0 + n_ctx, :] = (_rms_rows(t) * nw_ref[...]).astype(o_ref.dtype)


def _ssd_layout(w_in, conv_w, conv_b, dt_bias, a_log, d_skip):
    n_state = SSD_CONV_DIM + 2 * SSD_HEADS
    gh = SSD_GROUP_HEADS

    def dt_blocks(t):
        lead = t.shape[:-1]
        t = t.reshape(lead + (2, SSD_GROUPS, gh))
        t = jnp.moveaxis(t, -3, -2).reshape(lead + (SSD_GROUPS, 2 * gh))
        t = jnp.pad(t, [(0, 0)] * (len(lead) + 1) + [(0, LANES - 2 * gh)])
        return t.reshape(lead + (SSD_GROUPS * LANES,))

    w_in = w_in.astype(BF16)
    w = jnp.concatenate([w_in[:, :SSD_D_INNER], w_in[:, n_state:], w_in[:, SSD_D_INNER:SSD_CONV_DIM],
                         dt_blocks(w_in[:, SSD_CONV_DIM:n_state])], axis=1)
    return (w, conv_w, conv_b.reshape(1, -1), dt_blocks(dt_bias.reshape(-1)).reshape(1, -1),
            dt_blocks(a_log.reshape(-1)).reshape(1, -1), jnp.repeat(d_skip, SSD_HEAD_DIM).reshape(1, -1))


def _ssd(proj, conv_w, conv_b, dt_bias, a_log, d_skip, norm_w, n_ctx, n_lat, n_batch):
    rows = n_ctx + n_lat
    gd = SSD_GROUP_DIM
    tile = _scan_tile(rows, SSD_CHUNK)
    xb = SSD_D_INNER // gd
    bb = 2 * SSD_D_INNER // LANES
    sec = lambda width, off: pl.BlockSpec((rows, width), lambda b, g: (b, off + g))
    par = lambda r, width, off: pl.BlockSpec((r, width), lambda b, g: (0, off + g))
    return pl.pallas_call(
        functools.partial(_ssd_kernel, n_ctx=n_ctx, n_lat=n_lat, tile=tile),
        grid=(n_batch, SSD_GROUPS),
        in_specs=[sec(gd, 0), sec(gd, xb), sec(LANES, bb), sec(LANES, bb + SSD_GROUPS),
                  sec(LANES, bb + 2 * SSD_GROUPS),
                  par(3, gd, 0), par(3, LANES, SSD_D_INNER // LANES), par(3, LANES, SSD_D_INNER // LANES + SSD_GROUPS),
                  par(1, gd, 0), par(1, LANES, SSD_D_INNER // LANES), par(1, LANES, SSD_D_INNER // LANES + SSD_GROUPS),
                  par(1, LANES, 0), par(1, LANES, 0), par(1, gd, 0), par(1, gd, 0)],
        out_specs=pl.BlockSpec((rows, gd), lambda b, g: (b, g)),
        out_shape=jax.ShapeDtypeStruct((proj.shape[0], SSD_D_INNER), BF16),
        scratch_shapes=[pltpu.VMEM((rows, gd), F32),
                        pltpu.VMEM((rows, SSD_STATE), BF16),
                        pltpu.VMEM((rows, SSD_STATE), BF16),
                        pltpu.VMEM((rows, LANES), F32),
                        pltpu.VMEM((rows, LANES), F32),
                        pltpu.VMEM((rows, gd), F32),
                        pltpu.VMEM((rows, gd), F32),
                        pltpu.VMEM((rows, gd), F32),
                        pltpu.VMEM((rows, gd), BF16),
                        pltpu.VMEM((rows, gd), BF16),
                        pltpu.VMEM((SSD_STATE, gd), F32),
                        pltpu.VMEM((SSD_STATE, gd), F32),
                        pltpu.VMEM((2 * tile, tile), BF16),
                        pltpu.VMEM((2, 2 * LANES, gd), BF16)],
        compiler_params=_params("parallel", "parallel"),
        name="ssd",
    )(proj, proj, proj, proj, proj, conv_w, conv_w, conv_w, conv_b, conv_b, conv_b,
      dt_bias, a_log, d_skip, norm_w.reshape(1, -1))


HG_OWN, HG_EARLIER, HG_LATER = 1.0, 2.0, 3.0


def _hgrn_fill_tables(tr, tri_ref, kind_ref, place_ref):
    ch, sub = HG_CHUNK, HG_SUB
    ri = lax.broadcasted_iota(jnp.int32, (tr, tr), 0)
    ci = lax.broadcasted_iota(jnp.int32, (tr, tr), 1)
    same = (ri // ch) == (ci // ch)
    tri_ref[0] = (same & (ri >= ci)).astype(BF16)
    tri_ref[1] = (same & (ri <= ci)).astype(BF16)
    rb, cb = ri // sub, ci // sub
    kind_ref[...] = jnp.where(rb == cb, HG_OWN,
                              jnp.where(same & (cb < rb), HG_EARLIER, jnp.where(same & (cb > rb), HG_LATER, 0.0)))
    src_key = lax.broadcasted_iota(jnp.int32, (sub * LANES, LANES), 0) // LANES
    dst_key = lax.broadcasted_iota(jnp.int32, (sub * LANES, LANES), 1) % sub
    place_ref[...] = (src_key == dst_key).astype(BF16)


def _hgrn_intra(t, tr, rev, tables, lg_ref, k_ref, q_ref, v_ref, o_ref, qst_ref, upd_ref, dec_ref):
    ch, sub = HG_CHUNK, HG_SUB
    npc, nb, nblk = tr // ch, ch // sub, tr // sub
    tri_ref, kind_ref, place_ref = tables
    r0 = pl.multiple_of(t * tr, tr)
    lg = lg_ref[pl.ds(r0, tr), :]
    k = k_ref[pl.ds(r0, tr), :]
    q = q_ref[pl.ds(r0, tr), :]
    v = v_ref[pl.ds(r0, tr), :]
    tri = tri_ref[1 if rev else 0]
    cum = _select_rows(tri, lg)

    last = [c * ch + (0 if rev else ch - 1) for c in range(npc)]
    qst_ref[pl.ds(r0, tr), :] = (q * jnp.exp2(cum)).astype(BF16)
    k_end = (k * jnp.exp2(_rows_from(cum, last, ch) - cum)).astype(BF16)
    zero_chunk = jnp.zeros((ch, LANES), BF16)
    k_end_by_chunk = jnp.concatenate(
        [jnp.concatenate([k_end[c2 * ch:(c2 + 1) * ch] if c2 == c else zero_chunk for c2 in range(npc)], axis=0)
         for c in range(npc)], axis=1)
    upd = lax.dot_general(v, k_end_by_chunk, _TN, preferred_element_type=F32)
    for c in range(npc):
        chunk = t * npc + c
        dec_ref[pl.ds(chunk, 1), :] = jnp.exp2(cum[last[c]:last[c] + 1, :])
        upd_ref[pl.ds(pl.multiple_of(chunk * LANES, LANES), LANES), :] = upd[:, c * LANES:(c + 1) * LANES]

    def own_edge(j):
        i = j % nb
        if rev:
            return None if i == nb - 1 else (j + 1) * sub
        return None if i == 0 else j * sub - 1

    qs = q * jnp.exp2(cum - _rows_from(cum, [own_edge(j) for j in range(nblk)], sub))
    weights = {}
    for i in (range(nb - 1) if rev else range(1, nb)):
        q_rows = jnp.concatenate([qs[(c * nb + i) * sub:(c * nb + i + 1) * sub] for c in range(npc)],
                                 axis=0).astype(BF16)
        pieces = []
        for c in range(npc):
            base = c * ch
            e = base + ((i + 1) * sub if rev else i * sub - 1)
            lo, hi = ((i + 1) * sub, ch) if rev else (0, i * sub)
            part = k[base + lo:base + hi] * jnp.exp2(cum[e:e + 1, :] - cum[base + lo:base + hi])
            pad = jnp.zeros((ch - (hi - lo), LANES), F32)
            pieces += [pad, part] if rev else [part, pad]
        weights[i] = lax.dot_general(q_rows, jnp.concatenate(pieces, axis=0).astype(BF16), _NT,
                                     preferred_element_type=F32)
    zero_rows = jnp.zeros((sub, tr), F32)
    a_off = jnp.concatenate([weights[j % nb][(j // nb) * sub:(j // nb + 1) * sub] if j % nb in weights
                             else zero_rows for j in range(nblk)], axis=0)

    row8 = lax.broadcasted_iota(jnp.int32, (sub, LANES), 0)
    prods = []
    for s in range(sub):
        idx = [j * sub + s for j in range(nblk)]
        unseen = jnp.where((row8 <= s) if rev else (row8 >= s), 0.0, NEG_INF)
        w = jnp.exp2(cum - _rows_from(cum, idx, sub) + jnp.concatenate([unseen] * nblk, axis=0))
        prods.append((q * _rows_from(k, idx, sub) * w).astype(BF16))
    a_own = jnp.dot(jnp.concatenate(prods, axis=1), place_ref[...], preferred_element_type=F32)
    kind = kind_ref[...]
    a = jnp.where(kind == HG_OWN, jnp.concatenate([a_own] * (tr // LANES), axis=1),
                  jnp.where(kind == (HG_LATER if rev else HG_EARLIER), a_off, 0.0))
    o_ref[pl.ds(r0, tr), :] = jnp.dot(a.astype(BF16), v, preferred_element_type=F32)


def _hgrn_state_step(c, st, upd_ref, dec_ref, sin_ref):
    rows = pl.ds(pl.multiple_of(c * LANES, LANES), LANES)
    sin_ref[rows, :] = st.astype(BF16)
    return st * dec_ref[pl.ds(c, 1), :] + upd_ref[rows, :]


def _hgrn_readout(c, qst_ref, sin_ref):
    ch = HG_CHUNK
    return lax.dot_general(qst_ref[pl.ds(pl.multiple_of(c * ch, ch), ch), :],
                           sin_ref[pl.ds(pl.multiple_of(c * LANES, LANES), LANES), :], _NT,
                           preferred_element_type=F32)


def _hgrn_kernel(ff_ref, fb_ref, v_ref, q_ref, gate_ref, lb_ref, nw_ref, out_ref,
                 lgf_ref, lgb_ref, kf_ref, kb_ref, qa_ref, o_ref, ob_ref, qstf_ref, qstb_ref,
                 updf_ref, updb_ref, decf_ref, decb_ref, sinf_ref, sinb_ref, tri_ref, kind_ref, place_ref,
                 *, n_ctx, n_lat, tile):
    tables = (tri_ref, kind_ref, place_ref)
    _hgrn_fill_tables(tile, *tables)
    for d, (f_ref, lg_ref, k_ref) in enumerate(((ff_ref, lgf_ref, kf_ref), (fb_ref, lgb_ref, kb_ref))):
        lb = lb_ref[d:d + 1, :]
        g = lb + (1.0 - lb) * jax.nn.sigmoid(f_ref[...].astype(F32))
        lg_ref[...] = jnp.log(g) * LOG2_E
        k_ref[...] = 1.0 - g
    qa_ref[...] = _silu(q_ref[...].astype(F32))

    fwd = (qstf_ref, updf_ref, decf_ref)
    bwd = (qstb_ref, updb_ref, decb_ref)
    fwd_scan = (updf_ref, decf_ref, sinf_ref)
    bwd_scan = (updb_ref, decb_ref, sinb_ref)

    def intra_step(t, carry):
        _hgrn_intra(t, tile, False, tables, lgf_ref, kf_ref, qa_ref, v_ref, o_ref, *fwd)
        _hgrn_intra(t, tile, True, tables, lgb_ref, kb_ref, qa_ref, v_ref, ob_ref, *bwd)
        return carry

    lax.fori_loop(0, (n_ctx + n_lat) // tile, intra_step, 0, unroll=TILE_UNROLL)

    n_c = n_ctx // HG_CHUNK
    n_l = n_lat // HG_CHUNK

    def ctx_step(s, carry):
        return (_hgrn_state_step(s, carry[0], *fwd_scan),
                _hgrn_state_step(n_c - 1 - s, carry[1], *bwd_scan))

    def lat_step(s, carry):
        return (_hgrn_state_step(n_c + s, carry[0], *fwd_scan),
                _hgrn_state_step(n_c + n_l - 1 - s, carry[1], *bwd_scan))

    zero = jnp.zeros((LANES, LANES), F32)
    carry = lax.fori_loop(0, n_c, ctx_step, (zero, zero), unroll=SCAN_UNROLL)
    lax.fori_loop(0, n_l, lat_step, carry, unroll=SCAN_UNROLL)

    def read_step(c, carry):
        rows = pl.ds(pl.multiple_of(c * HG_CHUNK, HG_CHUNK), HG_CHUNK)
        o_ref[rows, :] += _hgrn_readout(c, qstf_ref, sinf_ref) + _hgrn_readout(c, qstb_ref, sinb_ref)
        return carry

    lax.fori_loop(0, n_c + n_l, read_step, 0, unroll=READ_UNROLL)

    o = o_ref[...] + ob_ref[...]
    out_ref[...] = (_rms_rows(o) * nw_ref[...] * _silu(gate_ref[...].astype(F32))).astype(out_ref.dtype)


def _hgrn(proj, lower, norm_w, n_ctx, n_lat, n_batch):
    rows = n_ctx + n_lat
    sec = lambda off: pl.BlockSpec((rows, LANES), lambda b, h: (b, off * HG_HEADS + h))
    seq = lambda dt: pltpu.VMEM((rows, LANES), dt)
    n_chunks = rows // HG_CHUNK
    upd = pltpu.VMEM((n_chunks * LANES, LANES), F32)
    dec = pltpu.VMEM((-(-n_chunks // 8) * 8, LANES), F32)
    sin = pltpu.VMEM((n_chunks * LANES, LANES), BF16)
    tile = _scan_tile(rows, HG_CHUNK)
    tables = [pltpu.VMEM((2, tile, tile), BF16), pltpu.VMEM((tile, tile), F32),
              pltpu.VMEM((HG_SUB * LANES, LANES), BF16)]
    return pl.pallas_call(
        functools.partial(_hgrn_kernel, n_ctx=n_ctx, n_lat=n_lat, tile=tile),
        grid=(n_batch, HG_HEADS),
        in_specs=[sec(0), sec(1), sec(2), sec(3), sec(4),
                  pl.BlockSpec((2, LANES), lambda b, h: (0, h)),
                  pl.BlockSpec((1, LANES), lambda b, h: (0, h))],
        out_specs=pl.BlockSpec((rows, LANES), lambda b, h: (b, h)),
        out_shape=jax.ShapeDtypeStruct((proj.shape[0], HG_VAL), BF16),
        scratch_shapes=[seq(F32), seq(F32), seq(F32), seq(F32), seq(F32), seq(F32), seq(F32),
                        seq(BF16), seq(BF16), upd, upd, dec, dec, sin, sin] + tables,
        compiler_params=_params("parallel", "parallel"),
        name="hgrn2",
    )(proj, proj, proj, proj, proj, lower, norm_w.reshape(1, -1))


def _hgrn_lower_bounds(lb_raw):
    p = jax.nn.softmax(lb_raw.astype(F32), axis=1)
    return jnp.cumsum(p, axis=1) - p[:, :1]


def kernel(x, c, ctx, c_ctx, ada_w, ada_b, norm_g, ffn_w_in, ffn_w_out, attn_w_in, attn_w_out, attn_sink,
           ssd_w_in, ssd_conv_w, ssd_conv_b, ssd_dt_bias, ssd_a_log, ssd_d, ssd_norm_w, ssd_w_out,
           hgrn_w_in, hgrn_lb, hgrn_norm_w, hgrn_w_out):
    n_batch, n_lat, d = x.shape
    n_ctx = ctx.shape[1]
    rows = n_ctx + n_lat
    tm = _token_tile(n_ctx, n_ctx + n_lat)
    tm_lat = _token_tile(n_ctx, n_lat)

    r = jnp.concatenate([ctx, x], axis=1).reshape(n_batch * rows, d)
    n_cond = -(-(n_batch + 1) // 8) * 8
    c_rows = jnp.concatenate([c, c_ctx[None, :], jnp.zeros((n_cond - n_batch - 1, d), F32)], axis=0)
    mods = _ada_mods(c_rows, ada_w, ada_b)
    tables = _rope_tables(n_ctx, n_lat)
    lower = _hgrn_lower_bounds(hgrn_lb)

    for i in range(DEPTH):
        kind, j = i % N_MIXERS, i // N_MIXERS
        mod = mods[i]
        if kind == 0:
            proj = _modproj(r, mod, norm_g[i, 0], attn_w_in[j].astype(BF16), n_ctx, tm, n_batch)
            y = _attention(proj, attn_sink[j], tables, n_ctx, n_lat, n_batch)
            w_out = attn_w_out[j]
        elif kind == 1:
            w_in, conv_w, conv_b, dt_bias, a_log, d_skip = _ssd_layout(
                ssd_w_in[j], ssd_conv_w[j], ssd_conv_b[j], ssd_dt_bias[j], ssd_a_log[j], ssd_d[j])
            proj = _modproj(r, mod, norm_g[i, 0], w_in, n_ctx, tm, n_batch)
            y = _ssd(proj, conv_w, conv_b, dt_bias, a_log, d_skip, ssd_norm_w[j], n_ctx, n_lat, n_batch)
            w_out = ssd_w_out[j]
        else:
            proj = _modproj(r, mod, norm_g[i, 0], hgrn_w_in[j].astype(BF16), n_ctx, tm, n_batch)
            y = _hgrn(proj, lower[:, i], hgrn_norm_w[j], n_ctx, n_lat, n_batch)
            w_out = hgrn_w_out[j]
        ffn_w = (ffn_w_in[i].astype(BF16), ffn_w_out[i].astype(BF16))
        if i < DEPTH - 1:
            r = _outproj(y, w_out.astype(BF16), r, mod, norm_g[i, 1], n_ctx, tm, n_batch)
            r = _ffn(r, mod, norm_g[i, 2], norm_g[i, 3], *ffn_w, n_ctx, tm, n_batch)
        else:
            r = _outproj(y, w_out.astype(BF16), r, mod, norm_g[i, 1], n_ctx, tm_lat, n_batch, lat_rows=n_lat)
            r = _ffn(r, mod, norm_g[i, 2], norm_g[i, 3], *ffn_w, n_ctx, tm_lat, n_batch, has_ctx=False)
    return r.reshape(n_batch, n_lat, d)
```

```python
import functools
import math

import jax
import jax.numpy as jnp
from jax import lax
from jax.experimental import pallas as pl
from jax.experimental.pallas import tpu as pltpu

F32 = jnp.float32
BF16 = jnp.bfloat16

D_MODEL = 2048
DEPTH = 4
N_MIXERS = 3
GRID_W = 64
RMS_EPS = 1e-6
N_MOD = 6

ATT_HEADS = 32
ATT_KV_HEADS = 4
ATT_GQA = ATT_HEADS // ATT_KV_HEADS
ATT_HEAD_DIM = 64
ATT_WINDOW = 128
ATT_BLOCK = 128
ATT_Q_DIM = ATT_HEADS * ATT_HEAD_DIM
ATT_KV_DIM = ATT_KV_HEADS * ATT_HEAD_DIM
ROPE_THETA = 10000.0
ROPE_PAIRS = ATT_HEAD_DIM // 4
NEG_INF = -1e30
LOG2_E = 1.0 / math.log(2.0)

SSD_D_INNER = 2 * D_MODEL
SSD_HEAD_DIM = 64
SSD_HEADS = SSD_D_INNER // SSD_HEAD_DIM
SSD_GROUPS = 8
SSD_GROUP_HEADS = SSD_HEADS // SSD_GROUPS
SSD_GROUP_DIM = SSD_D_INNER // SSD_GROUPS
SSD_STATE = 128
SSD_GN = SSD_GROUPS * SSD_STATE
SSD_CONV_DIM = SSD_D_INNER + 2 * SSD_GN
SSD_CHUNK = 64

HG_EXPAND = 128
HG_HEADS = D_MODEL // HG_EXPAND
HG_KEY = HG_HEADS * HG_EXPAND
HG_VAL = D_MODEL
HG_CHUNK = 64
HG_SUB = 8

FFN_DIM = -(-8 * D_MODEL // (3 * 256)) * 256

LANES = 128
MXU_TILE = 256
VMEM_LIMIT_BYTES = 56 * 1024 * 1024
MAX_TOKEN_TILE = 768
MAX_PROJ_TILE = 2816
SCAN_UNROLL = 2
READ_UNROLL = 6
TILE_UNROLL = 3


def _params(*sem):
    return pltpu.CompilerParams(dimension_semantics=sem, vmem_limit_bytes=VMEM_LIMIT_BYTES)


def _token_tile(n_ctx, rows):
    best = n_ctx
    for mult in range(1, rows // n_ctx + 1):
        t = mult * n_ctx
        if rows % t == 0 and t <= MAX_TOKEN_TILE:
            best = t
    assert rows % best == 0
    return best


def _silu(t):
    h = 0.5 * t
    return h + h * jnp.tanh(h)


def _rms_rows(t):
    return t * lax.rsqrt(jnp.mean(t * t, axis=-1, keepdims=True) + RMS_EPS)


def _rows_from(t, idx, height):
    width = t.shape[1]
    return jnp.concatenate([jnp.zeros((height, width), t.dtype) if r is None
                            else jnp.broadcast_to(t[r:r + 1, :], (height, width)) for r in idx], axis=0)


_NT = (((1,), (1,)), ((), ()))
_TN = (((0,), (0,)), ((), ()))


def _scan_tile(rows, chunk):
    return next(t for t in (4 * chunk, 2 * chunk) if rows % t == 0 and t % LANES == 0)


def _ada_kernel(c_ref, w_ref, b_ref, o_ref):
    s = _silu(c_ref[...]).astype(BF16)
    o_ref[...] = jnp.dot(s, w_ref[...].astype(BF16), preferred_element_type=F32) + b_ref[...]


def _ada_mods(c_rows, ada_w, ada_b):
    r = c_rows.shape[0]
    n = ada_w.shape[-1]
    tn = 1024
    out = pl.pallas_call(
        _ada_kernel,
        grid=(DEPTH, n // tn),
        in_specs=[pl.BlockSpec((r, D_MODEL), lambda l, j: (0, 0)),
                  pl.BlockSpec((None, D_MODEL, tn), lambda l, j: (l, 0, j)),
                  pl.BlockSpec((None, 1, tn), lambda l, j: (l, 0, j))],
        out_specs=pl.BlockSpec((None, r, tn), lambda l, j: (l, 0, j)),
        out_shape=jax.ShapeDtypeStruct((DEPTH, r, n), F32),
        compiler_params=_params("parallel", "parallel"),
        name="ada_mods",
    )(c_rows, ada_w, ada_b.reshape(DEPTH, 1, n))
    return out.reshape(DEPTH, r, N_MOD, D_MODEL)


def _first_tile(tiles_per_batch, has_ctx):
    return (pl.program_id(0) % tiles_per_batch == 0) if has_ctx else None


def _mod_row(mod_ref, modc_ref, row, first, r0):
    m = mod_ref[row:row + 1, :]
    if r0 == 0 and first is not None:
        m = jnp.where(first, modc_ref[row:row + 1, :], m)
    return m


def _modulate_into(x_ref, u_ref, mod_ref, modc_ref, g_ref, shift_row, first, n_ctx):
    g = g_ref[...]
    for r0 in range(0, x_ref.shape[0], n_ctx):
        shift = _mod_row(mod_ref, modc_ref, shift_row, first, r0)
        scale = _mod_row(mod_ref, modc_ref, shift_row + 1, first, r0)
        x = x_ref[r0:r0 + n_ctx, :]
        u_ref[r0:r0 + n_ctx, :] = (_rms_rows(x) * (g * (1.0 + scale)) + shift).astype(u_ref.dtype)


def _gated_residual_into(o_ref, f_src, x_ref, mod_ref, modc_ref, g_ref, gate_row, first, n_ctx):
    g = g_ref[...]
    for r0 in range(0, x_ref.shape[0], n_ctx):
        gate = _mod_row(mod_ref, modc_ref, gate_row, first, r0)
        f = f_src[r0:r0 + n_ctx, :]
        o_ref[r0:r0 + n_ctx, :] = x_ref[r0:r0 + n_ctx, :] + gate * (_rms_rows(f) * g)


def _mod_specs(tiles_per_batch, n_batch):
    return [pl.BlockSpec((None, N_MOD, D_MODEL), lambda i, j: (i // tiles_per_batch, 0, 0)),
            pl.BlockSpec((None, N_MOD, D_MODEL), lambda i, j: (n_batch, 0, 0))]


def _modproj_kernel(x_ref, mod_ref, modc_ref, g_ref, w_ref, o_ref, u_ref, *, n_ctx, tiles_per_batch,
                    has_ctx=True):
    first = _first_tile(tiles_per_batch, has_ctx)

    @pl.when(pl.program_id(1) == 0)
    def _():
        _modulate_into(x_ref, u_ref, mod_ref, modc_ref, g_ref, 0, first, n_ctx)

    tn = o_ref.shape[1]
    cut = -(-tn // (2 * MXU_TILE)) * MXU_TILE
    for lo, hi in ((0, cut), (cut, tn)):
        if hi > lo:
            o_ref[:, lo:hi] = jnp.dot(u_ref[...], w_ref[:, lo:hi], preferred_element_type=F32).astype(o_ref.dtype)


def _modproj(x, mod, g, w, n_ctx, tm, n_batch):
    t, d = x.shape
    n = w.shape[1]
    tn = max(c for c in range(LANES, MAX_PROJ_TILE + 1, LANES) if n % c == 0)
    tpb = t // n_batch // tm
    return pl.pallas_call(
        functools.partial(_modproj_kernel, n_ctx=n_ctx, tiles_per_batch=tpb),
        grid=(t // tm, n // tn),
        in_specs=[pl.BlockSpec((tm, d), lambda i, j: (i, 0))] + _mod_specs(tpb, n_batch)
        + [pl.BlockSpec((1, d), lambda i, j: (0, 0)),
           pl.BlockSpec((d, tn), lambda i, j: (0, j))],
        out_specs=pl.BlockSpec((tm, tn), lambda i, j: (i, j)),
        out_shape=jax.ShapeDtypeStruct((t, n), BF16),
        scratch_shapes=[pltpu.VMEM((tm, d), BF16)],
        compiler_params=_params("parallel", "arbitrary"),
        name="modproj",
    )(x, mod, mod, g.reshape(1, d), w)


def _outproj_kernel(y_ref, w_ref, x_ref, mod_ref, modc_ref, g_ref, o_ref, *, n_ctx, tiles_per_batch, nk,
                    has_ctx):
    k = pl.program_id(1)
    first = _first_tile(tiles_per_batch, has_ctx)
    part = jnp.dot(y_ref[...], w_ref[...], preferred_element_type=F32)
    if nk == 1:
        _gated_residual_into(o_ref, part, x_ref, mod_ref, modc_ref, g_ref, 2, first, n_ctx)
        return

    @pl.when(k == 0)
    def _():
        o_ref[...] = part

    @pl.when(k > 0)
    def _():
        o_ref[...] += part

    @pl.when(k == nk - 1)
    def _():
        _gated_residual_into(o_ref, o_ref, x_ref, mod_ref, modc_ref, g_ref, 2, first, n_ctx)


def _outproj(y, w, x, mod, g, n_ctx, tm, n_batch, lat_rows=None):
    t, d = x.shape
    kdim = y.shape[1]
    tk = min(kdim, 2048)
    nk = kdim // tk
    rows = t // n_batch
    if lat_rows is None:
        tpb, t_out = rows // tm, t
        y_spec = pl.BlockSpec((tm, tk), lambda i, k: (i, k))
        x_spec = pl.BlockSpec((tm, d), lambda i, k: (i, 0))
    else:
        tpb, t_out = lat_rows // tm, n_batch * lat_rows
        assert rows % n_ctx == 0 and lat_rows % n_ctx == 0 and tm % n_ctx == 0
        first_row = lambda i: pl.multiple_of(
            ((i // tpb) * (rows // n_ctx) + (rows - lat_rows) // n_ctx + (i % tpb) * (tm // n_ctx)) * n_ctx, n_ctx)
        y_spec = pl.BlockSpec((pl.Element(tm), pl.Element(tk)), lambda i, k: (first_row(i), k * tk))
        x_spec = pl.BlockSpec((pl.Element(tm), pl.Element(d)), lambda i, k: (first_row(i), 0))
    return pl.pallas_call(
        functools.partial(_outproj_kernel, n_ctx=n_ctx, tiles_per_batch=tpb, nk=nk, has_ctx=lat_rows is None),
        grid=(t_out // tm, nk),
        in_specs=[y_spec, pl.BlockSpec((tk, d), lambda i, k: (k, 0)), x_spec] + _mod_specs(tpb, n_batch)
        + [pl.BlockSpec((1, d), lambda i, k: (0, 0))],
        out_specs=pl.BlockSpec((tm, d), lambda i, k: (i, 0)),
        out_shape=jax.ShapeDtypeStruct((t_out, d), F32),
        compiler_params=_params("parallel", "arbitrary"),
        name="outproj",
    )(y, w, x, mod, mod, g.reshape(1, d))


def _ffn_kernel(x_ref, mod_ref, modc_ref, g2_ref, g3_ref, wg_ref, wu_ref, wo_ref, o_ref, u_ref,
                *, n_ctx, tiles_per_batch, nf, has_ctx):
    j = pl.program_id(1)
    first = _first_tile(tiles_per_batch, has_ctx)

    def hidden_tile(assign):
        u = u_ref[...]
        gate = jnp.dot(u, wg_ref[...], preferred_element_type=F32)
        up = jnp.dot(u, wu_ref[...], preferred_element_type=F32)
        h = (_silu(gate) * up).astype(BF16)
        part = jnp.dot(h, wo_ref[...], preferred_element_type=F32)
        if assign:
            o_ref[...] = part
        else:
            o_ref[...] += part

    @pl.when(j == 0)
    def _():
        _modulate_into(x_ref, u_ref, mod_ref, modc_ref, g2_ref, 3, first, n_ctx)
        hidden_tile(True)

    @pl.when(j > 0)
    def _():
        hidden_tile(False)

    @pl.when(j == nf - 1)
    def _():
        _gated_residual_into(o_ref, o_ref, x_ref, mod_ref, modc_ref, g3_ref, 5, first, n_ctx)


def _ffn(x, mod, g2, g3, w_in, w_out, n_ctx, tm, n_batch, has_ctx=True):
    t, d = x.shape
    f = w_out.shape[0]
    tf = 512
    nf = f // tf
    tpb = t // n_batch // tm
    return pl.pallas_call(
        functools.partial(_ffn_kernel, n_ctx=n_ctx, tiles_per_batch=tpb, nf=nf, has_ctx=has_ctx),
        grid=(t // tm, nf),
        in_specs=[pl.BlockSpec((tm, d), lambda i, j: (i, 0))] + _mod_specs(tpb, n_batch)
        + [pl.BlockSpec((1, d), lambda i, j: (0, 0)),
           pl.BlockSpec((1, d), lambda i, j: (0, 0)),
           pl.BlockSpec((d, tf), lambda i, j: (0, j)),
           pl.BlockSpec((d, tf), lambda i, j: (0, nf + j)),
           pl.BlockSpec((tf, d), lambda i, j: (j, 0))],
        out_specs=pl.BlockSpec((tm, d), lambda i, j: (i, 0)),
        out_shape=jax.ShapeDtypeStruct((t, d), F32),
        scratch_shapes=[pltpu.VMEM((tm, d), BF16)],
        compiler_params=_params("parallel", "arbitrary"),
        name="ffn",
    )(x, mod, mod, g2.reshape(1, d), g3.reshape(1, d), w_in, w_in, w_out)


def _rope_tables(n_ctx, n_lat):
    pos = jnp.arange(n_lat)
    row = (pos // GRID_W).astype(F32)
    col = (pos % GRID_W).astype(F32)
    inv_freq = ROPE_THETA ** (-jnp.arange(ROPE_PAIRS, dtype=F32) / ROPE_PAIRS)
    ang_row = row[:, None] * inv_freq
    ang_col = col[:, None] * inv_freq
    zero = jnp.zeros_like(ang_row)
    cos = jnp.concatenate([jnp.cos(ang_row)] * 2 + [jnp.cos(ang_col)] * 2, axis=1)
    s_lo = jnp.concatenate([-jnp.sin(ang_row), zero, -jnp.sin(ang_col), zero], axis=1)
    s_hi = jnp.concatenate([zero, jnp.sin(ang_row), zero, jnp.sin(ang_col)], axis=1)

    def full(tab, ctx_val):
        tab = jnp.concatenate([jnp.full((n_ctx, ATT_HEAD_DIM), ctx_val, F32), tab], axis=0)
        return jnp.concatenate([tab, tab], axis=1)

    return full(cos, 1.0), full(s_lo, 0.0), full(s_hi, 0.0)


def _rope(x, c, s_lo, s_hi):
    return x * c + pltpu.roll(x, LANES - ROPE_PAIRS, 1) * s_lo + pltpu.roll(x, ROPE_PAIRS, 1) * s_hi


def _attn_kernel(sink_ref, pq_ref, pk_ref, pv_ref, cq_ref, slq_ref, shq_ref, ck_ref, slk_ref, shk_ref,
                 o_ref, kx_ref, vt_ref, qt_ref, ot_ref, *, n_ctx, n_lat):
    t = pl.program_id(1)
    blk = ATT_BLOCK
    n_blocks = (n_ctx + n_lat) // blk
    ctx_blocks = n_ctx // blk
    hd = ATT_HEAD_DIM

    @pl.when(t == 0)
    def _():
        for i in range(n_blocks):
            rs = slice(i * blk, (i + 1) * blk)
            for c0 in range(0, ATT_KV_DIM, LANES):
                k = pk_ref[rs, c0:c0 + LANES].astype(F32)
                kx_ref[i, :, c0:c0 + LANES] = _rope(k, ck_ref[rs, :], slk_ref[rs, :], shk_ref[rs, :]).astype(BF16)
                vt_ref[i, c0:c0 + LANES, :] = pv_ref[rs, c0:c0 + LANES].astype(F32).T.astype(BF16)
        kx_ref[n_blocks] = jnp.zeros((blk, ATT_KV_DIM), BF16)
        vt_ref[n_blocks] = jnp.zeros((ATT_KV_DIM, blk), BF16)

    scale = ATT_HEAD_DIM ** -0.5 * LOG2_E
    for c0 in range(0, ATT_Q_DIM, LANES):
        q = pq_ref[:, 2 * ATT_KV_DIM + c0:2 * ATT_KV_DIM + c0 + LANES].astype(F32)
        qt_ref[c0:c0 + LANES, :] = (_rope(q, cq_ref[...], slq_ref[...], shq_ref[...]) * scale).T.astype(BF16)

    n = t - ctx_blocks
    key = lax.broadcasted_iota(jnp.int32, (blk, 2 * LANES), 0)
    qry = lax.broadcasted_iota(jnp.int32, (blk, 2 * LANES), 1) % blk
    blocks = [(i, None) for i in range(ctx_blocks)] + [
        (jnp.maximum(ctx_blocks + n - 1, 0), (key >= qry) & (n >= 1)),
        (jnp.maximum(ctx_blocks + n, 0), (key >= 0) & (n >= 0)),
        (ctx_blocks + n + 1, (key <= qry) & (n >= 0) & (n + 1 < n_lat // blk))]
    upper = lax.broadcasted_iota(jnp.int32, (1, 2 * LANES), 1) >= LANES
    zero = jnp.zeros((hd, LANES), BF16)

    for pair in range(ATT_KV_HEADS // 2):
        ps = slice(pair * LANES, (pair + 1) * LANES)
        for g in range(ATT_GQA):
            h0 = (2 * pair) * ATT_GQA + g
            h1 = (2 * pair + 1) * ATT_GQA + g
            w = jnp.concatenate([jnp.concatenate([qt_ref[h0 * hd:(h0 + 1) * hd, :], zero], axis=0),
                                 jnp.concatenate([zero, qt_ref[h1 * hd:(h1 + 1) * hd, :]], axis=0)], axis=1)
            sink = jnp.where(upper, sink_ref[h1], sink_ref[h0]) * LOG2_E
            scores = []
            m = sink
            for i, visible in blocks:
                s = jnp.dot(kx_ref[i, :, ps], w, preferred_element_type=F32)
                if visible is not None:
                    s = jnp.where(visible, s, NEG_INF)
                scores.append(s)
                m = jnp.maximum(m, jnp.max(s, axis=0, keepdims=True))
            den = jnp.exp2(sink - m)
            acc = jnp.zeros((LANES, 2 * LANES), F32)
            for (i, _), s in zip(blocks, scores):
                p = jnp.exp2(s - m)
                den = den + jnp.sum(p, axis=0, keepdims=True)
                acc = acc + jnp.dot(vt_ref[i, ps, :], p.astype(BF16), preferred_element_type=F32)
            ot_ref[h0 * hd:(h0 + 1) * hd, :] = acc[0:hd, 0:LANES] / den[:, 0:LANES]
            ot_ref[h1 * hd:(h1 + 1) * hd, :] = acc[hd:2 * hd, LANES:2 * LANES] / den[:, LANES:2 * LANES]

    for c0 in range(0, ATT_Q_DIM, LANES):
        o_ref[:, c0:c0 + LANES] = ot_ref[c0:c0 + LANES, :].T.astype(o_ref.dtype)


def _attention(proj, sink, tables, n_ctx, n_lat, n_batch):
    rows = n_ctx + n_lat
    nq = rows // ATT_BLOCK
    cos, s_lo, s_hi = tables
    qtab = pl.BlockSpec((ATT_BLOCK, LANES), lambda b, t: (t, 0))
    ktab = pl.BlockSpec((rows, LANES), lambda b, t: (0, 0))
    return pl.pallas_call(
        functools.partial(_attn_kernel, n_ctx=n_ctx, n_lat=n_lat),
        grid=(n_batch, nq),
        in_specs=[pl.BlockSpec(memory_space=pltpu.SMEM),
                  pl.BlockSpec((ATT_BLOCK, proj.shape[1]), lambda b, t: (b * nq + t, 0)),
                  pl.BlockSpec((rows, ATT_KV_DIM), lambda b, t: (b, 0)),
                  pl.BlockSpec((rows, ATT_KV_DIM), lambda b, t: (b, 1)),
                  qtab, qtab, qtab, ktab, ktab, ktab],
        out_specs=pl.BlockSpec((ATT_BLOCK, ATT_Q_DIM), lambda b, t: (b * nq + t, 0)),
        out_shape=jax.ShapeDtypeStruct((proj.shape[0], ATT_Q_DIM), BF16),
        scratch_shapes=[pltpu.VMEM((nq + 1, ATT_BLOCK, ATT_KV_DIM), BF16),
                        pltpu.VMEM((nq + 1, ATT_KV_DIM, ATT_BLOCK), BF16),
                        pltpu.VMEM((ATT_Q_DIM, ATT_BLOCK), BF16),
                        pltpu.VMEM((ATT_Q_DIM, ATT_BLOCK), F32)],
        compiler_params=_params("parallel", "arbitrary"),
        name="attention",
    )(sink.astype(F32), proj, proj, proj, cos, s_lo, s_hi, cos, s_lo, s_hi)


def _softplus(t):
    return jnp.maximum(t, 0.0) + jnp.log(1.0 + jnp.exp(-jnp.abs(t)))


def _split_bf16(t):
    hi = t.astype(BF16)
    return hi, (t - hi.astype(F32)).astype(BF16)


def _select_rows(sel, t):
    both = jnp.dot(sel, jnp.concatenate(_split_bf16(t), axis=1), preferred_element_type=F32)
    return both[:, 0:LANES] + both[:, LANES:2 * LANES]


def _conv_silu_into(src_ref, w_ref, b_ref, dst_ref, n_ctx):
    rows, width = src_ref.shape
    t = n_ctx
    ri = lax.broadcasted_iota(jnp.int32, (t, t), 0)
    ci = lax.broadcasted_iota(jnp.int32, (t, t), 1)
    down = (ri == ci + 1).astype(BF16)
    up = (ri + 1 == ci).astype(BF16)
    row8 = lax.broadcasted_iota(jnp.int32, (8, width), 0)
    w = w_ref[...]
    b = b_ref[...]
    for r0 in range(0, rows, t):
        x = src_ref[r0:r0 + t, :]
        prv = jnp.dot(down, x, preferred_element_type=F32)
        nxt = jnp.dot(up, x, preferred_element_type=F32)
        if r0 not in (0, n_ctx):
            edge = jnp.broadcast_to(src_ref[r0 - 1:r0, :].astype(F32), (8, width))
            prv = jnp.concatenate([jnp.where(row8 == 0, edge, prv[0:8]), prv[8:]], axis=0)
        if r0 + t not in (n_ctx, rows):
            edge = jnp.broadcast_to(src_ref[r0 + t:r0 + t + 1, :].astype(F32), (8, width))
            nxt = jnp.concatenate([nxt[:t - 8], jnp.where(row8 == 7, edge, nxt[t - 8:])], axis=0)
        out = b + prv * w[0:1, :] + x.astype(F32) * w[1:2, :] + nxt * w[2:3, :]
        dst_ref[r0:r0 + t, :] = _silu(out).astype(dst_ref.dtype)


def _ssd_fill_tables(tr, tri_ref, expand_ref):
    q = SSD_CHUNK
    ri = lax.broadcasted_iota(jnp.int32, (tr, tr), 0)
    ci = lax.broadcasted_iota(jnp.int32, (tr, tr), 1)
    same = (ri // q) == (ci // q)
    tri_ref[0:tr, :] = (same & (ri >= ci)).astype(BF16)
    tri_ref[tr:2 * tr, :] = (same & (ri <= ci)).astype(BF16)
    src = lax.broadcasted_iota(jnp.int32, (2 * LANES, SSD_GROUP_DIM), 0) % LANES
    dst = lax.broadcasted_iota(jnp.int32, (2 * LANES, SSD_GROUP_DIM), 1) // SSD_HEAD_DIM
    for d in range(2):
        expand_ref[d] = (src == d * SSD_GROUP_HEADS + dst).astype(BF16)


def _ssd_decays(r0, tr, rev, xa, dtc, cum, expand, cumx_ref, xdec_ref):
    q = SSD_CHUNK
    npc = tr // q
    cum_hi, cum_lo = _split_bf16(cum)
    cum = cum_hi.astype(F32) + cum_lo.astype(F32)
    dt_hi, dt_lo = _split_bf16(dtc)
    wide = jnp.dot(jnp.concatenate([jnp.concatenate([cum_hi, cum_lo], axis=1),
                                    jnp.concatenate([dt_hi, dt_lo], axis=1)], axis=0),
                   expand, preferred_element_type=F32)
    cum_x = wide[0:tr]
    dt_x = wide[tr:2 * tr]
    tot_x = _rows_from(cum_x, [c * q + (0 if rev else q - 1) for c in range(npc)], q)
    xdt = xa * dt_x
    cumx_ref[pl.ds(r0, tr), :] = cum_x
    xdec_ref[pl.ds(r0, tr), :] = (xdt * jnp.exp2(tot_x - cum_x)).astype(BF16)
    return cum, cum_x, xdt


def _ssd_intra(t, tr, tables, xa_ref, ba_ref, ca_ref, dtv_ref, da_ref, y_ref, fwd_refs, bwd_refs):
    q = SSD_CHUNK
    assert 2 * q == LANES and 2 * SSD_HEAD_DIM == LANES
    tri_ref, expand_ref = tables
    r0 = pl.multiple_of(t * tr, tr)
    xa = xa_ref[pl.ds(r0, tr), :]
    bc = ba_ref[pl.ds(r0, tr), :]
    cc = ca_ref[pl.ds(r0, tr), :]
    dtc = dtv_ref[pl.ds(r0, tr), :]
    cums = _select_rows(tri_ref[...], da_ref[pl.ds(r0, tr), :])
    scans = [(False, 0) + _ssd_decays(r0, tr, False, xa, dtc, cums[0:tr], expand_ref[0], *fwd_refs),
             (True, SSD_GROUP_HEADS) + _ssd_decays(r0, tr, True, xa, dtc, cums[tr:2 * tr], expand_ref[1], *bwd_refs)]

    row = lax.broadcasted_iota(jnp.int32, (q, LANES), 0)
    lane = lax.broadcasted_iota(jnp.int32, (q, LANES), 1)
    upper = lane >= q
    key = jnp.where(upper, lane - q, lane)
    for c in range(tr // q):
        cs = slice(c * q, (c + 1) * q)
        cb2 = lax.dot_general(cc[cs], jnp.concatenate([bc[cs], bc[cs]], axis=0), _NT, preferred_element_type=F32)
        cum_rows = [jnp.concatenate([cum[cs], pltpu.roll(cum[cs], LANES - 1, 1)], axis=0).T
                    for _, _, cum, _, _ in scans]
        ys = []
        for pr in range(SSD_GROUP_HEADS // 2):
            sl = slice(pr * LANES, (pr + 1) * LANES)
            weights, values = [], []
            for (rev, lane0, _, cum_x, xdt), rows in zip(scans, cum_rows):
                le = lane0 + 2 * pr
                seen = (row <= key) if rev else (row >= key)
                decay = jnp.exp2(jnp.where(seen, cum_x[cs, sl] - rows[le:le + 1, :], NEG_INF))
                weights.append((cb2 * decay).astype(BF16))
                xp = xdt[cs, sl]
                values.append(jnp.concatenate([jnp.where(upper, 0.0, xp), jnp.where(upper, xp, 0.0)],
                                              axis=0).astype(BF16))
            ys.append(jnp.dot(jnp.concatenate(weights, axis=1), jnp.concatenate(values, axis=0),
                              preferred_element_type=F32))
        y_ref[pl.ds(r0 + c * q, q), :] = jnp.concatenate(ys, axis=1)


def _ssd_state_step(c, rev, ba_ref, ca_ref, y_ref, cumx_ref, xdec_ref, s_ref):
    q = SSD_CHUNK
    r0 = pl.multiple_of(c * q, q)
    cum_x = cumx_ref[pl.ds(r0, q), :]
    tot_x = cum_x[0:1, :] if rev else cum_x[q - 1:q, :]
    state = s_ref[...]
    y_ref[pl.ds(r0, q), :] += (jnp.dot(ca_ref[pl.ds(r0, q), :], state.astype(BF16), preferred_element_type=F32)
                               * jnp.exp2(cum_x))
    upd = lax.dot_general(ba_ref[pl.ds(r0, q), :], xdec_ref[pl.ds(r0, q), :], _TN, preferred_element_type=F32)
    s_ref[...] = state * jnp.exp2(tot_x) + upd


def _ssd_kernel(x_ref, z_ref, b_ref, c_ref, dt_ref, cwx_ref, cwb_ref, cwc_ref, cbx_ref, cbb_ref, cbc_ref,
                dtb_ref, alog_ref, dskip_ref, nw_ref, o_ref,
                xa_ref, ba_ref, ca_ref, dtv_ref, da_ref, y_ref, cumxf_ref, cumxb_ref, xdecf_ref, xdecb_ref,
                s_ref, sb_ref, tri_ref, expand_ref, *, n_ctx, n_lat, tile):
    _conv_silu_into(x_ref, cwx_ref, cbx_ref, xa_ref, n_ctx)
    _conv_silu_into(b_ref, cwb_ref, cbb_ref, ba_ref, n_ctx)
    _conv_silu_into(c_ref, cwc_ref, cbc_ref, ca_ref, n_ctx)
    dtv = _softplus(dt_ref[...].astype(F32) + dtb_ref[...])
    dtv_ref[...] = dtv
    da_ref[...] = dtv * (-jnp.exp(alog_ref[...]) * LOG2_E)

    seq = (xa_ref, ba_ref, ca_ref, dtv_ref, da_ref)
    fwd = (cumxf_ref, xdecf_ref)
    bwd = (cumxb_ref, xdecb_ref)

    tables = (tri_ref, expand_ref)
    _ssd_fill_tables(tile, *tables)

    def intra_step(t, carry):
        _ssd_intra(t, tile, tables, *seq, y_ref, fwd, bwd)
        return carry

    lax.fori_loop(0, (n_ctx + n_lat) // tile, intra_step, 0, unroll=TILE_UNROLL)

    n_c = n_ctx // SSD_CHUNK
    n_l = n_lat // SSD_CHUNK

    def ctx_step(s, carry):
        _ssd_state_step(s, False, ba_ref, ca_ref, y_ref, *fwd, s_ref)
        _ssd_state_step(n_c - 1 - s, True, ba_ref, ca_ref, y_ref, *bwd, sb_ref)
        return carry

    def lat_step(s, carry):
        _ssd_state_step(n_c + s, False, ba_ref, ca_ref, y_ref, *fwd, s_ref)
        _ssd_state_step(n_c + n_l - 1 - s, True, ba_ref, ca_ref, y_ref, *bwd, sb_ref)
        return carry

    s_ref[...] = jnp.zeros(s_ref.shape, F32)
    sb_ref[...] = jnp.zeros(sb_ref.shape, F32)
    lax.fori_loop(0, n_c, ctx_step, 0, unroll=SCAN_UNROLL)
    lax.fori_loop(0, n_l, lat_step, 0, unroll=SCAN_UNROLL)

    for r0 in range(0, n_ctx + n_lat, n_ctx):
        y = y_ref[r0:r0 + n_ctx, :] + xa_ref[r0:r0 + n_ctx, :] * dskip_ref[...]
        t = y * _silu(z_ref[r0:r0 + n_ctx, :].astype(F32))
        o_ref[r0:r0 + n_ctx, :] = (_rms_rows(t) * nw_ref[...]).astype(o_ref.dtype)


def _ssd_layout(w_in, conv_w, conv_b, dt_bias, a_log, d_skip):
    n_state = SSD_CONV_DIM + 2 * SSD_HEADS
    gh = SSD_GROUP_HEADS

    def dt_blocks(t):
        lead = t.shape[:-1]
        t = t.reshape(lead + (2, SSD_GROUPS, gh))
        t = jnp.moveaxis(t, -3, -2).reshape(lead + (SSD_GROUPS, 2 * gh))
        t = jnp.pad(t, [(0, 0)] * (len(lead) + 1) + [(0, LANES - 2 * gh)])
        return t.reshape(lead + (SSD_GROUPS * LANES,))

    w_in = w_in.astype(BF16)
    w = jnp.concatenate([w_in[:, :SSD_D_INNER], w_in[:, n_state:], w_in[:, SSD_D_INNER:SSD_CONV_DIM],
                         dt_blocks(w_in[:, SSD_CONV_DIM:n_state])], axis=1)
    return (w, conv_w, conv_b.reshape(1, -1), dt_blocks(dt_bias.reshape(-1)).reshape(1, -1),
            dt_blocks(a_log.reshape(-1)).reshape(1, -1), jnp.repeat(d_skip, SSD_HEAD_DIM).reshape(1, -1))


def _ssd(proj, conv_w, conv_b, dt_bias, a_log, d_skip, norm_w, n_ctx, n_lat, n_batch):
    rows = n_ctx + n_lat
    gd = SSD_GROUP_DIM
    tile = _scan_tile(rows, SSD_CHUNK)
    xb = SSD_D_INNER // gd
    bb = 2 * SSD_D_INNER // LANES
    sec = lambda width, off: pl.BlockSpec((rows, width), lambda b, g: (b, off + g))
    par = lambda r, width, off: pl.BlockSpec((r, width), lambda b, g: (0, off + g))
    return pl.pallas_call(
        functools.partial(_ssd_kernel, n_ctx=n_ctx, n_lat=n_lat, tile=tile),
        grid=(n_batch, SSD_GROUPS),
        in_specs=[sec(gd, 0), sec(gd, xb), sec(LANES, bb), sec(LANES, bb + SSD_GROUPS),
                  sec(LANES, bb + 2 * SSD_GROUPS),
                  par(3, gd, 0), par(3, LANES, SSD_D_INNER // LANES), par(3, LANES, SSD_D_INNER // LANES + SSD_GROUPS),
                  par(1, gd, 0), par(1, LANES, SSD_D_INNER // LANES), par(1, LANES, SSD_D_INNER // LANES + SSD_GROUPS),
                  par(1, LANES, 0), par(1, LANES, 0), par(1, gd, 0), par(1, gd, 0)],
        out_specs=pl.BlockSpec((rows, gd), lambda b, g: (b, g)),
        out_shape=jax.ShapeDtypeStruct((proj.shape[0], SSD_D_INNER), BF16),
        scratch_shapes=[pltpu.VMEM((rows, gd), F32),
                        pltpu.VMEM((rows, SSD_STATE), BF16),
                        pltpu.VMEM((rows, SSD_STATE), BF16),
                        pltpu.VMEM((rows, LANES), F32),
                        pltpu.VMEM((rows, LANES), F32),
                        pltpu.VMEM((rows, gd), F32),
                        pltpu.VMEM((rows, gd), F32),
                        pltpu.VMEM((rows, gd), F32),
                        pltpu.VMEM((rows, gd), BF16),
                        pltpu.VMEM((rows, gd), BF16),
                        pltpu.VMEM((SSD_STATE, gd), F32),
                        pltpu.VMEM((SSD_STATE, gd), F32),
                        pltpu.VMEM((2 * tile, tile), BF16),
                        pltpu.VMEM((2, 2 * LANES, gd), BF16)],
        compiler_params=_params("parallel", "parallel"),
        name="ssd",
    )(proj, proj, proj, proj, proj, conv_w, conv_w, conv_w, conv_b, conv_b, conv_b,
      dt_bias, a_log, d_skip, norm_w.reshape(1, -1))


HG_OWN, HG_EARLIER, HG_LATER = 1.0, 2.0, 3.0


def _hgrn_fill_tables(tr, tri_ref, kind_ref, place_ref):
    ch, sub = HG_CHUNK, HG_SUB
    ri = lax.broadcasted_iota(jnp.int32, (tr, tr), 0)
    ci = lax.broadcasted_iota(jnp.int32, (tr, tr), 1)
    same = (ri // ch) == (ci // ch)
    tri_ref[0] = (same & (ri >= ci)).astype(BF16)
    tri_ref[1] = (same & (ri <= ci)).astype(BF16)
    rb, cb = ri // sub, ci // sub
    kind_ref[...] = jnp.where(rb == cb, HG_OWN,
                              jnp.where(same & (cb < rb), HG_EARLIER, jnp.where(same & (cb > rb), HG_LATER, 0.0)))
    src_key = lax.broadcasted_iota(jnp.int32, (sub * LANES, LANES), 0) // LANES
    dst_key = lax.broadcasted_iota(jnp.int32, (sub * LANES, LANES), 1) % sub
    place_ref[...] = (src_key == dst_key).astype(BF16)


def _hgrn_intra(t, tr, rev, tables, lg_ref, k_ref, q_ref, v_ref, o_ref, qst_ref, upd_ref, dec_ref):
    ch, sub = HG_CHUNK, HG_SUB
    npc, nb, nblk = tr // ch, ch // sub, tr // sub
    tri_ref, kind_ref, place_ref = tables
    r0 = pl.multiple_of(t * tr, tr)
    lg = lg_ref[pl.ds(r0, tr), :]
    k = k_ref[pl.ds(r0, tr), :]
    q = q_ref[pl.ds(r0, tr), :]
    v = v_ref[pl.ds(r0, tr), :]
    tri = tri_ref[1 if rev else 0]
    cum = _select_rows(tri, lg)

    last = [c * ch + (0 if rev else ch - 1) for c in range(npc)]
    qst_ref[pl.ds(r0, tr), :] = (q * jnp.exp2(cum)).astype(BF16)
    k_end = (k * jnp.exp2(_rows_from(cum, last, ch) - cum)).astype(BF16)
    zero_chunk = jnp.zeros((ch, LANES), BF16)
    k_end_by_chunk = jnp.concatenate(
        [jnp.concatenate([k_end[c2 * ch:(c2 + 1) * ch] if c2 == c else zero_chunk for c2 in range(npc)], axis=0)
         for c in range(npc)], axis=1)
    upd = lax.dot_general(v, k_end_by_chunk, _TN, preferred_element_type=F32)
    for c in range(npc):
        chunk = t * npc + c
        dec_ref[pl.ds(chunk, 1), :] = jnp.exp2(cum[last[c]:last[c] + 1, :])
        upd_ref[pl.ds(pl.multiple_of(chunk * LANES, LANES), LANES), :] = upd[:, c * LANES:(c + 1) * LANES]

    def own_edge(j):
        i = j % nb
        if rev:
            return None if i == nb - 1 else (j + 1) * sub
        return None if i == 0 else j * sub - 1

    qs = q * jnp.exp2(cum - _rows_from(cum, [own_edge(j) for j in range(nblk)], sub))
    queries, keys = {}, {}
    for i in (range(nb - 1) if rev else range(1, nb)):
        queries[i] = jnp.concatenate([qs[(c * nb + i) * sub:(c * nb + i + 1) * sub] for c in range(npc)],
                                     axis=0).astype(BF16)
        pieces = []
        for c in range(npc):
            base = c * ch
            e = base + ((i + 1) * sub if rev else i * sub - 1)
            lo, hi = ((i + 1) * sub, ch) if rev else (0, i * sub)
            part = k[base + lo:base + hi] * jnp.exp2(cum[e:e + 1, :] - cum[base + lo:base + hi])
            pad = jnp.zeros((ch - (hi - lo), LANES), F32)
            pieces += [pad, part] if rev else [part, pad]
        keys[i] = jnp.concatenate(pieces, axis=0).astype(BF16)
    weights = {}
    order = sorted(queries)
    nq = npc * sub
    zero_q = jnp.zeros((nq, LANES), BF16)
    for i1, i2 in zip(order[0::2], order[1::2] + [None]):
        if i2 is None:
            weights[i1] = lax.dot_general(queries[i1], keys[i1], _NT, preferred_element_type=F32)
            continue
        both = lax.dot_general(
            jnp.concatenate([jnp.concatenate([queries[i1], zero_q], axis=1),
                             jnp.concatenate([zero_q, queries[i2]], axis=1)], axis=0),
            jnp.concatenate([keys[i1], keys[i2]], axis=1), _NT, preferred_element_type=F32)
        weights[i1], weights[i2] = both[0:nq], both[nq:2 * nq]
    zero_rows = jnp.zeros((sub, tr), F32)
    a_off = jnp.concatenate([weights[j % nb][(j // nb) * sub:(j // nb + 1) * sub] if j % nb in weights
                             else zero_rows for j in range(nblk)], axis=0)

    row8 = lax.broadcasted_iota(jnp.int32, (sub, LANES), 0)
    prods = []
    for s in range(sub):
        idx = [j * sub + s for j in range(nblk)]
        unseen = jnp.where((row8 <= s) if rev else (row8 >= s), 0.0, NEG_INF)
        w = jnp.exp2(cum - _rows_from(cum, idx, sub) + jnp.concatenate([unseen] * nblk, axis=0))
        prods.append((q * _rows_from(k, idx, sub) * w).astype(BF16))
    a_own = jnp.dot(jnp.concatenate(prods, axis=1), place_ref[...], preferred_element_type=F32)
    kind = kind_ref[...]
    a = jnp.where(kind == HG_OWN, jnp.concatenate([a_own] * (tr // LANES), axis=1),
                  jnp.where(kind == (HG_LATER if rev else HG_EARLIER), a_off, 0.0))
    o_ref[pl.ds(r0, tr), :] = jnp.dot(a.astype(BF16), v, preferred_element_type=F32)


def _hgrn_state_step(c, st, upd_ref, dec_ref, sin_ref):
    rows = pl.ds(pl.multiple_of(c * LANES, LANES), LANES)
    sin_ref[rows, :] = st.astype(BF16)
    return st * dec_ref[pl.ds(c, 1), :] + upd_ref[rows, :]


def _hgrn_readout(c, qst_ref, sin_ref):
    ch = HG_CHUNK
    return lax.dot_general(qst_ref[pl.ds(pl.multiple_of(c * ch, ch), ch), :],
                           sin_ref[pl.ds(pl.multiple_of(c * LANES, LANES), LANES), :], _NT,
                           preferred_element_type=F32)


def _hgrn_kernel(ff_ref, fb_ref, v_ref, q_ref, gate_ref, lb_ref, nw_ref, out_ref,
                 lgf_ref, lgb_ref, kf_ref, kb_ref, qa_ref, o_ref, ob_ref, qstf_ref, qstb_ref,
                 updf_ref, updb_ref, decf_ref, decb_ref, sinf_ref, sinb_ref, tri_ref, kind_ref, place_ref,
                 *, n_ctx, n_lat, tile):
    tables = (tri_ref, kind_ref, place_ref)
    _hgrn_fill_tables(tile, *tables)
    for d, (f_ref, lg_ref, k_ref) in enumerate(((ff_ref, lgf_ref, kf_ref), (fb_ref, lgb_ref, kb_ref))):
        lb = lb_ref[d:d + 1, :]
        g = lb + (1.0 - lb) * jax.nn.sigmoid(f_ref[...].astype(F32))
        lg_ref[...] = jnp.log(g) * LOG2_E
        k_ref[...] = 1.0 - g
    qa_ref[...] = _silu(q_ref[...].astype(F32))

    fwd = (qstf_ref, updf_ref, decf_ref)
    bwd = (qstb_ref, updb_ref, decb_ref)
    fwd_scan = (updf_ref, decf_ref, sinf_ref)
    bwd_scan = (updb_ref, decb_ref, sinb_ref)

    def intra_step(t, carry):
        _hgrn_intra(t, tile, False, tables, lgf_ref, kf_ref, qa_ref, v_ref, o_ref, *fwd)
        _hgrn_intra(t, tile, True, tables, lgb_ref, kb_ref, qa_ref, v_ref, ob_ref, *bwd)
        return carry

    lax.fori_loop(0, (n_ctx + n_lat) // tile, intra_step, 0, unroll=TILE_UNROLL)

    n_c = n_ctx // HG_CHUNK
    n_l = n_lat // HG_CHUNK

    def ctx_step(s, carry):
        return (_hgrn_state_step(s, carry[0], *fwd_scan),
                _hgrn_state_step(n_c - 1 - s, carry[1], *bwd_scan))

    def lat_step(s, carry):
        return (_hgrn_state_step(n_c + s, carry[0], *fwd_scan),
                _hgrn_state_step(n_c + n_l - 1 - s, carry[1], *bwd_scan))

    zero = jnp.zeros((LANES, LANES), F32)
    carry = lax.fori_loop(0, n_c, ctx_step, (zero, zero), unroll=SCAN_UNROLL)
    lax.fori_loop(0, n_l, lat_step, carry, unroll=SCAN_UNROLL)

    def read_step(c, carry):
        rows = pl.ds(pl.multiple_of(c * HG_CHUNK, HG_CHUNK), HG_CHUNK)
        o_ref[rows, :] += _hgrn_readout(c, qstf_ref, sinf_ref) + _hgrn_readout(c, qstb_ref, sinb_ref)
        return carry

    lax.fori_loop(0, n_c + n_l, read_step, 0, unroll=READ_UNROLL)

    o = o_ref[...] + ob_ref[...]
    out_ref[...] = (_rms_rows(o) * nw_ref[...] * _silu(gate_ref[...].astype(F32))).astype(out_ref.dtype)


def _hgrn(proj, lower, norm_w, n_ctx, n_lat, n_batch):
    rows = n_ctx + n_lat
    sec = lambda off: pl.BlockSpec((rows, LANES), lambda b, h: (b, off * HG_HEADS + h))
    seq = lambda dt: pltpu.VMEM((rows, LANES), dt)
    n_chunks = rows // HG_CHUNK
    upd = pltpu.VMEM((n_chunks * LANES, LANES), F32)
    dec = pltpu.VMEM((-(-n_chunks // 8) * 8, LANES), F32)
    sin = pltpu.VMEM((n_chunks * LANES, LANES), BF16)
    tile = _scan_tile(rows, HG_CHUNK)
    tables = [pltpu.VMEM((2, tile, tile), BF16), pltpu.VMEM((tile, tile), F32),
              pltpu.VMEM((HG_SUB * LANES, LANES), BF16)]
    return pl.pallas_call(
        functools.partial(_hgrn_kernel, n_ctx=n_ctx, n_lat=n_lat, tile=tile),
        grid=(n_batch, HG_HEADS),
        in_specs=[sec(0), sec(1), sec(2), sec(3), sec(4),
                  pl.BlockSpec((2, LANES), lambda b, h: (0, h)),
                  pl.BlockSpec((1, LANES), lambda b, h: (0, h))],
        out_specs=pl.BlockSpec((rows, LANES), lambda b, h: (b, h)),
        out_shape=jax.ShapeDtypeStruct((proj.shape[0], HG_VAL), BF16),
        scratch_shapes=[seq(F32), seq(F32), seq(F32), seq(F32), seq(F32), seq(F32), seq(F32),
                        seq(BF16), seq(BF16), upd, upd, dec, dec, sin, sin] + tables,
        compiler_params=_params("parallel", "parallel"),
        name="hgrn2",
    )(proj, proj, proj, proj, proj, lower, norm_w.reshape(1, -1))


def _hgrn_lower_bounds(lb_raw):
    p = jax.nn.softmax(lb_raw.astype(F32), axis=1)
    return jnp.cumsum(p, axis=1) - p[:, :1]


def kernel(x, c, ctx, c_ctx, ada_w, ada_b, norm_g, ffn_w_in, ffn_w_out, attn_w_in, attn_w_out, attn_sink,
           ssd_w_in, ssd_conv_w, ssd_conv_b, ssd_dt_bias, ssd_a_log, ssd_d, ssd_norm_w, ssd_w_out,
           hgrn_w_in, hgrn_lb, hgrn_norm_w, hgrn_w_out):
    n_batch, n_lat, d = x.shape
    n_ctx = ctx.shape[1]
    rows = n_ctx + n_lat
    tm = _token_tile(n_ctx, n_ctx + n_lat)
    tm_lat = _token_tile(n_ctx, n_lat)

    r = jnp.concatenate([ctx, x], axis=1).reshape(n_batch * rows, d)
    n_cond = -(-(n_batch + 1) // 8) * 8
    c_rows = jnp.concatenate([c, c_ctx[None, :], jnp.zeros((n_cond - n_batch - 1, d), F32)], axis=0)
    mods = _ada_mods(c_rows, ada_w, ada_b)
    tables = _rope_tables(n_ctx, n_lat)
    lower = _hgrn_lower_bounds(hgrn_lb)

    for i in range(DEPTH):
        kind, j = i % N_MIXERS, i // N_MIXERS
        mod = mods[i]
        if kind == 0:
            proj = _modproj(r, mod, norm_g[i, 0], attn_w_in[j].astype(BF16), n_ctx, tm, n_batch)
            y = _attention(proj, attn_sink[j], tables, n_ctx, n_lat, n_batch)
            w_out = attn_w_out[j]
        elif kind == 1:
            w_in, conv_w, conv_b, dt_bias, a_log, d_skip = _ssd_layout(
                ssd_w_in[j], ssd_conv_w[j], ssd_conv_b[j], ssd_dt_bias[j], ssd_a_log[j], ssd_d[j])
            proj = _modproj(r, mod, norm_g[i, 0], w_in, n_ctx, tm, n_batch)
            y = _ssd(proj, conv_w, conv_b, dt_bias, a_log, d_skip, ssd_norm_w[j], n_ctx, n_lat, n_batch)
            w_out = ssd_w_out[j]
        else:
            proj = _modproj(r, mod, norm_g[i, 0], hgrn_w_in[j].astype(BF16), n_ctx, tm, n_batch)
            y = _hgrn(proj, lower[:, i], hgrn_norm_w[j], n_ctx, n_lat, n_batch)
            w_out = hgrn_w_out[j]
        ffn_w = (ffn_w_in[i].astype(BF16), ffn_w_out[i].astype(BF16))
        if i < DEPTH - 1:
            r = _outproj(y, w_out.astype(BF16), r, mod, norm_g[i, 1], n_ctx, tm, n_batch)
            r = _ffn(r, mod, norm_g[i, 2], norm_g[i, 3], *ffn_w, n_ctx, tm, n_batch)
        else:
            r = _outproj(y, w_out.astype(BF16), r, mod, norm_g[i, 1], n_ctx, tm_lat, n_batch, lat_rows=n_lat)
            r = _ffn(r, mod, norm_g[i, 2], norm_g[i, 3], *ffn_w, n_ctx, tm_lat, n_batch, has_ctx=False)
    return r.reshape(n_batch, n_lat, d)
```

```python
import functools
import math

import jax
import jax.numpy as jnp
from jax import lax
from jax.experimental import pallas as pl
from jax.experimental.pallas import tpu as pltpu

F32 = jnp.float32
BF16 = jnp.bfloat16

D_MODEL = 2048
DEPTH = 4
N_MIXERS = 3
GRID_W = 64
RMS_EPS = 1e-6
N_MOD = 6

ATT_HEADS = 32
ATT_KV_HEADS = 4
ATT_GQA = ATT_HEADS // ATT_KV_HEADS
ATT_HEAD_DIM = 64
ATT_WINDOW = 128
ATT_BLOCK = 128
ATT_Q_DIM = ATT_HEADS * ATT_HEAD_DIM
ATT_KV_DIM = ATT_KV_HEADS * ATT_HEAD_DIM
ROPE_THETA = 10000.0
ROPE_PAIRS = ATT_HEAD_DIM // 4
NEG_INF = -1e30
LOG2_E = 1.0 / math.log(2.0)

SSD_D_INNER = 2 * D_MODEL
SSD_HEAD_DIM = 64
SSD_HEADS = SSD_D_INNER // SSD_HEAD_DIM
SSD_GROUPS = 8
SSD_GROUP_HEADS = SSD_HEADS // SSD_GROUPS
SSD_GROUP_DIM = SSD_D_INNER // SSD_GROUPS
SSD_STATE = 128
SSD_GN = SSD_GROUPS * SSD_STATE
SSD_CONV_DIM = SSD_D_INNER + 2 * SSD_GN
SSD_CHUNK = 64

HG_EXPAND = 128
HG_HEADS = D_MODEL // HG_EXPAND
HG_VAL = D_MODEL
HG_CHUNK = 64
HG_SUB = 8

FFN_DIM = -(-8 * D_MODEL // (3 * 256)) * 256

LANES = 128
MXU_TILE = 256
VMEM_LIMIT_BYTES = 56 * 1024 * 1024
MAX_TOKEN_TILE = 768
MAX_PROJ_TILE = 2816
FFN_TILE = 512
SCAN_UNROLL = 4
READ_UNROLL = 12
TILE_UNROLL = 3


def _params(*sem):
    return pltpu.CompilerParams(dimension_semantics=sem, vmem_limit_bytes=VMEM_LIMIT_BYTES)


def _token_tile(n_ctx, rows):
    best = n_ctx
    for mult in range(1, rows // n_ctx + 1):
        t = mult * n_ctx
        if rows % t == 0 and t <= MAX_TOKEN_TILE:
            best = t
    assert rows % best == 0
    return best


def _silu(t):
    h = 0.5 * t
    return h + h * jnp.tanh(h)


def _rms_rows(t):
    return t * lax.rsqrt(jnp.mean(t * t, axis=-1, keepdims=True) + RMS_EPS)


def _rows_from(t, idx, height):
    width = t.shape[1]
    return jnp.concatenate([jnp.zeros((height, width), t.dtype) if r is None
                            else jnp.broadcast_to(t[r:r + 1, :], (height, width)) for r in idx], axis=0)


_NT = (((1,), (1,)), ((), ()))
_TN = (((0,), (0,)), ((), ()))


def _scan_tile(rows, chunk):
    return next(t for t in (4 * chunk, 2 * chunk) if rows % t == 0 and t % LANES == 0)


def _ada_kernel(c_ref, w_ref, b_ref, o_ref):
    s = _silu(c_ref[...]).astype(BF16)
    o_ref[...] = jnp.dot(s, w_ref[...].astype(BF16), preferred_element_type=F32) + b_ref[...]


def _ada_mods(c_rows, ada_w, ada_b):
    r = c_rows.shape[0]
    n = ada_w.shape[-1]
    tn = 1024
    out = pl.pallas_call(
        _ada_kernel,
        grid=(DEPTH, n // tn),
        in_specs=[pl.BlockSpec((r, D_MODEL), lambda l, j: (0, 0)),
                  pl.BlockSpec((None, D_MODEL, tn), lambda l, j: (l, 0, j)),
                  pl.BlockSpec((None, 1, tn), lambda l, j: (l, 0, j))],
        out_specs=pl.BlockSpec((None, r, tn), lambda l, j: (l, 0, j)),
        out_shape=jax.ShapeDtypeStruct((DEPTH, r, n), F32),
        compiler_params=_params("parallel", "parallel"),
        name="ada_mods",
    )(c_rows, ada_w, ada_b.reshape(DEPTH, 1, n))
    return out.reshape(DEPTH, r, N_MOD, D_MODEL)


def _first_tile(tiles_per_batch, has_ctx):
    return (pl.program_id(0) % tiles_per_batch == 0) if has_ctx else None


def _mod_row(mod_ref, modc_ref, row, first, r0):
    m = mod_ref[row:row + 1, :]
    if r0 == 0 and first is not None:
        m = jnp.where(first, modc_ref[row:row + 1, :], m)
    return m


def _modulate_into(x_ref, u_ref, mod_ref, modc_ref, g_ref, shift_row, first, n_ctx):
    g = g_ref[...]
    for r0 in range(0, x_ref.shape[0], n_ctx):
        shift = _mod_row(mod_ref, modc_ref, shift_row, first, r0)
        scale = _mod_row(mod_ref, modc_ref, shift_row + 1, first, r0)
        x = x_ref[r0:r0 + n_ctx, :]
        u_ref[r0:r0 + n_ctx, :] = (_rms_rows(x) * (g * (1.0 + scale)) + shift).astype(u_ref.dtype)


def _gated_residual_into(o_ref, f_src, x_ref, mod_ref, modc_ref, g_ref, gate_row, first, n_ctx):
    g = g_ref[...]
    for r0 in range(0, x_ref.shape[0], n_ctx):
        gate = _mod_row(mod_ref, modc_ref, gate_row, first, r0)
        f = f_src[r0:r0 + n_ctx, :]
        o_ref[r0:r0 + n_ctx, :] = x_ref[r0:r0 + n_ctx, :] + gate * (_rms_rows(f) * g)


def _mod_specs(tiles_per_batch, n_batch):
    return [pl.BlockSpec((None, N_MOD, D_MODEL), lambda i, j: (i // tiles_per_batch, 0, 0)),
            pl.BlockSpec((None, N_MOD, D_MODEL), lambda i, j: (n_batch, 0, 0))]


def _modproj_kernel(x_ref, mod_ref, modc_ref, g_ref, w_ref, o_ref, u_ref, *, n_ctx, tiles_per_batch,
                    has_ctx=True):
    first = _first_tile(tiles_per_batch, has_ctx)

    @pl.when(pl.program_id(1) == 0)
    def _():
        _modulate_into(x_ref, u_ref, mod_ref, modc_ref, g_ref, 0, first, n_ctx)

    tn = o_ref.shape[1]
    cut = -(-tn // (2 * MXU_TILE)) * MXU_TILE
    for lo, hi in ((0, cut), (cut, tn)):
        if hi > lo:
            o_ref[:, lo:hi] = jnp.dot(u_ref[...], w_ref[:, lo:hi], preferred_element_type=F32).astype(o_ref.dtype)


def _modproj(x, mod, g, w, n_ctx, tm, n_batch):
    t, d = x.shape
    n = w.shape[1]
    tn = max(c for c in range(LANES, MAX_PROJ_TILE + 1, LANES) if n % c == 0)
    tpb = t // n_batch // tm
    return pl.pallas_call(
        functools.partial(_modproj_kernel, n_ctx=n_ctx, tiles_per_batch=tpb),
        grid=(t // tm, n // tn),
        in_specs=[pl.BlockSpec((tm, d), lambda i, j: (i, 0))] + _mod_specs(tpb, n_batch)
        + [pl.BlockSpec((1, d), lambda i, j: (0, 0)),
           pl.BlockSpec((d, tn), lambda i, j: (0, j))],
        out_specs=pl.BlockSpec((tm, tn), lambda i, j: (i, j)),
        out_shape=jax.ShapeDtypeStruct((t, n), BF16),
        scratch_shapes=[pltpu.VMEM((tm, d), BF16)],
        compiler_params=_params("parallel", "arbitrary"),
        name="modproj",
    )(x, mod, mod, g.reshape(1, d), w)


def _outproj_kernel(y_ref, w_ref, x_ref, mod_ref, modc_ref, g_ref, o_ref, *, n_ctx, tiles_per_batch, nk,
                    has_ctx):
    k = pl.program_id(1)
    first = _first_tile(tiles_per_batch, has_ctx)
    part = jnp.dot(y_ref[...], w_ref[...], preferred_element_type=F32)
    if nk == 1:
        _gated_residual_into(o_ref, part, x_ref, mod_ref, modc_ref, g_ref, 2, first, n_ctx)
        return

    @pl.when(k == 0)
    def _():
        o_ref[...] = part

    @pl.when(k > 0)
    def _():
        o_ref[...] += part

    @pl.when(k == nk - 1)
    def _():
        _gated_residual_into(o_ref, o_ref, x_ref, mod_ref, modc_ref, g_ref, 2, first, n_ctx)


def _outproj(y, w, x, mod, g, n_ctx, tm, n_batch, lat_rows=None):
    t, d = x.shape
    kdim = y.shape[1]
    tk = min(kdim, 2048)
    nk = kdim // tk
    rows = t // n_batch
    if lat_rows is None:
        tpb, t_out = rows // tm, t
        y_spec = pl.BlockSpec((tm, tk), lambda i, k: (i, k))
        x_spec = pl.BlockSpec((tm, d), lambda i, k: (i, 0))
    else:
        tpb, t_out = lat_rows // tm, n_batch * lat_rows
        assert rows % n_ctx == 0 and lat_rows % n_ctx == 0 and tm % n_ctx == 0
        first_row = lambda i: pl.multiple_of(
            ((i // tpb) * (rows // n_ctx) + (rows - lat_rows) // n_ctx + (i % tpb) * (tm // n_ctx)) * n_ctx, n_ctx)
        y_spec = pl.BlockSpec((pl.Element(tm), pl.Element(tk)), lambda i, k: (first_row(i), k * tk))
        x_spec = pl.BlockSpec((pl.Element(tm), pl.Element(d)), lambda i, k: (first_row(i), 0))
    return pl.pallas_call(
        functools.partial(_outproj_kernel, n_ctx=n_ctx, tiles_per_batch=tpb, nk=nk, has_ctx=lat_rows is None),
        grid=(t_out // tm, nk),
        in_specs=[y_spec, pl.BlockSpec((tk, d), lambda i, k: (k, 0)), x_spec] + _mod_specs(tpb, n_batch)
        + [pl.BlockSpec((1, d), lambda i, k: (0, 0))],
        out_specs=pl.BlockSpec((tm, d), lambda i, k: (i, 0)),
        out_shape=jax.ShapeDtypeStruct((t_out, d), F32),
        compiler_params=_params("parallel", "arbitrary"),
        name="outproj",
    )(y, w, x, mod, mod, g.reshape(1, d))


def _ffn_kernel(x_ref, mod_ref, modc_ref, g2_ref, g3_ref, wg_ref, wu_ref, wo_ref, o_ref, u_ref,
                *, n_ctx, tiles_per_batch, nf, has_ctx):
    j = pl.program_id(1)
    first = _first_tile(tiles_per_batch, has_ctx)

    def hidden_tile(assign):
        u = u_ref[...]
        gate = jnp.dot(u, wg_ref[...], preferred_element_type=F32)
        up = jnp.dot(u, wu_ref[...], preferred_element_type=F32)
        h = (_silu(gate) * up).astype(BF16)
        part = jnp.dot(h, wo_ref[...], preferred_element_type=F32)
        if assign:
            o_ref[...] = part
        else:
            o_ref[...] += part

    @pl.when(j == 0)
    def _():
        _modulate_into(x_ref, u_ref, mod_ref, modc_ref, g2_ref, 3, first, n_ctx)
        hidden_tile(True)

    @pl.when(j > 0)
    def _():
        hidden_tile(False)

    @pl.when(j == nf - 1)
    def _():
        _gated_residual_into(o_ref, o_ref, x_ref, mod_ref, modc_ref, g3_ref, 5, first, n_ctx)


def _ffn(x, mod, g2, g3, w_in, w_out, n_ctx, tm, n_batch, has_ctx=True):
    t, d = x.shape
    f = w_out.shape[0]
    tf = FFN_TILE
    assert f == FFN_DIM and w_in.shape[1] == 2 * f and f % tf == 0
    nf = f // tf
    tpb = t // n_batch // tm
    return pl.pallas_call(
        functools.partial(_ffn_kernel, n_ctx=n_ctx, tiles_per_batch=tpb, nf=nf, has_ctx=has_ctx),
        grid=(t // tm, nf),
        in_specs=[pl.BlockSpec((tm, d), lambda i, j: (i, 0))] + _mod_specs(tpb, n_batch)
        + [pl.BlockSpec((1, d), lambda i, j: (0, 0)),
           pl.BlockSpec((1, d), lambda i, j: (0, 0)),
           pl.BlockSpec((d, tf), lambda i, j: (0, j)),
           pl.BlockSpec((d, tf), lambda i, j: (0, nf + j)),
           pl.BlockSpec((tf, d), lambda i, j: (j, 0))],
        out_specs=pl.BlockSpec((tm, d), lambda i, j: (i, 0)),
        out_shape=jax.ShapeDtypeStruct((t, d), F32),
        scratch_shapes=[pltpu.VMEM((tm, d), BF16)],
        compiler_params=_params("parallel", "arbitrary"),
        name="ffn",
    )(x, mod, mod, g2.reshape(1, d), g3.reshape(1, d), w_in, w_in, w_out)


def _rope_tables(n_ctx, n_lat):
    pos = jnp.arange(n_lat)
    row = (pos // GRID_W).astype(F32)
    col = (pos % GRID_W).astype(F32)
    inv_freq = ROPE_THETA ** (-jnp.arange(ROPE_PAIRS, dtype=F32) / ROPE_PAIRS)
    ang_row = row[:, None] * inv_freq
    ang_col = col[:, None] * inv_freq
    zero = jnp.zeros_like(ang_row)
    cos = jnp.concatenate([jnp.cos(ang_row)] * 2 + [jnp.cos(ang_col)] * 2, axis=1)
    s_lo = jnp.concatenate([-jnp.sin(ang_row), zero, -jnp.sin(ang_col), zero], axis=1)
    s_hi = jnp.concatenate([zero, jnp.sin(ang_row), zero, jnp.sin(ang_col)], axis=1)

    def full(tab, ctx_val):
        tab = jnp.concatenate([jnp.full((n_ctx, ATT_HEAD_DIM), ctx_val, F32), tab], axis=0)
        return jnp.concatenate([tab, tab], axis=1)

    return full(cos, 1.0), full(s_lo, 0.0), full(s_hi, 0.0)


def _rope(x, c, s_lo, s_hi):
    return x * c + pltpu.roll(x, LANES - ROPE_PAIRS, 1) * s_lo + pltpu.roll(x, ROPE_PAIRS, 1) * s_hi


def _attn_kernel(sink_ref, pq_ref, pk_ref, pv_ref, cq_ref, slq_ref, shq_ref, ck_ref, slk_ref, shk_ref,
                 o_ref, kx_ref, vt_ref, qt_ref, ot_ref, *, n_ctx, n_lat):
    t = pl.program_id(1)
    blk = ATT_BLOCK
    n_blocks = (n_ctx + n_lat) // blk
    ctx_blocks = n_ctx // blk
    hd = ATT_HEAD_DIM

    @pl.when(t == 0)
    def _():
        for i in range(n_blocks):
            rs = slice(i * blk, (i + 1) * blk)
            for c0 in range(0, ATT_KV_DIM, LANES):
                k = pk_ref[rs, c0:c0 + LANES].astype(F32)
                kx_ref[i, :, c0:c0 + LANES] = _rope(k, ck_ref[rs, :], slk_ref[rs, :], shk_ref[rs, :]).astype(BF16)
                vt_ref[i, c0:c0 + LANES, :] = pv_ref[rs, c0:c0 + LANES].astype(F32).T.astype(BF16)
        kx_ref[n_blocks] = jnp.zeros((blk, ATT_KV_DIM), BF16)
        vt_ref[n_blocks] = jnp.zeros((ATT_KV_DIM, blk), BF16)

    scale = ATT_HEAD_DIM ** -0.5 * LOG2_E
    for c0 in range(0, ATT_Q_DIM, LANES):
        q = pq_ref[:, 2 * ATT_KV_DIM + c0:2 * ATT_KV_DIM + c0 + LANES].astype(F32)
        qt_ref[c0:c0 + LANES, :] = (_rope(q, cq_ref[...], slq_ref[...], shq_ref[...]) * scale).T.astype(BF16)

    n = t - ctx_blocks
    key = lax.broadcasted_iota(jnp.int32, (blk, 2 * LANES), 0)
    qry = lax.broadcasted_iota(jnp.int32, (blk, 2 * LANES), 1) % blk
    blocks = [(i, None) for i in range(ctx_blocks)] + [
        (jnp.maximum(ctx_blocks + n - 1, 0), (key >= qry) & (n >= 1)),
        (jnp.maximum(ctx_blocks + n, 0), (key >= 0) & (n >= 0)),
        (ctx_blocks + n + 1, (key <= qry) & (n >= 0) & (n + 1 < n_lat // blk))]
    upper = lax.broadcasted_iota(jnp.int32, (1, 2 * LANES), 1) >= LANES
    zero = jnp.zeros((hd, LANES), BF16)

    for pair in range(ATT_KV_HEADS // 2):
        ps = slice(pair * LANES, (pair + 1) * LANES)
        for g in range(ATT_GQA):
            h0 = (2 * pair) * ATT_GQA + g
            h1 = (2 * pair + 1) * ATT_GQA + g
            w = jnp.concatenate([jnp.concatenate([qt_ref[h0 * hd:(h0 + 1) * hd, :], zero], axis=0),
                                 jnp.concatenate([zero, qt_ref[h1 * hd:(h1 + 1) * hd, :]], axis=0)], axis=1)
            sink = jnp.where(upper, sink_ref[h1], sink_ref[h0]) * LOG2_E
            scores = []
            m = sink
            for i, visible in blocks:
                s = jnp.dot(kx_ref[i, :, ps], w, preferred_element_type=F32)
                if visible is not None:
                    s = jnp.where(visible, s, NEG_INF)
                scores.append(s)
                m = jnp.maximum(m, jnp.max(s, axis=0, keepdims=True))
            den = jnp.exp2(sink - m)
            acc = jnp.zeros((LANES, 2 * LANES), F32)
            for (i, _), s in zip(blocks, scores):
                p = jnp.exp2(s - m)
                den = den + jnp.sum(p, axis=0, keepdims=True)
                acc = acc + jnp.dot(vt_ref[i, ps, :], p.astype(BF16), preferred_element_type=F32)
            ot_ref[h0 * hd:(h0 + 1) * hd, :] = acc[0:hd, 0:LANES] / den[:, 0:LANES]
            ot_ref[h1 * hd:(h1 + 1) * hd, :] = acc[hd:2 * hd, LANES:2 * LANES] / den[:, LANES:2 * LANES]

    for c0 in range(0, ATT_Q_DIM, LANES):
        o_ref[:, c0:c0 + LANES] = ot_ref[c0:c0 + LANES, :].T.astype(o_ref.dtype)


def _attention(proj, sink, tables, n_ctx, n_lat, n_batch):
    assert ATT_WINDOW == ATT_BLOCK and n_ctx % ATT_BLOCK == 0 and n_lat % ATT_BLOCK == 0
    assert proj.shape[1] == 2 * ATT_KV_DIM + ATT_Q_DIM
    rows = n_ctx + n_lat
    nq = rows // ATT_BLOCK
    cos, s_lo, s_hi = tables
    qtab = pl.BlockSpec((ATT_BLOCK, LANES), lambda b, t: (t, 0))
    ktab = pl.BlockSpec((rows, LANES), lambda b, t: (0, 0))
    return pl.pallas_call(
        functools.partial(_attn_kernel, n_ctx=n_ctx, n_lat=n_lat),
        grid=(n_batch, nq),
        in_specs=[pl.BlockSpec(memory_space=pltpu.SMEM),
                  pl.BlockSpec((ATT_BLOCK, proj.shape[1]), lambda b, t: (b * nq + t, 0)),
                  pl.BlockSpec((rows, ATT_KV_DIM), lambda b, t: (b, 0)),
                  pl.BlockSpec((rows, ATT_KV_DIM), lambda b, t: (b, 1)),
                  qtab, qtab, qtab, ktab, ktab, ktab],
        out_specs=pl.BlockSpec((ATT_BLOCK, ATT_Q_DIM), lambda b, t: (b * nq + t, 0)),
        out_shape=jax.ShapeDtypeStruct((proj.shape[0], ATT_Q_DIM), BF16),
        scratch_shapes=[pltpu.VMEM((nq + 1, ATT_BLOCK, ATT_KV_DIM), BF16),
                        pltpu.VMEM((nq + 1, ATT_KV_DIM, ATT_BLOCK), BF16),
                        pltpu.VMEM((ATT_Q_DIM, ATT_BLOCK), BF16),
                        pltpu.VMEM((ATT_Q_DIM, ATT_BLOCK), F32)],
        compiler_params=_params("parallel", "arbitrary"),
        name="attention",
    )(sink.astype(F32), proj, proj, proj, cos, s_lo, s_hi, cos, s_lo, s_hi)


def _softplus(t):
    return jnp.maximum(t, 0.0) + jnp.log(1.0 + jnp.exp(-jnp.abs(t)))


def _split_bf16(t):
    hi = t.astype(BF16)
    return hi, (t - hi.astype(F32)).astype(BF16)


def _select_rows(sel, t):
    both = jnp.dot(sel, jnp.concatenate(_split_bf16(t), axis=1), preferred_element_type=F32)
    return both[:, 0:LANES] + both[:, LANES:2 * LANES]


def _conv_silu_into(src_ref, w_ref, b_ref, dst_ref, n_ctx):
    rows, width = src_ref.shape
    t = n_ctx
    ri = lax.broadcasted_iota(jnp.int32, (t, t), 0)
    ci = lax.broadcasted_iota(jnp.int32, (t, t), 1)
    down = (ri == ci + 1).astype(BF16)
    up = (ri + 1 == ci).astype(BF16)
    row8 = lax.broadcasted_iota(jnp.int32, (8, width), 0)
    w = 0.5 * w_ref[...]
    b = 0.5 * b_ref[...]
    for r0 in range(0, rows, t):
        x = src_ref[r0:r0 + t, :]
        prv = jnp.dot(down, x, preferred_element_type=F32)
        nxt = jnp.dot(up, x, preferred_element_type=F32)
        if r0 not in (0, n_ctx):
            edge = jnp.broadcast_to(src_ref[r0 - 1:r0, :].astype(F32), (8, width))
            prv = jnp.concatenate([jnp.where(row8 == 0, edge, prv[0:8]), prv[8:]], axis=0)
        if r0 + t not in (n_ctx, rows):
            edge = jnp.broadcast_to(src_ref[r0 + t:r0 + t + 1, :].astype(F32), (8, width))
            nxt = jnp.concatenate([nxt[:t - 8], jnp.where(row8 == 7, edge, nxt[t - 8:])], axis=0)
        h = b + prv * w[0:1, :] + x.astype(F32) * w[1:2, :] + nxt * w[2:3, :]
        dst_ref[r0:r0 + t, :] = (h + h * jnp.tanh(h)).astype(dst_ref.dtype)


def _ssd_fill_tables(tr, tri_ref, expand_ref):
    q = SSD_CHUNK
    ri = lax.broadcasted_iota(jnp.int32, (tr, tr), 0)
    ci = lax.broadcasted_iota(jnp.int32, (tr, tr), 1)
    same = (ri // q) == (ci // q)
    tri_ref[0:tr, :] = (same & (ri >= ci)).astype(BF16)
    tri_ref[tr:2 * tr, :] = (same & (ri <= ci)).astype(BF16)
    src = lax.broadcasted_iota(jnp.int32, (2 * LANES, SSD_GROUP_DIM), 0) % LANES
    dst = lax.broadcasted_iota(jnp.int32, (2 * LANES, SSD_GROUP_DIM), 1) // SSD_HEAD_DIM
    for d in range(2):
        expand_ref[d] = (src == d * SSD_GROUP_HEADS + dst).astype(BF16)


def _ssd_decays(r0, tr, rev, xa, dtc, cum, expand, cumx_ref, xdec_ref):
    q = SSD_CHUNK
    npc = tr // q
    cum_hi, cum_lo = _split_bf16(cum)
    cum = cum_hi.astype(F32) + cum_lo.astype(F32)
    dt_hi, dt_lo = _split_bf16(dtc)
    wide = jnp.dot(jnp.concatenate([jnp.concatenate([cum_hi, cum_lo], axis=1),
                                    jnp.concatenate([dt_hi, dt_lo], axis=1)], axis=0),
                   expand, preferred_element_type=F32)
    cum_x = wide[0:tr]
    dt_x = wide[tr:2 * tr]
    tot_x = _rows_from(cum_x, [c * q + (0 if rev else q - 1) for c in range(npc)], q)
    xdt = xa * dt_x
    cumx_ref[pl.ds(r0, tr), :] = cum_x
    xdec_ref[pl.ds(r0, tr), :] = (xdt * jnp.exp2(tot_x - cum_x)).astype(BF16)
    return cum, cum_x, xdt


def _ssd_intra(t, tr, tables, xa_ref, ba_ref, ca_ref, dtv_ref, da_ref, y_ref, fwd_refs, bwd_refs):
    q = SSD_CHUNK
    assert 2 * q == LANES and 2 * SSD_HEAD_DIM == LANES
    tri_ref, expand_ref = tables
    r0 = pl.multiple_of(t * tr, tr)
    xa = xa_ref[pl.ds(r0, tr), :]
    bc = ba_ref[pl.ds(r0, tr), :]
    cc = ca_ref[pl.ds(r0, tr), :]
    dtc = dtv_ref[pl.ds(r0, tr), :]
    cums = _select_rows(tri_ref[...], da_ref[pl.ds(r0, tr), :])
    scans = [(False, 0) + _ssd_decays(r0, tr, False, xa, dtc, cums[0:tr], expand_ref[0], *fwd_refs),
             (True, SSD_GROUP_HEADS) + _ssd_decays(r0, tr, True, xa, dtc, cums[tr:2 * tr], expand_ref[1], *bwd_refs)]

    row = lax.broadcasted_iota(jnp.int32, (q, LANES), 0)
    lane = lax.broadcasted_iota(jnp.int32, (q, LANES), 1)
    upper = lane >= q
    key = jnp.where(upper, lane - q, lane)
    for c in range(tr // q):
        cs = slice(c * q, (c + 1) * q)
        cb2 = lax.dot_general(cc[cs], jnp.concatenate([bc[cs], bc[cs]], axis=0), _NT, preferred_element_type=F32)
        cum_rows = [jnp.concatenate([cum[cs], pltpu.roll(cum[cs], LANES - 1, 1)], axis=0).T
                    for _, _, cum, _, _ in scans]
        ys = []
        for pr in range(SSD_GROUP_HEADS // 2):
            sl = slice(pr * LANES, (pr + 1) * LANES)
            weights, values = [], []
            for (rev, lane0, _, cum_x, xdt), rows in zip(scans, cum_rows):
                le = lane0 + 2 * pr
                seen = (row <= key) if rev else (row >= key)
                decay = jnp.exp2(jnp.where(seen, cum_x[cs, sl] - rows[le:le + 1, :], NEG_INF))
                weights.append((cb2 * decay).astype(BF16))
                xp = xdt[cs, sl]
                values.append(jnp.concatenate([jnp.where(upper, 0.0, xp), jnp.where(upper, xp, 0.0)],
                                              axis=0).astype(BF16))
            ys.append(jnp.dot(jnp.concatenate(weights, axis=1), jnp.concatenate(values, axis=0),
                              preferred_element_type=F32))
        y_ref[pl.ds(r0 + c * q, q), :] = jnp.concatenate(ys, axis=1)


def _ssd_state_step(c, rev, ba_ref, ca_ref, y_ref, cumx_ref, xdec_ref, s_ref):
    q = SSD_CHUNK
    r0 = pl.multiple_of(c * q, q)
    cum_x = cumx_ref[pl.ds(r0, q), :]
    tot_x = cum_x[0:1, :] if rev else cum_x[q - 1:q, :]
    state = s_ref[...]
    y_ref[pl.ds(r0, q), :] += (jnp.dot(ca_ref[pl.ds(r0, q), :], state.astype(BF16), preferred_element_type=F32)
                               * jnp.exp2(cum_x))
    upd = lax.dot_general(ba_ref[pl.ds(r0, q), :], xdec_ref[pl.ds(r0, q), :], _TN, preferred_element_type=F32)
    s_ref[...] = state * jnp.exp2(tot_x) + upd


def _ssd_kernel(x_ref, z_ref, b_ref, c_ref, dt_ref, cwx_ref, cwb_ref, cwc_ref, cbx_ref, cbb_ref, cbc_ref,
                dtb_ref, alog_ref, dskip_ref, nw_ref, o_ref,
                xa_ref, ba_ref, ca_ref, dtv_ref, da_ref, y_ref, cumxf_ref, cumxb_ref, xdecf_ref, xdecb_ref,
                s_ref, sb_ref, tri_ref, expand_ref, *, n_ctx, n_lat, tile):
    _conv_silu_into(x_ref, cwx_ref, cbx_ref, xa_ref, n_ctx)
    _conv_silu_into(b_ref, cwb_ref, cbb_ref, ba_ref, n_ctx)
    _conv_silu_into(c_ref, cwc_ref, cbc_ref, ca_ref, n_ctx)
    dtv = _softplus(dt_ref[...].astype(F32) + dtb_ref[...])
    dtv_ref[...] = dtv
    da_ref[...] = dtv * (-jnp.exp(alog_ref[...]) * LOG2_E)

    seq = (xa_ref, ba_ref, ca_ref, dtv_ref, da_ref)
    fwd = (cumxf_ref, xdecf_ref)
    bwd = (cumxb_ref, xdecb_ref)

    tables = (tri_ref, expand_ref)
    _ssd_fill_tables(tile, *tables)

    def intra_step(t, carry):
        _ssd_intra(t, tile, tables, *seq, y_ref, fwd, bwd)
        return carry

    lax.fori_loop(0, (n_ctx + n_lat) // tile, intra_step, 0, unroll=TILE_UNROLL)

    n_c = n_ctx // SSD_CHUNK
    n_l = n_lat // SSD_CHUNK

    def ctx_step(s, carry):
        _ssd_state_step(s, False, ba_ref, ca_ref, y_ref, *fwd, s_ref)
        _ssd_state_step(n_c - 1 - s, True, ba_ref, ca_ref, y_ref, *bwd, sb_ref)
        return carry

    def lat_step(s, carry):
        _ssd_state_step(n_c + s, False, ba_ref, ca_ref, y_ref, *fwd, s_ref)
        _ssd_state_step(n_c + n_l - 1 - s, True, ba_ref, ca_ref, y_ref, *bwd, sb_ref)
        return carry

    s_ref[...] = jnp.zeros(s_ref.shape, F32)
    sb_ref[...] = jnp.zeros(sb_ref.shape, F32)
    lax.fori_loop(0, n_c, ctx_step, 0, unroll=SCAN_UNROLL)
    lax.fori_loop(0, n_l, lat_step, 0, unroll=SCAN_UNROLL)

    for r0 in range(0, n_ctx + n_lat, n_ctx):
        y = y_ref[r0:r0 + n_ctx, :] + xa_ref[r0:r0 + n_ctx, :] * dskip_ref[...]
        t = y * _silu(z_ref[r0:r0 + n_ctx, :].astype(F32))
        o_ref[r0:r0 + n_ctx, :] = (_rms_rows(t) * nw_ref[...]).astype(o_ref.dtype)


def _ssd_layout(w_in, conv_w, conv_b, dt_bias, a_log, d_skip):
    n_state = SSD_CONV_DIM + 2 * SSD_HEADS
    gh = SSD_GROUP_HEADS

    def dt_blocks(t):
        lead = t.shape[:-1]
        t = t.reshape(lead + (2, SSD_GROUPS, gh))
        t = jnp.moveaxis(t, -3, -2).reshape(lead + (SSD_GROUPS, 2 * gh))
        t = jnp.pad(t, [(0, 0)] * (len(lead) + 1) + [(0, LANES - 2 * gh)])
        return t.reshape(lead + (SSD_GROUPS * LANES,))

    w_in = w_in.astype(BF16)
    w = jnp.concatenate([w_in[:, :SSD_D_INNER], w_in[:, n_state:], w_in[:, SSD_D_INNER:SSD_CONV_DIM],
                         dt_blocks(w_in[:, SSD_CONV_DIM:n_state])], axis=1)
    return (w, conv_w, conv_b.reshape(1, -1), dt_blocks(dt_bias.reshape(-1)).reshape(1, -1),
            dt_blocks(a_log.reshape(-1)).reshape(1, -1), jnp.repeat(d_skip, SSD_HEAD_DIM).reshape(1, -1))


def _ssd(proj, conv_w, conv_b, dt_bias, a_log, d_skip, norm_w, n_ctx, n_lat, n_batch):
    rows = n_ctx + n_lat
    gd = SSD_GROUP_DIM
    tile = _scan_tile(rows, SSD_CHUNK)
    xb = SSD_D_INNER // gd
    bb = 2 * SSD_D_INNER // LANES
    sec = lambda width, off: pl.BlockSpec((rows, width), lambda b, g: (b, off + g))
    par = lambda r, width, off: pl.BlockSpec((r, width), lambda b, g: (0, off + g))
    return pl.pallas_call(
        functools.partial(_ssd_kernel, n_ctx=n_ctx, n_lat=n_lat, tile=tile),
        grid=(n_batch, SSD_GROUPS),
        in_specs=[sec(gd, 0), sec(gd, xb), sec(LANES, bb), sec(LANES, bb + SSD_GROUPS),
                  sec(LANES, bb + 2 * SSD_GROUPS),
                  par(3, gd, 0), par(3, LANES, SSD_D_INNER // LANES), par(3, LANES, SSD_D_INNER // LANES + SSD_GROUPS),
                  par(1, gd, 0), par(1, LANES, SSD_D_INNER // LANES), par(1, LANES, SSD_D_INNER // LANES + SSD_GROUPS),
                  par(1, LANES, 0), par(1, LANES, 0), par(1, gd, 0), par(1, gd, 0)],
        out_specs=pl.BlockSpec((rows, gd), lambda b, g: (b, g)),
        out_shape=jax.ShapeDtypeStruct((proj.shape[0], SSD_D_INNER), BF16),
        scratch_shapes=[pltpu.VMEM((rows, gd), F32),
                        pltpu.VMEM((rows, SSD_STATE), BF16),
                        pltpu.VMEM((rows, SSD_STATE), BF16),
                        pltpu.VMEM((rows, LANES), F32),
                        pltpu.VMEM((rows, LANES), F32),
                        pltpu.VMEM((rows, gd), F32),
                        pltpu.VMEM((rows, gd), F32),
                        pltpu.VMEM((rows, gd), F32),
                        pltpu.VMEM((rows, gd), BF16),
                        pltpu.VMEM((rows, gd), BF16),
                        pltpu.VMEM((SSD_STATE, gd), F32),
                        pltpu.VMEM((SSD_STATE, gd), F32),
                        pltpu.VMEM((2 * tile, tile), BF16),
                        pltpu.VMEM((2, 2 * LANES, gd), BF16)],
        compiler_params=_params("parallel", "parallel"),
        name="ssd",
    )(proj, proj, proj, proj, proj, conv_w, conv_w, conv_w, conv_b, conv_b, conv_b,
      dt_bias, a_log, d_skip, norm_w.reshape(1, -1))


HG_OWN, HG_EARLIER, HG_LATER = 1.0, 2.0, 3.0


def _hgrn_fill_tables(tr, tri_ref, kind_ref, place_ref):
    ch, sub = HG_CHUNK, HG_SUB
    ri = lax.broadcasted_iota(jnp.int32, (tr, tr), 0)
    ci = lax.broadcasted_iota(jnp.int32, (tr, tr), 1)
    same = (ri // ch) == (ci // ch)
    tri_ref[0] = (same & (ri >= ci)).astype(BF16)
    tri_ref[1] = (same & (ri <= ci)).astype(BF16)
    rb, cb = ri // sub, ci // sub
    kind_ref[...] = jnp.where(rb == cb, HG_OWN,
                              jnp.where(same & (cb < rb), HG_EARLIER, jnp.where(same & (cb > rb), HG_LATER, 0.0)))
    src_key = lax.broadcasted_iota(jnp.int32, (sub * LANES, LANES), 0) // LANES
    dst_key = lax.broadcasted_iota(jnp.int32, (sub * LANES, LANES), 1) % sub
    place_ref[...] = (src_key == dst_key).astype(BF16)


def _hgrn_intra(t, tr, rev, tables, lg_ref, k_ref, q_ref, v_ref, o_ref, qst_ref, upd_ref, dec_ref):
    ch, sub = HG_CHUNK, HG_SUB
    npc, nb, nblk = tr // ch, ch // sub, tr // sub
    tri_ref, kind_ref, place_ref = tables
    r0 = pl.multiple_of(t * tr, tr)
    lg = lg_ref[pl.ds(r0, tr), :]
    k = k_ref[pl.ds(r0, tr), :]
    q = q_ref[pl.ds(r0, tr), :]
    v = v_ref[pl.ds(r0, tr), :]
    tri = tri_ref[1 if rev else 0]
    cum = _select_rows(tri, lg)

    last = [c * ch + (0 if rev else ch - 1) for c in range(npc)]
    qst_ref[pl.ds(r0, tr), :] = (q * jnp.exp2(cum)).astype(BF16)
    k_end = (k * jnp.exp2(_rows_from(cum, last, ch) - cum)).astype(BF16)
    zero_chunk = jnp.zeros((ch, LANES), BF16)
    k_end_by_chunk = jnp.concatenate(
        [jnp.concatenate([k_end[c2 * ch:(c2 + 1) * ch] if c2 == c else zero_chunk for c2 in range(npc)], axis=0)
         for c in range(npc)], axis=1)
    upd = lax.dot_general(v, k_end_by_chunk, _TN, preferred_element_type=F32)
    for c in range(npc):
        chunk = t * npc + c
        dec_ref[pl.ds(chunk, 1), :] = jnp.exp2(cum[last[c]:last[c] + 1, :])
        upd_ref[pl.ds(pl.multiple_of(chunk * LANES, LANES), LANES), :] = upd[:, c * LANES:(c + 1) * LANES]

    def own_edge(j):
        i = j % nb
        if rev:
            return None if i == nb - 1 else (j + 1) * sub
        return None if i == 0 else j * sub - 1

    qs = q * jnp.exp2(cum - _rows_from(cum, [own_edge(j) for j in range(nblk)], sub))
    queries, keys = {}, {}
    for i in (range(nb - 1) if rev else range(1, nb)):
        queries[i] = jnp.concatenate([qs[(c * nb + i) * sub:(c * nb + i + 1) * sub] for c in range(npc)],
                                     axis=0).astype(BF16)
        pieces = []
        for c in range(npc):
            base = c * ch
            e = base + ((i + 1) * sub if rev else i * sub - 1)
            lo, hi = ((i + 1) * sub, ch) if rev else (0, i * sub)
            part = k[base + lo:base + hi] * jnp.exp2(cum[e:e + 1, :] - cum[base + lo:base + hi])
            pad = jnp.zeros((ch - (hi - lo), LANES), F32)
            pieces += [pad, part] if rev else [part, pad]
        keys[i] = jnp.concatenate(pieces, axis=0).astype(BF16)
    weights = {}
    order = sorted(queries)
    nq = npc * sub
    zero_q = jnp.zeros((nq, LANES), BF16)
    for i1, i2 in zip(order[0::2], order[1::2] + [None]):
        if i2 is None:
            weights[i1] = lax.dot_general(queries[i1], keys[i1], _NT, preferred_element_type=F32)
            continue
        both = lax.dot_general(
            jnp.concatenate([jnp.concatenate([queries[i1], zero_q], axis=1),
                             jnp.concatenate([zero_q, queries[i2]], axis=1)], axis=0),
            jnp.concatenate([keys[i1], keys[i2]], axis=1), _NT, preferred_element_type=F32)
        weights[i1], weights[i2] = both[0:nq], both[nq:2 * nq]
    zero_rows = jnp.zeros((sub, tr), F32)
    a_off = jnp.concatenate([weights[j % nb][(j // nb) * sub:(j // nb + 1) * sub] if j % nb in weights
                             else zero_rows for j in range(nblk)], axis=0)

    row8 = lax.broadcasted_iota(jnp.int32, (sub, LANES), 0)
    prods = []
    for s in range(sub):
        idx = [j * sub + s for j in range(nblk)]
        unseen = jnp.where((row8 <= s) if rev else (row8 >= s), 0.0, NEG_INF)
        w = jnp.exp2(cum - _rows_from(cum, idx, sub) + jnp.concatenate([unseen] * nblk, axis=0))
        prods.append((q * _rows_from(k, idx, sub) * w).astype(BF16))
    a_own = jnp.dot(jnp.concatenate(prods, axis=1), place_ref[...], preferred_element_type=F32)
    kind = kind_ref[...]
    a = jnp.where(kind == HG_OWN, jnp.concatenate([a_own] * (tr // LANES), axis=1),
                  jnp.where(kind == (HG_LATER if rev else HG_EARLIER), a_off, 0.0))
    o_ref[pl.ds(r0, tr), :] = jnp.dot(a.astype(BF16), v, preferred_element_type=F32)


def _hgrn_state_step(c, st, upd_ref, dec_ref, sin_ref):
    rows = pl.ds(pl.multiple_of(c * LANES, LANES), LANES)
    sin_ref[rows, :] = st.astype(BF16)
    return st * dec_ref[pl.ds(c, 1), :] + upd_ref[rows, :]


def _hgrn_readout(c, qst_ref, sin_ref):
    ch = HG_CHUNK
    return lax.dot_general(qst_ref[pl.ds(pl.multiple_of(c * ch, ch), ch), :],
                           sin_ref[pl.ds(pl.multiple_of(c * LANES, LANES), LANES), :], _NT,
                           preferred_element_type=F32)


def _hgrn_kernel(ff_ref, fb_ref, v_ref, q_ref, gate_ref, lb_ref, nw_ref, out_ref,
                 lgf_ref, lgb_ref, kf_ref, kb_ref, qa_ref, o_ref, ob_ref, qstf_ref, qstb_ref,
                 updf_ref, updb_ref, decf_ref, decb_ref, sinf_ref, sinb_ref, tri_ref, kind_ref, place_ref,
                 *, n_ctx, n_lat, tile):
    tables = (tri_ref, kind_ref, place_ref)
    _hgrn_fill_tables(tile, *tables)
    for d, (f_ref, lg_ref, k_ref) in enumerate(((ff_ref, lgf_ref, kf_ref), (fb_ref, lgb_ref, kb_ref))):
        lb = lb_ref[d:d + 1, :]
        g = lb + (1.0 - lb) * jax.nn.sigmoid(f_ref[...].astype(F32))
        lg_ref[...] = jnp.log(g) * LOG2_E
        k_ref[...] = 1.0 - g
    qa_ref[...] = _silu(q_ref[...].astype(F32))

    fwd = (qstf_ref, updf_ref, decf_ref)
    bwd = (qstb_ref, updb_ref, decb_ref)
    fwd_scan = (updf_ref, decf_ref, sinf_ref)
    bwd_scan = (updb_ref, decb_ref, sinb_ref)

    def intra_step(t, carry):
        _hgrn_intra(t, tile, False, tables, lgf_ref, kf_ref, qa_ref, v_ref, o_ref, *fwd)
        _hgrn_intra(t, tile, True, tables, lgb_ref, kb_ref, qa_ref, v_ref, ob_ref, *bwd)
        return carry

    lax.fori_loop(0, (n_ctx + n_lat) // tile, intra_step, 0, unroll=TILE_UNROLL)

    n_c = n_ctx // HG_CHUNK
    n_l = n_lat // HG_CHUNK

    def ctx_step(s, carry):
        return (_hgrn_state_step(s, carry[0], *fwd_scan),
                _hgrn_state_step(n_c - 1 - s, carry[1], *bwd_scan))

    def lat_step(s, carry):
        return (_hgrn_state_step(n_c + s, carry[0], *fwd_scan),
                _hgrn_state_step(n_c + n_l - 1 - s, carry[1], *bwd_scan))

    zero = jnp.zeros((LANES, LANES), F32)
    carry = lax.fori_loop(0, n_c, ctx_step, (zero, zero), unroll=SCAN_UNROLL)
    lax.fori_loop(0, n_l, lat_step, carry, unroll=SCAN_UNROLL)

    def read_step(c, carry):
        rows = pl.ds(pl.multiple_of(c * HG_CHUNK, HG_CHUNK), HG_CHUNK)
        o_ref[rows, :] += _hgrn_readout(c, qstf_ref, sinf_ref) + _hgrn_readout(c, qstb_ref, sinb_ref)
        return carry

    lax.fori_loop(0, n_c + n_l, read_step, 0, unroll=READ_UNROLL)

    o = o_ref[...] + ob_ref[...]
    out_ref[...] = (_rms_rows(o) * nw_ref[...] * _silu(gate_ref[...].astype(F32))).astype(out_ref.dtype)


def _hgrn(proj, lower, norm_w, n_ctx, n_lat, n_batch):
    rows = n_ctx + n_lat
    sec = lambda off: pl.BlockSpec((rows, LANES), lambda b, h: (b, off * HG_HEADS + h))
    seq = lambda dt: pltpu.VMEM((rows, LANES), dt)
    n_chunks = rows // HG_CHUNK
    upd = pltpu.VMEM((n_chunks * LANES, LANES), F32)
    dec = pltpu.VMEM((-(-n_chunks // 8) * 8, LANES), F32)
    sin = pltpu.VMEM((n_chunks * LANES, LANES), BF16)
    tile = _scan_tile(rows, HG_CHUNK)
    tables = [pltpu.VMEM((2, tile, tile), BF16), pltpu.VMEM((tile, tile), F32),
              pltpu.VMEM((HG_SUB * LANES, LANES), BF16)]
    return pl.pallas_call(
        functools.partial(_hgrn_kernel, n_ctx=n_ctx, n_lat=n_lat, tile=tile),
        grid=(n_batch, HG_HEADS),
        in_specs=[sec(0), sec(1), sec(2), sec(3), sec(4),
                  pl.BlockSpec((2, LANES), lambda b, h: (0, h)),
                  pl.BlockSpec((1, LANES), lambda b, h: (0, h))],
        out_specs=pl.BlockSpec((rows, LANES), lambda b, h: (b, h)),
        out_shape=jax.ShapeDtypeStruct((proj.shape[0], HG_VAL), BF16),
        scratch_shapes=[seq(F32), seq(F32), seq(F32), seq(F32), seq(F32), seq(F32), seq(F32),
                        seq(BF16), seq(BF16), upd, upd, dec, dec, sin, sin] + tables,
        compiler_params=_params("parallel", "parallel"),
        name="hgrn2",
    )(proj, proj, proj, proj, proj, lower, norm_w.reshape(1, -1))


def _hgrn_lower_bounds(lb_raw):
    p = jax.nn.softmax(lb_raw.astype(F32), axis=1)
    return jnp.cumsum(p, axis=1) - p[:, :1]


def kernel(x, c, ctx, c_ctx, ada_w, ada_b, norm_g, ffn_w_in, ffn_w_out, attn_w_in, attn_w_out, attn_sink,
           ssd_w_in, ssd_conv_w, ssd_conv_b, ssd_dt_bias, ssd_a_log, ssd_d, ssd_norm_w, ssd_w_out,
           hgrn_w_in, hgrn_lb, hgrn_norm_w, hgrn_w_out):
    n_batch, n_lat, d = x.shape
    n_ctx = ctx.shape[1]
    rows = n_ctx + n_lat
    tm = _token_tile(n_ctx, n_ctx + n_lat)
    tm_lat = _token_tile(n_ctx, n_lat)

    r = jnp.concatenate([ctx, x], axis=1).reshape(n_batch * rows, d)
    n_cond = -(-(n_batch + 1) // 8) * 8
    c_rows = jnp.concatenate([c, c_ctx[None, :], jnp.zeros((n_cond - n_batch - 1, d), F32)], axis=0)
    mods = _ada_mods(c_rows, ada_w, ada_b)
    tables = _rope_tables(n_ctx, n_lat)
    lower = _hgrn_lower_bounds(hgrn_lb)

    for i in range(DEPTH):
        kind, j = i % N_MIXERS, i // N_MIXERS
        mod = mods[i]
        if kind == 0:
            proj = _modproj(r, mod, norm_g[i, 0], attn_w_in[j].astype(BF16), n_ctx, tm, n_batch)
            y = _attention(proj, attn_sink[j], tables, n_ctx, n_lat, n_batch)
            w_out = attn_w_out[j]
        elif kind == 1:
            w_in, conv_w, conv_b, dt_bias, a_log, d_skip = _ssd_layout(
                ssd_w_in[j], ssd_conv_w[j], ssd_conv_b[j], ssd_dt_bias[j], ssd_a_log[j], ssd_d[j])
            proj = _modproj(r, mod, norm_g[i, 0], w_in, n_ctx, tm, n_batch)
            y = _ssd(proj, conv_w, conv_b, dt_bias, a_log, d_skip, ssd_norm_w[j], n_ctx, n_lat, n_batch)
            w_out = ssd_w_out[j]
        else:
            proj = _modproj(r, mod, norm_g[i, 0], hgrn_w_in[j].astype(BF16), n_ctx, tm, n_batch)
            y = _hgrn(proj, lower[:, i], hgrn_norm_w[j], n_ctx, n_lat, n_batch)
            w_out = hgrn_w_out[j]
        ffn_w = (ffn_w_in[i].astype(BF16), ffn_w_out[i].astype(BF16))
        if i < DEPTH - 1:
            r = _outproj(y, w_out.astype(BF16), r, mod, norm_g[i, 1], n_ctx, tm, n_batch)
            r = _ffn(r, mod, norm_g[i, 2], norm_g[i, 3], *ffn_w, n_ctx, tm, n_batch)
        else:
            r = _outproj(y, w_out.astype(BF16), r, mod, norm_g[i, 1], n_ctx, tm_lat, n_batch, lat_rows=n_lat)
            r = _ffn(r, mod, norm_g[i, 2], norm_g[i, 3], *ffn_w, n_ctx, tm_lat, n_batch, has_ctx=False)
    return r.reshape(n_batch, n_lat, d)
```

```python
import functools
import math

import jax
import jax.numpy as jnp
from jax import lax
from jax.experimental import pallas as pl
from jax.experimental.pallas import tpu as pltpu

F32 = jnp.float32
BF16 = jnp.bfloat16

D_MODEL = 2048
DEPTH = 4
N_MIXERS = 3
GRID_W = 64
RMS_EPS = 1e-6
N_MOD = 6

ATT_HEADS = 32
ATT_KV_HEADS = 4
ATT_GQA = ATT_HEADS // ATT_KV_HEADS
ATT_HEAD_DIM = 64
ATT_WINDOW = 128
ATT_BLOCK = 128
ATT_Q_DIM = ATT_HEADS * ATT_HEAD_DIM
ATT_KV_DIM = ATT_KV_HEADS * ATT_HEAD_DIM
ROPE_THETA = 10000.0
ROPE_PAIRS = ATT_HEAD_DIM // 4
NEG_INF = -1e30
LOG2_E = 1.0 / math.log(2.0)

SSD_D_INNER = 2 * D_MODEL
SSD_HEAD_DIM = 64
SSD_HEADS = SSD_D_INNER // SSD_HEAD_DIM
SSD_GROUPS = 8
SSD_GROUP_HEADS = SSD_HEADS // SSD_GROUPS
SSD_GROUP_DIM = SSD_D_INNER // SSD_GROUPS
SSD_STATE = 128
SSD_GN = SSD_GROUPS * SSD_STATE
SSD_CONV_DIM = SSD_D_INNER + 2 * SSD_GN
SSD_CHUNK = 64

HG_EXPAND = 128
HG_HEADS = D_MODEL // HG_EXPAND
HG_VAL = D_MODEL
HG_CHUNK = 64
HG_SUB = 8

FFN_DIM = -(-8 * D_MODEL // (3 * 256)) * 256

LANES = 128
MXU_TILE = 256
VMEM_LIMIT_BYTES = 56 * 1024 * 1024
MAX_TOKEN_TILE = 768
MAX_PROJ_TILE = 2816
FFN_TILE = 512
SCAN_UNROLL = 4
READ_UNROLL = 12
TILE_UNROLL = 3


def _params(*sem):
    return pltpu.CompilerParams(dimension_semantics=sem, vmem_limit_bytes=VMEM_LIMIT_BYTES)


def _token_tile(n_ctx, rows):
    best = n_ctx
    for mult in range(1, rows // n_ctx + 1):
        t = mult * n_ctx
        if rows % t == 0 and t <= MAX_TOKEN_TILE:
            best = t
    assert rows % best == 0
    return best


def _silu(t):
    h = 0.5 * t
    return h + h * jnp.tanh(h)


def _rms_rows(t):
    return t * lax.rsqrt(jnp.mean(t * t, axis=-1, keepdims=True) + RMS_EPS)


def _rows_from(t, idx, height):
    width = t.shape[1]
    return jnp.concatenate([jnp.zeros((height, width), t.dtype) if r is None
                            else jnp.broadcast_to(t[r:r + 1, :], (height, width)) for r in idx], axis=0)


_NT = (((1,), (1,)), ((), ()))
_TN = (((0,), (0,)), ((), ()))


def _scan_tile(rows, chunk):
    return next(t for t in (4 * chunk, 2 * chunk) if rows % t == 0 and t % LANES == 0)


def _ada_kernel(c_ref, w_ref, b_ref, o_ref):
    s = _silu(c_ref[...]).astype(BF16)
    o_ref[...] = jnp.dot(s, w_ref[...].astype(BF16), preferred_element_type=F32) + b_ref[...]


def _ada_mods(c_rows, ada_w, ada_b):
    r = c_rows.shape[0]
    n = ada_w.shape[-1]
    tn = 1024
    out = pl.pallas_call(
        _ada_kernel,
        grid=(DEPTH, n // tn),
        in_specs=[pl.BlockSpec((r, D_MODEL), lambda l, j: (0, 0)),
                  pl.BlockSpec((None, D_MODEL, tn), lambda l, j: (l, 0, j)),
                  pl.BlockSpec((None, 1, tn), lambda l, j: (l, 0, j))],
        out_specs=pl.BlockSpec((None, r, tn), lambda l, j: (l, 0, j)),
        out_shape=jax.ShapeDtypeStruct((DEPTH, r, n), F32),
        compiler_params=_params("parallel", "parallel"),
        name="ada_mods",
    )(c_rows, ada_w, ada_b.reshape(DEPTH, 1, n))
    return out.reshape(DEPTH, r, N_MOD, D_MODEL)


def _first_tile(tiles_per_batch, has_ctx):
    return (pl.program_id(0) % tiles_per_batch == 0) if has_ctx else None


def _mod_row(mod_ref, modc_ref, row, first, r0):
    m = mod_ref[row:row + 1, :]
    if r0 == 0 and first is not None:
        m = jnp.where(first, modc_ref[row:row + 1, :], m)
    return m


def _tile_rows(srcs, n_ctx, first):
    if len(srcs) == 1:
        return lambda r0: srcs[0][r0:r0 + n_ctx, :]

    def rows(r0):
        block = srcs[1 + r0 // n_ctx][...]
        return jnp.where(first, srcs[0][...], block) if r0 == 0 else block

    return rows


def _modulate_into(rows, u_ref, mod_ref, modc_ref, g_ref, shift_row, first, n_ctx):
    g = g_ref[...]
    for r0 in range(0, u_ref.shape[0], n_ctx):
        shift = _mod_row(mod_ref, modc_ref, shift_row, first, r0)
        scale = _mod_row(mod_ref, modc_ref, shift_row + 1, first, r0)
        u_ref[r0:r0 + n_ctx, :] = (_rms_rows(rows(r0)) * (g * (1.0 + scale)) + shift).astype(u_ref.dtype)


def _gated_residual_into(o_ref, f_src, rows, mod_ref, modc_ref, g_ref, gate_row, first, n_ctx):
    g = g_ref[...]
    for r0 in range(0, o_ref.shape[0], n_ctx):
        gate = _mod_row(mod_ref, modc_ref, gate_row, first, r0)
        f = f_src[r0:r0 + n_ctx, :]
        o_ref[r0:r0 + n_ctx, :] = rows(r0) + gate * (_rms_rows(f) * g)


def _split_source_specs(n_ctx, n_lat, tm, d):
    per_tile = tm // n_ctx
    tpb = (n_ctx + n_lat) // tm
    lat_blocks = n_lat // n_ctx
    specs = [pl.BlockSpec((n_ctx, d), lambda i, j: (i // tpb, 0))]
    for blk in range(per_tile):
        specs.append(pl.BlockSpec(
            (n_ctx, d), lambda i, j, blk=blk: ((i // tpb) * lat_blocks
                                               + jnp.maximum((i % tpb) * per_tile + blk - 1, 0), 0)))
    return specs


def _mod_specs(tiles_per_batch, n_batch):
    return [pl.BlockSpec((None, N_MOD, D_MODEL), lambda i, j: (i // tiles_per_batch, 0, 0)),
            pl.BlockSpec((None, N_MOD, D_MODEL), lambda i, j: (n_batch, 0, 0))]


def _modproj_kernel(*refs, n_ctx, tiles_per_batch, n_src, has_ctx=True):
    srcs = refs[:n_src]
    mod_ref, modc_ref, g_ref, w_ref, o_ref, u_ref = refs[n_src:]
    first = _first_tile(tiles_per_batch, has_ctx)

    @pl.when(pl.program_id(1) == 0)
    def _():
        _modulate_into(_tile_rows(srcs, n_ctx, first), u_ref, mod_ref, modc_ref, g_ref, 0, first, n_ctx)

    tn = o_ref.shape[1]
    cut = -(-tn // (2 * MXU_TILE)) * MXU_TILE
    for lo, hi in ((0, cut), (cut, tn)):
        if hi > lo:
            o_ref[:, lo:hi] = jnp.dot(u_ref[...], w_ref[:, lo:hi], preferred_element_type=F32).astype(o_ref.dtype)


def _stream_sources(x, n_ctx, tm, n_batch):
    if not isinstance(x, tuple):
        return [x], [pl.BlockSpec((tm, x.shape[1]), lambda i, j: (i, 0))], x.shape[0]
    ctx, lat = x
    d = ctx.shape[1]
    n_lat = lat.shape[0] // n_batch
    specs = _split_source_specs(n_ctx, n_lat, tm, d)
    return [ctx] + [lat] * (len(specs) - 1), specs, ctx.shape[0] + lat.shape[0]


def _modproj(x, mod, g, w, n_ctx, tm, n_batch):
    d, n = w.shape
    arrays, specs, t = _stream_sources(x, n_ctx, tm, n_batch)
    tn = max(c for c in range(LANES, MAX_PROJ_TILE + 1, LANES) if n % c == 0)
    tpb = t // n_batch // tm
    return pl.pallas_call(
        functools.partial(_modproj_kernel, n_ctx=n_ctx, tiles_per_batch=tpb, n_src=len(arrays)),
        grid=(t // tm, n // tn),
        in_specs=specs + _mod_specs(tpb, n_batch)
        + [pl.BlockSpec((1, d), lambda i, j: (0, 0)),
           pl.BlockSpec((d, tn), lambda i, j: (0, j))],
        out_specs=pl.BlockSpec((tm, tn), lambda i, j: (i, j)),
        out_shape=jax.ShapeDtypeStruct((t, n), BF16),
        scratch_shapes=[pltpu.VMEM((tm, d), BF16)],
        compiler_params=_params("parallel", "arbitrary"),
        name="modproj",
    )(*arrays, mod, mod, g.reshape(1, d), w)


def _outproj_kernel(y_ref, w_ref, *refs, n_ctx, tiles_per_batch, nk, has_ctx, n_src):
    srcs = refs[:n_src]
    mod_ref, modc_ref, g_ref, o_ref = refs[n_src:]
    k = pl.program_id(1)
    first = _first_tile(tiles_per_batch, has_ctx)
    x_rows = _tile_rows(srcs, n_ctx, first)
    part = jnp.dot(y_ref[...], w_ref[...], preferred_element_type=F32)
    if nk == 1:
        _gated_residual_into(o_ref, part, x_rows, mod_ref, modc_ref, g_ref, 2, first, n_ctx)
        return

    @pl.when(k == 0)
    def _():
        o_ref[...] = part

    @pl.when(k > 0)
    def _():
        o_ref[...] += part

    @pl.when(k == nk - 1)
    def _():
        _gated_residual_into(o_ref, o_ref, x_rows, mod_ref, modc_ref, g_ref, 2, first, n_ctx)


def _outproj(y, w, x, mod, g, n_ctx, tm, n_batch, lat_rows=None):
    kdim, d = w.shape
    tk = min(kdim, 2048)
    nk = kdim // tk
    if lat_rows is None:
        x_arrays, x_specs, t = _stream_sources(x, n_ctx, tm, n_batch)
        tpb, t_out = t // n_batch // tm, t
        y_spec = pl.BlockSpec((tm, tk), lambda i, k: (i, k))
    else:
        t = x.shape[0]
        rows = t // n_batch
        x_arrays = [x]
        tpb, t_out = lat_rows // tm, n_batch * lat_rows
        assert rows % n_ctx == 0 and lat_rows % n_ctx == 0 and tm % n_ctx == 0
        first_row = lambda i: pl.multiple_of(
            ((i // tpb) * (rows // n_ctx) + (rows - lat_rows) // n_ctx + (i % tpb) * (tm // n_ctx)) * n_ctx, n_ctx)
        y_spec = pl.BlockSpec((pl.Element(tm), pl.Element(tk)), lambda i, k: (first_row(i), k * tk))
        x_specs = [pl.BlockSpec((pl.Element(tm), pl.Element(d)), lambda i, k: (first_row(i), 0))]
    return pl.pallas_call(
        functools.partial(_outproj_kernel, n_ctx=n_ctx, tiles_per_batch=tpb, nk=nk, has_ctx=lat_rows is None,
                          n_src=len(x_arrays)),
        grid=(t_out // tm, nk),
        in_specs=[y_spec, pl.BlockSpec((tk, d), lambda i, k: (k, 0))] + x_specs + _mod_specs(tpb, n_batch)
        + [pl.BlockSpec((1, d), lambda i, k: (0, 0))],
        out_specs=pl.BlockSpec((tm, d), lambda i, k: (i, 0)),
        out_shape=jax.ShapeDtypeStruct((t_out, d), F32),
        compiler_params=_params("parallel", "arbitrary"),
        name="outproj",
    )(y, w, *x_arrays, mod, mod, g.reshape(1, d))


def _ffn_kernel(x_ref, mod_ref, modc_ref, g2_ref, g3_ref, wg_ref, wu_ref, wo_ref, o_ref, u_ref,
                *, n_ctx, tiles_per_batch, nf, has_ctx):
    j = pl.program_id(1)
    first = _first_tile(tiles_per_batch, has_ctx)

    def hidden_tile(assign):
        u = u_ref[...]
        gate = jnp.dot(u, wg_ref[...], preferred_element_type=F32)
        up = jnp.dot(u, wu_ref[...], preferred_element_type=F32)
        h = (_silu(gate) * up).astype(BF16)
        part = jnp.dot(h, wo_ref[...], preferred_element_type=F32)
        if assign:
            o_ref[...] = part
        else:
            o_ref[...] += part

    @pl.when(j == 0)
    def _():
        _modulate_into(_tile_rows([x_ref], n_ctx, first), u_ref, mod_ref, modc_ref, g2_ref, 3, first, n_ctx)
        hidden_tile(True)

    @pl.when(j > 0)
    def _():
        hidden_tile(False)

    @pl.when(j == nf - 1)
    def _():
        _gated_residual_into(o_ref, o_ref, _tile_rows([x_ref], n_ctx, first), mod_ref, modc_ref, g3_ref, 5,
                             first, n_ctx)


def _ffn(x, mod, g2, g3, w_in, w_out, layer, n_ctx, tm, n_batch, has_ctx=True):
    t, d = x.shape
    f = w_out.shape[1]
    tf = FFN_TILE
    assert f == FFN_DIM and w_in.shape[2] == 2 * f and f % tf == 0
    nf = f // tf
    tpb = t // n_batch // tm
    return pl.pallas_call(
        functools.partial(_ffn_kernel, n_ctx=n_ctx, tiles_per_batch=tpb, nf=nf, has_ctx=has_ctx),
        grid=(t // tm, nf),
        in_specs=[pl.BlockSpec((tm, d), lambda i, j: (i, 0))] + _mod_specs(tpb, n_batch)
        + [pl.BlockSpec((1, d), lambda i, j: (0, 0)),
           pl.BlockSpec((1, d), lambda i, j: (0, 0)),
           pl.BlockSpec((None, d, tf), lambda i, j: (layer, 0, j)),
           pl.BlockSpec((None, d, tf), lambda i, j: (layer, 0, nf + j)),
           pl.BlockSpec((None, tf, d), lambda i, j: (layer, j, 0))],
        out_specs=pl.BlockSpec((tm, d), lambda i, j: (i, 0)),
        out_shape=jax.ShapeDtypeStruct((t, d), F32),
        scratch_shapes=[pltpu.VMEM((tm, d), BF16)],
        compiler_params=_params("parallel", "arbitrary"),
        name="ffn",
    )(x, mod, mod, g2.reshape(1, d), g3.reshape(1, d), w_in, w_in, w_out)


def _rope_tables(n_ctx, n_lat):
    pos = jnp.arange(n_lat)
    row = (pos // GRID_W).astype(F32)
    col = (pos % GRID_W).astype(F32)
    inv_freq = ROPE_THETA ** (-jnp.arange(ROPE_PAIRS, dtype=F32) / ROPE_PAIRS)
    ang_row = row[:, None] * inv_freq
    ang_col = col[:, None] * inv_freq
    zero = jnp.zeros_like(ang_row)
    cos = jnp.concatenate([jnp.cos(ang_row)] * 2 + [jnp.cos(ang_col)] * 2, axis=1)
    s_lo = jnp.concatenate([-jnp.sin(ang_row), zero, -jnp.sin(ang_col), zero], axis=1)
    s_hi = jnp.concatenate([zero, jnp.sin(ang_row), zero, jnp.sin(ang_col)], axis=1)

    def full(tab, ctx_val):
        tab = jnp.concatenate([jnp.full((n_ctx, ATT_HEAD_DIM), ctx_val, F32), tab], axis=0)
        return jnp.concatenate([tab, tab], axis=1)

    return full(cos, 1.0), full(s_lo, 0.0), full(s_hi, 0.0)


def _rope(x, c, s_lo, s_hi):
    return x * c + pltpu.roll(x, LANES - ROPE_PAIRS, 1) * s_lo + pltpu.roll(x, ROPE_PAIRS, 1) * s_hi


def _attn_kernel(sink_ref, pq_ref, pk_ref, pv_ref, cq_ref, slq_ref, shq_ref, ck_ref, slk_ref, shk_ref,
                 o_ref, kx_ref, vt_ref, qt_ref, ot_ref, *, n_ctx, n_lat):
    t = pl.program_id(1)
    blk = ATT_BLOCK
    n_blocks = (n_ctx + n_lat) // blk
    ctx_blocks = n_ctx // blk
    hd = ATT_HEAD_DIM

    @pl.when(t == 0)
    def _():
        for i in range(n_blocks):
            rs = slice(i * blk, (i + 1) * blk)
            for c0 in range(0, ATT_KV_DIM, LANES):
                k = pk_ref[rs, c0:c0 + LANES].astype(F32)
                kx_ref[i, :, c0:c0 + LANES] = _rope(k, ck_ref[rs, :], slk_ref[rs, :], shk_ref[rs, :]).astype(BF16)
                vt_ref[i, c0:c0 + LANES, :] = pv_ref[rs, c0:c0 + LANES].astype(F32).T.astype(BF16)
        kx_ref[n_blocks] = jnp.zeros((blk, ATT_KV_DIM), BF16)
        vt_ref[n_blocks] = jnp.zeros((ATT_KV_DIM, blk), BF16)

    scale = ATT_HEAD_DIM ** -0.5 * LOG2_E
    for c0 in range(0, ATT_Q_DIM, LANES):
        q = pq_ref[:, 2 * ATT_KV_DIM + c0:2 * ATT_KV_DIM + c0 + LANES].astype(F32)
        qt_ref[c0:c0 + LANES, :] = (_rope(q, cq_ref[...], slq_ref[...], shq_ref[...]) * scale).T.astype(BF16)

    n = t - ctx_blocks
    key = lax.broadcasted_iota(jnp.int32, (blk, 2 * LANES), 0)
    qry = lax.broadcasted_iota(jnp.int32, (blk, 2 * LANES), 1) % blk
    blocks = [(i, None) for i in range(ctx_blocks)] + [
        (jnp.maximum(ctx_blocks + n - 1, 0), (key >= qry) & (n >= 1)),
        (jnp.maximum(ctx_blocks + n, 0), (key >= 0) & (n >= 0)),
        (ctx_blocks + n + 1, (key <= qry) & (n >= 0) & (n + 1 < n_lat // blk))]
    upper = lax.broadcasted_iota(jnp.int32, (1, 2 * LANES), 1) >= LANES
    zero = jnp.zeros((hd, LANES), BF16)

    for pair in range(ATT_KV_HEADS // 2):
        ps = slice(pair * LANES, (pair + 1) * LANES)
        for g in range(ATT_GQA):
            h0 = (2 * pair) * ATT_GQA + g
            h1 = (2 * pair + 1) * ATT_GQA + g
            w = jnp.concatenate([jnp.concatenate([qt_ref[h0 * hd:(h0 + 1) * hd, :], zero], axis=0),
                                 jnp.concatenate([zero, qt_ref[h1 * hd:(h1 + 1) * hd, :]], axis=0)], axis=1)
            sink = jnp.where(upper, sink_ref[h1], sink_ref[h0]) * LOG2_E
            scores = []
            m = sink
            for i, visible in blocks:
                s = jnp.dot(kx_ref[i, :, ps], w, preferred_element_type=F32)
                if visible is not None:
                    s = jnp.where(visible, s, NEG_INF)
                scores.append(s)
                m = jnp.maximum(m, jnp.max(s, axis=0, keepdims=True))
            den = jnp.exp2(sink - m)
            acc = jnp.zeros((LANES, 2 * LANES), F32)
            for (i, _), s in zip(blocks, scores):
                p = jnp.exp2(s - m)
                den = den + jnp.sum(p, axis=0, keepdims=True)
                acc = acc + jnp.dot(vt_ref[i, ps, :], p.astype(BF16), preferred_element_type=F32)
            ot_ref[h0 * hd:(h0 + 1) * hd, :] = acc[0:hd, 0:LANES] / den[:, 0:LANES]
            ot_ref[h1 * hd:(h1 + 1) * hd, :] = acc[hd:2 * hd, LANES:2 * LANES] / den[:, LANES:2 * LANES]

    for c0 in range(0, ATT_Q_DIM, LANES):
        o_ref[:, c0:c0 + LANES] = ot_ref[c0:c0 + LANES, :].T.astype(o_ref.dtype)


def _attention(proj, sink, tables, n_ctx, n_lat, n_batch):
    assert ATT_WINDOW == ATT_BLOCK and n_ctx % ATT_BLOCK == 0 and n_lat % ATT_BLOCK == 0
    assert proj.shape[1] == 2 * ATT_KV_DIM + ATT_Q_DIM
    rows = n_ctx + n_lat
    nq = rows // ATT_BLOCK
    cos, s_lo, s_hi = tables
    qtab = pl.BlockSpec((ATT_BLOCK, LANES), lambda b, t: (t, 0))
    ktab = pl.BlockSpec((rows, LANES), lambda b, t: (0, 0))
    return pl.pallas_call(
        functools.partial(_attn_kernel, n_ctx=n_ctx, n_lat=n_lat),
        grid=(n_batch, nq),
        in_specs=[pl.BlockSpec(memory_space=pltpu.SMEM),
                  pl.BlockSpec((ATT_BLOCK, proj.shape[1]), lambda b, t: (b * nq + t, 0)),
                  pl.BlockSpec((rows, ATT_KV_DIM), lambda b, t: (b, 0)),
                  pl.BlockSpec((rows, ATT_KV_DIM), lambda b, t: (b, 1)),
                  qtab, qtab, qtab, ktab, ktab, ktab],
        out_specs=pl.BlockSpec((ATT_BLOCK, ATT_Q_DIM), lambda b, t: (b * nq + t, 0)),
        out_shape=jax.ShapeDtypeStruct((proj.shape[0], ATT_Q_DIM), BF16),
        scratch_shapes=[pltpu.VMEM((nq + 1, ATT_BLOCK, ATT_KV_DIM), BF16),
                        pltpu.VMEM((nq + 1, ATT_KV_DIM, ATT_BLOCK), BF16),
                        pltpu.VMEM((ATT_Q_DIM, ATT_BLOCK), BF16),
                        pltpu.VMEM((ATT_Q_DIM, ATT_BLOCK), F32)],
        compiler_params=_params("parallel", "arbitrary"),
        name="attention",
    )(sink.astype(F32), proj, proj, proj, cos, s_lo, s_hi, cos, s_lo, s_hi)


def _softplus(t):
    return jnp.maximum(t, 0.0) + jnp.log(1.0 + jnp.exp(-jnp.abs(t)))


def _split_bf16(t):
    hi = t.astype(BF16)
    return hi, (t - hi.astype(F32)).astype(BF16)


def _select_rows(sel, t):
    both = jnp.dot(sel, jnp.concatenate(_split_bf16(t), axis=1), preferred_element_type=F32)
    return both[:, 0:LANES] + both[:, LANES:2 * LANES]


def _conv_silu_into(src_ref, w_ref, b_ref, dst_ref, n_ctx):
    rows, width = src_ref.shape
    t = n_ctx
    ri = lax.broadcasted_iota(jnp.int32, (t, t), 0)
    ci = lax.broadcasted_iota(jnp.int32, (t, t), 1)
    down = (ri == ci + 1).astype(BF16)
    up = (ri + 1 == ci).astype(BF16)
    row8 = lax.broadcasted_iota(jnp.int32, (8, width), 0)
    w = 0.5 * w_ref[...]
    b = 0.5 * b_ref[...]
    for r0 in range(0, rows, t):
        x = src_ref[r0:r0 + t, :]
        prv = jnp.dot(down, x, preferred_element_type=F32)
        nxt = jnp.dot(up, x, preferred_element_type=F32)
        if r0 not in (0, n_ctx):
            edge = jnp.broadcast_to(src_ref[r0 - 1:r0, :].astype(F32), (8, width))
            prv = jnp.concatenate([jnp.where(row8 == 0, edge, prv[0:8]), prv[8:]], axis=0)
        if r0 + t not in (n_ctx, rows):
            edge = jnp.broadcast_to(src_ref[r0 + t:r0 + t + 1, :].astype(F32), (8, width))
            nxt = jnp.concatenate([nxt[:t - 8], jnp.where(row8 == 7, edge, nxt[t - 8:])], axis=0)
        h = b + prv * w[0:1, :] + x.astype(F32) * w[1:2, :] + nxt * w[2:3, :]
        dst_ref[r0:r0 + t, :] = (h + h * jnp.tanh(h)).astype(dst_ref.dtype)


def _ssd_fill_tables(tr, tri_ref, expand_ref):
    q = SSD_CHUNK
    ri = lax.broadcasted_iota(jnp.int32, (tr, tr), 0)
    ci = lax.broadcasted_iota(jnp.int32, (tr, tr), 1)
    same = (ri // q) == (ci // q)
    tri_ref[0:tr, :] = (same & (ri >= ci)).astype(BF16)
    tri_ref[tr:2 * tr, :] = (same & (ri <= ci)).astype(BF16)
    src = lax.broadcasted_iota(jnp.int32, (2 * LANES, SSD_GROUP_DIM), 0) % LANES
    dst = lax.broadcasted_iota(jnp.int32, (2 * LANES, SSD_GROUP_DIM), 1) // SSD_HEAD_DIM
    for d in range(2):
        expand_ref[d] = (src == d * SSD_GROUP_HEADS + dst).astype(BF16)


def _ssd_decays(r0, tr, rev, xa, dtc, cum, expand, cumx_ref, xdec_ref):
    q = SSD_CHUNK
    npc = tr // q
    cum_hi, cum_lo = _split_bf16(cum)
    cum = cum_hi.astype(F32) + cum_lo.astype(F32)
    dt_hi, dt_lo = _split_bf16(dtc)
    wide = jnp.dot(jnp.concatenate([jnp.concatenate([cum_hi, cum_lo], axis=1),
                                    jnp.concatenate([dt_hi, dt_lo], axis=1)], axis=0),
                   expand, preferred_element_type=F32)
    cum_x = wide[0:tr]
    dt_x = wide[tr:2 * tr]
    tot_x = _rows_from(cum_x, [c * q + (0 if rev else q - 1) for c in range(npc)], q)
    xdt = xa * dt_x
    cumx_ref[pl.ds(r0, tr), :] = cum_x
    xdec_ref[pl.ds(r0, tr), :] = (xdt * jnp.exp2(tot_x - cum_x)).astype(BF16)
    return cum, cum_x, xdt


def _ssd_intra(t, tr, tables, xa_ref, ba_ref, ca_ref, dtv_ref, da_ref, y_ref, fwd_refs, bwd_refs):
    q = SSD_CHUNK
    assert 2 * q == LANES and 2 * SSD_HEAD_DIM == LANES
    tri_ref, expand_ref = tables
    r0 = pl.multiple_of(t * tr, tr)
    xa = xa_ref[pl.ds(r0, tr), :]
    bc = ba_ref[pl.ds(r0, tr), :]
    cc = ca_ref[pl.ds(r0, tr), :]
    dtc = dtv_ref[pl.ds(r0, tr), :]
    cums = _select_rows(tri_ref[...], da_ref[pl.ds(r0, tr), :])
    scans = [(False, 0) + _ssd_decays(r0, tr, False, xa, dtc, cums[0:tr], expand_ref[0], *fwd_refs),
             (True, SSD_GROUP_HEADS) + _ssd_decays(r0, tr, True, xa, dtc, cums[tr:2 * tr], expand_ref[1], *bwd_refs)]

    row = lax.broadcasted_iota(jnp.int32, (q, LANES), 0)
    lane = lax.broadcasted_iota(jnp.int32, (q, LANES), 1)
    upper = lane >= q
    key = jnp.where(upper, lane - q, lane)
    for c in range(tr // q):
        cs = slice(c * q, (c + 1) * q)
        cb2 = lax.dot_general(cc[cs], jnp.concatenate([bc[cs], bc[cs]], axis=0), _NT, preferred_element_type=F32)
        cum_rows = [jnp.concatenate([cum[cs], pltpu.roll(cum[cs], LANES - 1, 1)], axis=0).T
                    for _, _, cum, _, _ in scans]
        ys = []
        for pr in range(SSD_GROUP_HEADS // 2):
            sl = slice(pr * LANES, (pr + 1) * LANES)
            weights, values = [], []
            for (rev, lane0, _, cum_x, xdt), rows in zip(scans, cum_rows):
                le = lane0 + 2 * pr
                seen = (row <= key) if rev else (row >= key)
                decay = jnp.exp2(jnp.where(seen, cum_x[cs, sl] - rows[le:le + 1, :], NEG_INF))
                weights.append((cb2 * decay).astype(BF16))
                xp = xdt[cs, sl]
                values.append(jnp.concatenate([jnp.where(upper, 0.0, xp), jnp.where(upper, xp, 0.0)],
                                              axis=0).astype(BF16))
            ys.append(jnp.dot(jnp.concatenate(weights, axis=1), jnp.concatenate(values, axis=0),
                              preferred_element_type=F32))
        y_ref[pl.ds(r0 + c * q, q), :] = jnp.concatenate(ys, axis=1)


def _ssd_state_step(c, rev, ba_ref, ca_ref, y_ref, cumx_ref, xdec_ref, s_ref):
    q = SSD_CHUNK
    r0 = pl.multiple_of(c * q, q)
    cum_x = cumx_ref[pl.ds(r0, q), :]
    tot_x = cum_x[0:1, :] if rev else cum_x[q - 1:q, :]
    state = s_ref[...]
    y_ref[pl.ds(r0, q), :] += (jnp.dot(ca_ref[pl.ds(r0, q), :], state.astype(BF16), preferred_element_type=F32)
                               * jnp.exp2(cum_x))
    upd = lax.dot_general(ba_ref[pl.ds(r0, q), :], xdec_ref[pl.ds(r0, q), :], _TN, preferred_element_type=F32)
    s_ref[...] = state * jnp.exp2(tot_x) + upd


def _ssd_kernel(x_ref, z_ref, b_ref, c_ref, dt_ref, cwx_ref, cwb_ref, cwc_ref, cbx_ref, cbb_ref, cbc_ref,
                dtb_ref, alog_ref, dskip_ref, nw_ref, o_ref,
                xa_ref, ba_ref, ca_ref, dtv_ref, da_ref, y_ref, cumxf_ref, cumxb_ref, xdecf_ref, xdecb_ref,
                s_ref, sb_ref, tri_ref, expand_ref, *, n_ctx, n_lat, tile):
    _conv_silu_into(x_ref, cwx_ref, cbx_ref, xa_ref, n_ctx)
    _conv_silu_into(b_ref, cwb_ref, cbb_ref, ba_ref, n_ctx)
    _conv_silu_into(c_ref, cwc_ref, cbc_ref, ca_ref, n_ctx)
    dtv = _softplus(dt_ref[...].astype(F32) + dtb_ref[...])
    dtv_ref[...] = dtv
    da_ref[...] = dtv * (-jnp.exp(alog_ref[...]) * LOG2_E)

    seq = (xa_ref, ba_ref, ca_ref, dtv_ref, da_ref)
    fwd = (cumxf_ref, xdecf_ref)
    bwd = (cumxb_ref, xdecb_ref)

    tables = (tri_ref, expand_ref)
    _ssd_fill_tables(tile, *tables)

    def intra_step(t, carry):
        _ssd_intra(t, tile, tables, *seq, y_ref, fwd, bwd)
        return carry

    lax.fori_loop(0, (n_ctx + n_lat) // tile, intra_step, 0, unroll=TILE_UNROLL)

    n_c = n_ctx // SSD_CHUNK
    n_l = n_lat // SSD_CHUNK

    def ctx_step(s, carry):
        _ssd_state_step(s, False, ba_ref, ca_ref, y_ref, *fwd, s_ref)
        _ssd_state_step(n_c - 1 - s, True, ba_ref, ca_ref, y_ref, *bwd, sb_ref)
        return carry

    def lat_step(s, carry):
        _ssd_state_step(n_c + s, False, ba_ref, ca_ref, y_ref, *fwd, s_ref)
        _ssd_state_step(n_c + n_l - 1 - s, True, ba_ref, ca_ref, y_ref, *bwd, sb_ref)
        return carry

    s_ref[...] = jnp.zeros(s_ref.shape, F32)
    sb_ref[...] = jnp.zeros(sb_ref.shape, F32)
    lax.fori_loop(0, n_c, ctx_step, 0, unroll=SCAN_UNROLL)
    lax.fori_loop(0, n_l, lat_step, 0, unroll=SCAN_UNROLL)

    for r0 in range(0, n_ctx + n_lat, n_ctx):
        y = y_ref[r0:r0 + n_ctx, :] + xa_ref[r0:r0 + n_ctx, :] * dskip_ref[...]
        t = y * _silu(z_ref[r0:r0 + n_ctx, :].astype(F32))
        o_ref[r0:r0 + n_ctx, :] = (_rms_rows(t) * nw_ref[...]).astype(o_ref.dtype)


def _ssd_layout(w_in, conv_w, conv_b, dt_bias, a_log, d_skip):
    n_state = SSD_CONV_DIM + 2 * SSD_HEADS
    gh = SSD_GROUP_HEADS

    def dt_blocks(t):
        lead = t.shape[:-1]
        t = t.reshape(lead + (2, SSD_GROUPS, gh))
        t = jnp.moveaxis(t, -3, -2).reshape(lead + (SSD_GROUPS, 2 * gh))
        t = jnp.pad(t, [(0, 0)] * (len(lead) + 1) + [(0, LANES - 2 * gh)])
        return t.reshape(lead + (SSD_GROUPS * LANES,))

    w_in = w_in.astype(BF16)
    w = jnp.concatenate([w_in[:, :SSD_D_INNER], w_in[:, n_state:], w_in[:, SSD_D_INNER:SSD_CONV_DIM],
                         dt_blocks(w_in[:, SSD_CONV_DIM:n_state])], axis=1)
    return (w, conv_w, conv_b.reshape(1, -1), dt_blocks(dt_bias.reshape(-1)).reshape(1, -1),
            dt_blocks(a_log.reshape(-1)).reshape(1, -1), jnp.repeat(d_skip, SSD_HEAD_DIM).reshape(1, -1))


def _ssd(proj, conv_w, conv_b, dt_bias, a_log, d_skip, norm_w, n_ctx, n_lat, n_batch):
    rows = n_ctx + n_lat
    gd = SSD_GROUP_DIM
    tile = _scan_tile(rows, SSD_CHUNK)
    xb = SSD_D_INNER // gd
    bb = 2 * SSD_D_INNER // LANES
    sec = lambda width, off: pl.BlockSpec((rows, width), lambda b, g: (b, off + g))
    par = lambda r, width, off: pl.BlockSpec((r, width), lambda b, g: (0, off + g))
    return pl.pallas_call(
        functools.partial(_ssd_kernel, n_ctx=n_ctx, n_lat=n_lat, tile=tile),
        grid=(n_batch, SSD_GROUPS),
        in_specs=[sec(gd, 0), sec(gd, xb), sec(LANES, bb), sec(LANES, bb + SSD_GROUPS),
                  sec(LANES, bb + 2 * SSD_GROUPS),
                  par(3, gd, 0), par(3, LANES, SSD_D_INNER // LANES), par(3, LANES, SSD_D_INNER // LANES + SSD_GROUPS),
                  par(1, gd, 0), par(1, LANES, SSD_D_INNER // LANES), par(1, LANES, SSD_D_INNER // LANES + SSD_GROUPS),
                  par(1, LANES, 0), par(1, LANES, 0), par(1, gd, 0), par(1, gd, 0)],
        out_specs=pl.BlockSpec((rows, gd), lambda b, g: (b, g)),
        out_shape=jax.ShapeDtypeStruct((proj.shape[0], SSD_D_INNER), BF16),
        scratch_shapes=[pltpu.VMEM((rows, gd), F32),
                        pltpu.VMEM((rows, SSD_STATE), BF16),
                        pltpu.VMEM((rows, SSD_STATE), BF16),
                        pltpu.VMEM((rows, LANES), F32),
                        pltpu.VMEM((rows, LANES), F32),
                        pltpu.VMEM((rows, gd), F32),
                        pltpu.VMEM((rows, gd), F32),
                        pltpu.VMEM((rows, gd), F32),
                        pltpu.VMEM((rows, gd), BF16),
                        pltpu.VMEM((rows, gd), BF16),
                        pltpu.VMEM((SSD_STATE, gd), F32),
                        pltpu.VMEM((SSD_STATE, gd), F32),
                        pltpu.VMEM((2 * tile, tile), BF16),
                        pltpu.VMEM((2, 2 * LANES, gd), BF16)],
        compiler_params=_params("parallel", "parallel"),
        name="ssd",
    )(proj, proj, proj, proj, proj, conv_w, conv_w, conv_w, conv_b, conv_b, conv_b,
      dt_bias, a_log, d_skip, norm_w.reshape(1, -1))


HG_OWN, HG_EARLIER, HG_LATER = 1.0, 2.0, 3.0


def _hgrn_fill_tables(tr, tri_ref, kind_ref, place_ref):
    ch, sub = HG_CHUNK, HG_SUB
    ri = lax.broadcasted_iota(jnp.int32, (tr, tr), 0)
    ci = lax.broadcasted_iota(jnp.int32, (tr, tr), 1)
    same = (ri // ch) == (ci // ch)
    tri_ref[0] = (same & (ri >= ci)).astype(BF16)
    tri_ref[1] = (same & (ri <= ci)).astype(BF16)
    rb, cb = ri // sub, ci // sub
    kind_ref[...] = jnp.where(rb == cb, HG_OWN,
                              jnp.where(same & (cb < rb), HG_EARLIER, jnp.where(same & (cb > rb), HG_LATER, 0.0)))
    src_key = lax.broadcasted_iota(jnp.int32, (sub * LANES, LANES), 0) // LANES
    dst_key = lax.broadcasted_iota(jnp.int32, (sub * LANES, LANES), 1) % sub
    place_ref[...] = (src_key == dst_key).astype(BF16)


def _hgrn_intra(t, tr, rev, tables, lg_ref, k_ref, q_ref, v_ref, o_ref, qst_ref, upd_ref, dec_ref):
    ch, sub = HG_CHUNK, HG_SUB
    npc, nb, nblk = tr // ch, ch // sub, tr // sub
    tri_ref, kind_ref, place_ref = tables
    r0 = pl.multiple_of(t * tr, tr)
    lg = lg_ref[pl.ds(r0, tr), :]
    k = k_ref[pl.ds(r0, tr), :]
    q = q_ref[pl.ds(r0, tr), :]
    v = v_ref[pl.ds(r0, tr), :]
    tri = tri_ref[1 if rev else 0]
    cum = _select_rows(tri, lg)

    last = [c * ch + (0 if rev else ch - 1) for c in range(npc)]
    qst_ref[pl.ds(r0, tr), :] = (q * jnp.exp2(cum)).astype(BF16)
    k_end = (k * jnp.exp2(_rows_from(cum, last, ch) - cum)).astype(BF16)
    zero_chunk = jnp.zeros((ch, LANES), BF16)
    k_end_by_chunk = jnp.concatenate(
        [jnp.concatenate([k_end[c2 * ch:(c2 + 1) * ch] if c2 == c else zero_chunk for c2 in range(npc)], axis=0)
         for c in range(npc)], axis=1)
    upd = lax.dot_general(v, k_end_by_chunk, _TN, preferred_element_type=F32)
    for c in range(npc):
        chunk = t * npc + c
        dec_ref[pl.ds(chunk, 1), :] = jnp.exp2(cum[last[c]:last[c] + 1, :])
        upd_ref[pl.ds(pl.multiple_of(chunk * LANES, LANES), LANES), :] = upd[:, c * LANES:(c + 1) * LANES]

    def own_edge(j):
        i = j % nb
        if rev:
            return None if i == nb - 1 else (j + 1) * sub
        return None if i == 0 else j * sub - 1

    qs = q * jnp.exp2(cum - _rows_from(cum, [own_edge(j) for j in range(nblk)], sub))
    queries, keys = {}, {}
    for i in (range(nb - 1) if rev else range(1, nb)):
        queries[i] = jnp.concatenate([qs[(c * nb + i) * sub:(c * nb + i + 1) * sub] for c in range(npc)],
                                     axis=0).astype(BF16)
        pieces = []
        for c in range(npc):
            base = c * ch
            e = base + ((i + 1) * sub if rev else i * sub - 1)
            lo, hi = ((i + 1) * sub, ch) if rev else (0, i * sub)
            part = k[base + lo:base + hi] * jnp.exp2(cum[e:e + 1, :] - cum[base + lo:base + hi])
            pad = jnp.zeros((ch - (hi - lo), LANES), F32)
            pieces += [pad, part] if rev else [part, pad]
        keys[i] = jnp.concatenate(pieces, axis=0).astype(BF16)
    weights = {}
    order = sorted(queries)
    nq = npc * sub
    zero_q = jnp.zeros((nq, LANES), BF16)
    for i1, i2 in zip(order[0::2], order[1::2] + [None]):
        if i2 is None:
            weights[i1] = lax.dot_general(queries[i1], keys[i1], _NT, preferred_element_type=F32)
            continue
        both = lax.dot_general(
            jnp.concatenate([jnp.concatenate([queries[i1], zero_q], axis=1),
                             jnp.concatenate([zero_q, queries[i2]], axis=1)], axis=0),
            jnp.concatenate([keys[i1], keys[i2]], axis=1), _NT, preferred_element_type=F32)
        weights[i1], weights[i2] = both[0:nq], both[nq:2 * nq]
    zero_rows = jnp.zeros((sub, tr), F32)
    a_off = jnp.concatenate([weights[j % nb][(j // nb) * sub:(j // nb + 1) * sub] if j % nb in weights
                             else zero_rows for j in range(nblk)], axis=0)

    row8 = lax.broadcasted_iota(jnp.int32, (sub, LANES), 0)
    prods = []
    for s in range(sub):
        idx = [j * sub + s for j in range(nblk)]
        unseen = jnp.where((row8 <= s) if rev else (row8 >= s), 0.0, NEG_INF)
        w = jnp.exp2(cum - _rows_from(cum, idx, sub) + jnp.concatenate([unseen] * nblk, axis=0))
        prods.append((q * _rows_from(k, idx, sub) * w).astype(BF16))
    a_own = jnp.dot(jnp.concatenate(prods, axis=1), place_ref[...], preferred_element_type=F32)
    kind = kind_ref[...]
    a = jnp.where(kind == HG_OWN, jnp.concatenate([a_own] * (tr // LANES), axis=1),
                  jnp.where(kind == (HG_LATER if rev else HG_EARLIER), a_off, 0.0))
    o_ref[pl.ds(r0, tr), :] = jnp.dot(a.astype(BF16), v, preferred_element_type=F32)


def _hgrn_state_step(c, st, upd_ref, dec_ref, sin_ref):
    rows = pl.ds(pl.multiple_of(c * LANES, LANES), LANES)
    sin_ref[rows, :] = st.astype(BF16)
    return st * dec_ref[pl.ds(c, 1), :] + upd_ref[rows, :]


def _hgrn_readout(c, qst_ref, sin_ref):
    ch = HG_CHUNK
    return lax.dot_general(qst_ref[pl.ds(pl.multiple_of(c * ch, ch), ch), :],
                           sin_ref[pl.ds(pl.multiple_of(c * LANES, LANES), LANES), :], _NT,
                           preferred_element_type=F32)


def _hgrn_kernel(ff_ref, fb_ref, v_ref, q_ref, gate_ref, lb_ref, nw_ref, out_ref,
                 lgf_ref, lgb_ref, kf_ref, kb_ref, qa_ref, o_ref, ob_ref, qstf_ref, qstb_ref,
                 updf_ref, updb_ref, decf_ref, decb_ref, sinf_ref, sinb_ref, tri_ref, kind_ref, place_ref,
                 *, n_ctx, n_lat, tile):
    tables = (tri_ref, kind_ref, place_ref)
    _hgrn_fill_tables(tile, *tables)
    for d, (f_ref, lg_ref, k_ref) in enumerate(((ff_ref, lgf_ref, kf_ref), (fb_ref, lgb_ref, kb_ref))):
        lb = lb_ref[d:d + 1, :]
        g = lb + (1.0 - lb) * jax.nn.sigmoid(f_ref[...].astype(F32))
        lg_ref[...] = jnp.log(g) * LOG2_E
        k_ref[...] = 1.0 - g
    qa_ref[...] = _silu(q_ref[...].astype(F32))

    fwd = (qstf_ref, updf_ref, decf_ref)
    bwd = (qstb_ref, updb_ref, decb_ref)
    fwd_scan = (updf_ref, decf_ref, sinf_ref)
    bwd_scan = (updb_ref, decb_ref, sinb_ref)

    def intra_step(t, carry):
        _hgrn_intra(t, tile, False, tables, lgf_ref, kf_ref, qa_ref, v_ref, o_ref, *fwd)
        _hgrn_intra(t, tile, True, tables, lgb_ref, kb_ref, qa_ref, v_ref, ob_ref, *bwd)
        return carry

    lax.fori_loop(0, (n_ctx + n_lat) // tile, intra_step, 0, unroll=TILE_UNROLL)

    n_c = n_ctx // HG_CHUNK
    n_l = n_lat // HG_CHUNK

    def ctx_step(s, carry):
        return (_hgrn_state_step(s, carry[0], *fwd_scan),
                _hgrn_state_step(n_c - 1 - s, carry[1], *bwd_scan))

    def lat_step(s, carry):
        return (_hgrn_state_step(n_c + s, carry[0], *fwd_scan),
                _hgrn_state_step(n_c + n_l - 1 - s, carry[1], *bwd_scan))

    zero = jnp.zeros((LANES, LANES), F32)
    carry = lax.fori_loop(0, n_c, ctx_step, (zero, zero), unroll=SCAN_UNROLL)
    lax.fori_loop(0, n_l, lat_step, carry, unroll=SCAN_UNROLL)

    def read_step(c, carry):
        rows = pl.ds(pl.multiple_of(c * HG_CHUNK, HG_CHUNK), HG_CHUNK)
        o_ref[rows, :] += _hgrn_readout(c, qstf_ref, sinf_ref) + _hgrn_readout(c, qstb_ref, sinb_ref)
        return carry

    lax.fori_loop(0, n_c + n_l, read_step, 0, unroll=READ_UNROLL)

    o = o_ref[...] + ob_ref[...]
    out_ref[...] = (_rms_rows(o) * nw_ref[...] * _silu(gate_ref[...].astype(F32))).astype(out_ref.dtype)


def _hgrn(proj, lower, norm_w, n_ctx, n_lat, n_batch):
    rows = n_ctx + n_lat
    sec = lambda off: pl.BlockSpec((rows, LANES), lambda b, h: (b, off * HG_HEADS + h))
    seq = lambda dt: pltpu.VMEM((rows, LANES), dt)
    n_chunks = rows // HG_CHUNK
    upd = pltpu.VMEM((n_chunks * LANES, LANES), F32)
    dec = pltpu.VMEM((-(-n_chunks // 8) * 8, LANES), F32)
    sin = pltpu.VMEM((n_chunks * LANES, LANES), BF16)
    tile = _scan_tile(rows, HG_CHUNK)
    tables = [pltpu.VMEM((2, tile, tile), BF16), pltpu.VMEM((tile, tile), F32),
              pltpu.VMEM((HG_SUB * LANES, LANES), BF16)]
    return pl.pallas_call(
        functools.partial(_hgrn_kernel, n_ctx=n_ctx, n_lat=n_lat, tile=tile),
        grid=(n_batch, HG_HEADS),
        in_specs=[sec(0), sec(1), sec(2), sec(3), sec(4),
                  pl.BlockSpec((2, LANES), lambda b, h: (0, h)),
                  pl.BlockSpec((1, LANES), lambda b, h: (0, h))],
        out_specs=pl.BlockSpec((rows, LANES), lambda b, h: (b, h)),
        out_shape=jax.ShapeDtypeStruct((proj.shape[0], HG_VAL), BF16),
        scratch_shapes=[seq(F32), seq(F32), seq(F32), seq(F32), seq(F32), seq(F32), seq(F32),
                        seq(BF16), seq(BF16), upd, upd, dec, dec, sin, sin] + tables,
        compiler_params=_params("parallel", "parallel"),
        name="hgrn2",
    )(proj, proj, proj, proj, proj, lower, norm_w.reshape(1, -1))


def _hgrn_lower_bounds(lb_raw):
    p = jax.nn.softmax(lb_raw.astype(F32), axis=1)
    return jnp.cumsum(p, axis=1) - p[:, :1]


def kernel(x, c, ctx, c_ctx, ada_w, ada_b, norm_g, ffn_w_in, ffn_w_out, attn_w_in, attn_w_out, attn_sink,
           ssd_w_in, ssd_conv_w, ssd_conv_b, ssd_dt_bias, ssd_a_log, ssd_d, ssd_norm_w, ssd_w_out,
           hgrn_w_in, hgrn_lb, hgrn_norm_w, hgrn_w_out):
    n_batch, n_lat, d = x.shape
    n_ctx = ctx.shape[1]
    rows = n_ctx + n_lat
    tm = _token_tile(n_ctx, n_ctx + n_lat)
    tm_lat = _token_tile(n_ctx, n_lat)

    r = (ctx.reshape(n_batch * n_ctx, d), x.reshape(n_batch * n_lat, d))
    n_cond = -(-(n_batch + 1) // 8) * 8
    c_rows = jnp.concatenate([c, c_ctx[None, :], jnp.zeros((n_cond - n_batch - 1, d), F32)], axis=0)
    mods = _ada_mods(c_rows, ada_w, ada_b)
    tables = _rope_tables(n_ctx, n_lat)
    lower = _hgrn_lower_bounds(hgrn_lb)
    ffn_w_in_bf16 = ffn_w_in.astype(BF16)
    ffn_w_out_bf16 = ffn_w_out.astype(BF16)

    for i in range(DEPTH):
        kind, j = i % N_MIXERS, i // N_MIXERS
        mod = mods[i]
        if kind == 0:
            proj = _modproj(r, mod, norm_g[i, 0], attn_w_in[j].astype(BF16), n_ctx, tm, n_batch)
            y = _attention(proj, attn_sink[j], tables, n_ctx, n_lat, n_batch)
            w_out = attn_w_out[j]
        elif kind == 1:
            w_in, conv_w, conv_b, dt_bias, a_log, d_skip = _ssd_layout(
                ssd_w_in[j], ssd_conv_w[j], ssd_conv_b[j], ssd_dt_bias[j], ssd_a_log[j], ssd_d[j])
            proj = _modproj(r, mod, norm_g[i, 0], w_in, n_ctx, tm, n_batch)
            y = _ssd(proj, conv_w, conv_b, dt_bias, a_log, d_skip, ssd_norm_w[j], n_ctx, n_lat, n_batch)
            w_out = ssd_w_out[j]
        else:
            proj = _modproj(r, mod, norm_g[i, 0], hgrn_w_in[j].astype(BF16), n_ctx, tm, n_batch)
            y = _hgrn(proj, lower[:, i], hgrn_norm_w[j], n_ctx, n_lat, n_batch)
            w_out = hgrn_w_out[j]
        ffn_w = (ffn_w_in_bf16, ffn_w_out_bf16, i)
        if i < DEPTH - 1:
            r = _outproj(y, w_out.astype(BF16), r, mod, norm_g[i, 1], n_ctx, tm, n_batch)
            r = _ffn(r, mod, norm_g[i, 2], norm_g[i, 3], *ffn_w, n_ctx, tm, n_batch)
        else:
            r = _outproj(y, w_out.astype(BF16), r, mod, norm_g[i, 1], n_ctx, tm_lat, n_batch, lat_rows=n_lat)
            r = _ffn(r, mod, norm_g[i, 2], norm_g[i, 3], *ffn_w, n_ctx, tm_lat, n_batch, has_ctx=False)
    return r.reshape(n_batch, n_lat, d)
```
